```python
import jax, jax.numpy as jnp
from jax import lax
import numpy as np

D_MODEL = 2048
BATCH = 8
SEQ = 4096
DEPTH = 4

N_MIXERS = 3
N_A_LAYERS = (DEPTH + 2) // 3
N_B_LAYERS = (DEPTH + 1) // 3
N_C_LAYERS = DEPTH // 3
GLU_EXPAND = 2
CONV_WIDTH = 31
SHORT_CONV_WIDTH = 3
POOL_WINDOWS = (2, 4, 8, 16)
N_POOL_GROUPS = len(POOL_WINDOWS)
POOL_GROUP_DIM = D_MODEL // N_POOL_GROUPS
N_EXPERTS = 64
TOP_K = 8
N_EXPERT_GROUPS = 8
EXPERTS_PER_GROUP = N_EXPERTS // N_EXPERT_GROUPS
TOPK_GROUPS = 4
D_EXPERT = 3 * D_MODEL // 16
D_SHARED = D_EXPERT
ROUTED_SCALE = 2.5
EXPERT_BLOCK = 512
PLE_DIM = 256
LN_EPS = 1e-5
DEEPNORM_ALPHA = (2 * DEPTH) ** 0.25
DEEPNORM_BETA = (8 * DEPTH) ** -0.25

kernel_name = "hybrid_conv_shortconv_pool_moe_deepnorm"


def layer_norm(x, g, b):
    xf = x.astype(jnp.float32)
    mu = jnp.mean(xf, axis=-1, keepdims=True)
    var = jnp.mean(jnp.square(xf - mu), axis=-1, keepdims=True)
    y = (xf - mu) * lax.rsqrt(var + LN_EPS)
    return (y * g.astype(jnp.float32) + b.astype(jnp.float32)).astype(x.dtype)


def causal_depthwise_conv(x, w):
    width, ch = w.shape
    return lax.conv_general_dilated(
        x, w[:, None, :].astype(x.dtype), window_strides=(1,),
        padding=[(width - 1, 0)], dimension_numbers=("NWC", "WIO", "NWC"),
        feature_group_count=ch)


def swiglu(x, w_gate, w_up, w_down):
    return (jax.nn.silu(x @ w_gate) * (x @ w_up)) @ w_down


def conformer_conv(h, w_in, b_in, w_dw, b_dw, ln_g, ln_b, w_out, b_out):
    a, g = jnp.split(h @ w_in + b_in, 2, axis=-1)
    v = causal_depthwise_conv(a * jax.nn.sigmoid(g), w_dw) + b_dw
    v = jax.nn.silu(layer_norm(v, ln_g, ln_b))
    return v @ w_out + b_out


def short_gated_conv(h, w_in, w_conv, w_out):
    b_gate, c_gate, v = jnp.split(h @ w_in, 3, axis=-1)
    return (b_gate * causal_depthwise_conv(c_gate * v, w_conv)) @ w_out


def multiscale_pool(h, w_grp, scale):
    bsz, s, d = h.shape
    xf = h.reshape(bsz, s, N_POOL_GROUPS, POOL_GROUP_DIM).astype(jnp.float32)
    csum = jnp.cumsum(xf, axis=1)
    pos = jnp.arange(1, s + 1, dtype=jnp.float32)
    pooled = []
    for g, w in enumerate(POOL_WINDOWS):
        c = csum[:, :, g]
        lo = jnp.pad(c, ((0, 0), (w, 0), (0, 0)))[:, :s]
        cnt = jnp.minimum(pos, float(w))[None, :, None]
        pooled.append((c - lo) / cnt - xf[:, :, g])
    z = jnp.stack(pooled, axis=2).astype(h.dtype)
    y = jnp.einsum("bsgc,gcd->bsgd", z, w_grp).reshape(bsz, s, d)
    return y * scale


def route(xt, w_router, b_router):
    t = xt.shape[0]
    scores = jax.nn.sigmoid((xt @ w_router).astype(jnp.float32))
    biased = scores + b_router.astype(jnp.float32)
    grp = biased.reshape(t, N_EXPERT_GROUPS, EXPERTS_PER_GROUP)
    grp_score = lax.top_k(grp, 2)[0].sum(-1)
    _, gidx = lax.top_k(grp_score, TOPK_GROUPS)
    gmask = jnp.zeros((t, N_EXPERT_GROUPS), bool).at[jnp.arange(t)[:, None], gidx].set(True)
    emask = jnp.repeat(gmask, EXPERTS_PER_GROUP, axis=1)
    _, eidx = lax.top_k(jnp.where(emask, biased, -jnp.inf), TOP_K)
    gate = jnp.take_along_axis(scores, eidx, axis=1)
    gate = gate / jnp.sum(gate, axis=-1, keepdims=True) * ROUTED_SCALE
    return eidx, gate


def routed_experts(xt, eidx, gate, w_gate, w_up, w_down):
    t, d = xt.shape
    tk = t * TOP_K
    flat_e = eidx.reshape(tk)
    flat_tok = jnp.repeat(jnp.arange(t, dtype=jnp.int32), TOP_K)
    flat_w = gate.reshape(tk)
    order = jnp.argsort(flat_e)
    se = flat_e[order]
    counts = jnp.bincount(flat_e, length=N_EXPERTS)
    starts = jnp.cumsum(counts) - counts
    pcounts = (counts + EXPERT_BLOCK - 1) // EXPERT_BLOCK * EXPERT_BLOCK
    pends = jnp.cumsum(pcounts)
    pstarts = pends - pcounts
    dest = pstarts[se] + (jnp.arange(tk) - starts[se])
    n_blocks = -(-tk // EXPERT_BLOCK) + N_EXPERTS
    n_rows = n_blocks * EXPERT_BLOCK
    row_tok = jnp.full((n_rows,), t, dtype=jnp.int32).at[dest].set(flat_tok[order])
    row_w = jnp.zeros((n_rows,), jnp.float32).at[dest].set(flat_w[order])
    blk_e = jnp.minimum(
        jnp.searchsorted(pends, jnp.arange(n_blocks) * EXPERT_BLOCK, side="right"),
        N_EXPERTS - 1)
    xpad = jnp.concatenate([xt, jnp.zeros((1, d), xt.dtype)], axis=0)

    def expert_block(args):
        rows, wts, e = args
        xb = xpad[rows]
        hb = jax.nn.silu(xb @ w_gate[e]) * (xb @ w_up[e])
        return (hb @ w_down[e]) * wts[:, None].astype(xb.dtype)

    out = lax.map(expert_block, (row_tok.reshape(n_blocks, EXPERT_BLOCK),
                                 row_w.reshape(n_blocks, EXPERT_BLOCK), blk_e))
    return jax.ops.segment_sum(out.reshape(n_rows, d), row_tok, num_segments=t + 1)[:t]


def moe_ffn(h, w_router, b_router, w_gate, w_up, w_down, ws_gate, ws_up, ws_down):
    bsz, s, d = h.shape
    xt = h.reshape(bsz * s, d)
    eidx, gate = route(xt, w_router, b_router)
    y = routed_experts(xt, eidx, gate, w_gate, w_up, w_down) + swiglu(xt, ws_gate, ws_up, ws_down)
    return y.reshape(bsz, s, d)


def setup_inputs(seed: int = 0) -> dict:
    key = jax.random.key(seed)
    ks = iter(jax.random.split(key, 40))
    f32 = jnp.float32
    D, F = D_MODEL, D_EXPERT

    def nrm(shape, scale):
        return jax.random.normal(next(ks), shape, f32) * scale

    def gain(shape):
        return 1.0 + nrm(shape, 0.02)

    def bias(shape):
        return nrm(shape, 0.02)

    return {
        "x": nrm((BATCH, SEQ, D), 1.0),
        "p": nrm((DEPTH, BATCH, SEQ, PLE_DIM), 1.0),
        "a_w_in": nrm((N_A_LAYERS, D, GLU_EXPAND * D), D ** -0.5),
        "a_b_in": bias((N_A_LAYERS, GLU_EXPAND * D)),
        "a_w_dw": nrm((N_A_LAYERS, CONV_WIDTH, D), CONV_WIDTH ** -0.5),
        "a_b_dw": bias((N_A_LAYERS, D)),
        "a_ln_g": gain((N_A_LAYERS, D)),
        "a_ln_b": bias((N_A_LAYERS, D)),
        "a_w_out": nrm((N_A_LAYERS, D, D), D ** -0.5 * DEEPNORM_BETA),
        "a_b_out": bias((N_A_LAYERS, D)),
        "b_w_in": nrm((N_B_LAYERS, D, 3 * D), D ** -0.5),
        "b_w_conv": nrm((N_B_LAYERS, SHORT_CONV_WIDTH, D), SHORT_CONV_WIDTH ** -0.5),
        "b_w_out": nrm((N_B_LAYERS, D, D), D ** -0.5 * DEEPNORM_BETA),
        "c_w_grp": nrm((N_C_LAYERS, N_POOL_GROUPS, POOL_GROUP_DIM, POOL_GROUP_DIM),
                        POOL_GROUP_DIM ** -0.5 * DEEPNORM_BETA),
        "c_scale": gain((N_C_LAYERS, D)),
        "ln1_g": gain((DEPTH, D)),
        "ln1_b": bias((DEPTH, D)),
        "ln2_g": gain((DEPTH, D)),
        "ln2_b": bias((DEPTH, D)),
        "router_w": nrm((DEPTH, D, N_EXPERTS), D ** -0.5),
        "router_b": nrm((DEPTH, N_EXPERTS), 0.01),
        "exp_w_gate": nrm((DEPTH, N_EXPERTS, D, F), D ** -0.5),
        "exp_w_up": nrm((DEPTH, N_EXPERTS, D, F), D ** -0.5),
        "exp_w_down": nrm((DEPTH, N_EXPERTS, F, D), F ** -0.5 * DEEPNORM_BETA),
        "sh_w_gate": nrm((DEPTH, D, D_SHARED), D ** -0.5),
        "sh_w_up": nrm((DEPTH, D, D_SHARED), D ** -0.5),
        "sh_w_down": nrm((DEPTH, D_SHARED, D), D_SHARED ** -0.5 * DEEPNORM_BETA),
        "ple_w_proj": nrm((DEPTH, PLE_DIM, D), PLE_DIM ** -0.5),
        "ple_w_gate": nrm((DEPTH, D, D), D ** -0.5),
        "ple_ln_g": gain((DEPTH, D)),
        "ple_ln_b": bias((DEPTH, D)),
    }


def reference(x, p, a_w_in, a_b_in, a_w_dw, a_b_dw, a_ln_g, a_ln_b, a_w_out, a_b_out,
              b_w_in, b_w_conv, b_w_out, c_w_grp, c_scale,
              ln1_g, ln1_b, ln2_g, ln2_b, router_w, router_b,
              exp_w_gate, exp_w_up, exp_w_down, sh_w_gate, sh_w_up, sh_w_down,
              ple_w_proj, ple_w_gate, ple_ln_g, ple_ln_b):
    h = x
    for i in range(DEPTH):
        kind, j = i % N_MIXERS, i // N_MIXERS
        if kind == 0:
            m = conformer_conv(h, a_w_in[j], a_b_in[j], a_w_dw[j], a_b_dw[j],
                               a_ln_g[j], a_ln_b[j], a_w_out[j], a_b_out[j])
        elif kind == 1:
            m = short_gated_conv(h, b_w_in[j], b_w_conv[j], b_w_out[j])
        else:
            m = multiscale_pool(h, c_w_grp[j], c_scale[j])
        h = layer_norm(DEEPNORM_ALPHA * h + m, ln1_g[i], ln1_b[i])
        f = moe_ffn(h, router_w[i], router_b[i], exp_w_gate[i], exp_w_up[i], exp_w_down[i],
                    sh_w_gate[i], sh_w_up[i], sh_w_down[i])
        h = layer_norm(DEEPNORM_ALPHA * h + f, ln2_g[i], ln2_b[i])
        e = p[i].astype(h.dtype) @ ple_w_proj[i]
        g = jax.nn.sigmoid(h @ ple_w_gate[i])
        h = h + layer_norm(g * e, ple_ln_g[i], ple_ln_b[i])
    return h
```

```python
import functools

import jax
import jax.numpy as jnp
from jax import lax
from jax.experimental import pallas as pl
from jax.experimental.pallas import tpu as pltpu

LN_EPS = 1e-5
TOP_K = 8
N_EXPERT_GROUPS = 8
TOPK_GROUPS = 4
ROUTED_SCALE = 2.5
POOL_WINDOWS = (2, 4, 8, 16)
EXPERT_BLOCK = 512
LANES = 128
SUBLANES = 8
VMEM_LIMIT = 56 * 1024 * 1024

BF16 = jnp.bfloat16
F32 = jnp.float32


def _params(*sem):
    return pltpu.CompilerParams(dimension_semantics=sem, vmem_limit_bytes=VMEM_LIMIT)


def _ln(x, g, b):
    mu = jnp.mean(x, axis=-1, keepdims=True)
    xc = x - mu
    var = jnp.mean(xc * xc, axis=-1, keepdims=True)
    return xc * lax.rsqrt(var + LN_EPS) * g + b


def _dot(a, b):
    return jnp.dot(a, b, preferred_element_type=F32)


def _silu(x):
    return x * jax.nn.sigmoid(x)


def _glu_kernel(x_ref, wa_ref, wg_ref, ba_ref, bg_ref, o_ref):
    x = x_ref[...].astype(BF16)
    a = _dot(x, wa_ref[...]) + ba_ref[...]
    g = _dot(x, wg_ref[...]) + bg_ref[...]
    o_ref[...] = a * jax.nn.sigmoid(g)


def _glu_proj(h, w_in, b_in, bm, bn):
    t, d = h.shape
    nb = d // bn
    b2 = b_in.reshape(1, 2 * d)
    return pl.pallas_call(
        _glu_kernel,
        grid=(t // bm, nb),
        in_specs=[
            pl.BlockSpec((bm, d), lambda i, j: (i, 0)),
            pl.BlockSpec((d, bn), lambda i, j: (0, j)),
            pl.BlockSpec((d, bn), lambda i, j: (0, j + nb)),
            pl.BlockSpec((1, bn), lambda i, j: (0, j)),
            pl.BlockSpec((1, bn), lambda i, j: (0, j + nb)),
        ],
        out_specs=pl.BlockSpec((bm, bn), lambda i, j: (i, j)),
        out_shape=jax.ShapeDtypeStruct((t, d), F32),
        compiler_params=_params("parallel", "arbitrary"),
        name="glu_proj",
    )(h, w_in, w_in, b2, b2)


def _bcv_kernel(x_ref, wb_ref, wc_ref, wv_ref, b_ref, cv_ref):
    x = x_ref[...].astype(BF16)
    b_ref[...] = _dot(x, wb_ref[...])
    cv_ref[...] = _dot(x, wc_ref[...]) * _dot(x, wv_ref[...])


def _bcv_proj(h, w_in, bm, bn):
    t, d = h.shape
    nb = d // bn
    return pl.pallas_call(
        _bcv_kernel,
        grid=(t // bm, nb),
        in_specs=[
            pl.BlockSpec((bm, d), lambda i, j: (i, 0)),
            pl.BlockSpec((d, bn), lambda i, j: (0, j)),
            pl.BlockSpec((d, bn), lambda i, j: (0, j + nb)),
            pl.BlockSpec((d, bn), lambda i, j: (0, j + 2 * nb)),
        ],
        out_specs=[pl.BlockSpec((bm, bn), lambda i, j: (i, j))] * 2,
        out_shape=[jax.ShapeDtypeStruct((t, d), F32)] * 2,
        compiler_params=_params("parallel", "arbitrary"),
        name="bcv_proj",
    )(h, w_in, w_in, w_in)


def _fill_window(buf_ref, halo_ref, main_ref, halo):
    first = pl.program_id(1) == 0
    buf_ref[0:halo, :] = jnp.where(first, 0.0, halo_ref[0])
    buf_ref[halo:, :] = main_ref[0]


def _conv_ln_kernel(u_ref, halo_ref, w_ref, bdw_ref, g_ref, b_ref, o_ref, buf_ref, acc_ref,
                    *, width, halo, ts, rows):
    _fill_window(buf_ref, halo_ref, u_ref, halo)
    d = buf_ref.shape[1]
    off = halo - (width - 1)

    def chunk(r, _):
        r0 = pl.multiple_of(r * rows, rows)
        for c in range(d // LANES):
            cs = slice(c * LANES, (c + 1) * LANES)
            win = buf_ref[pl.ds(r0, rows + halo), cs]
            acc = jnp.zeros((rows, LANES), F32)
            for s in range(SUBLANES):
                taps = [k for k in range(width) if (off + k) % SUBLANES == s]
                if not taps:
                    continue
                span = max(off + k for k in taps) - s + rows
                ws = win[s:s + span]
                for k in taps:
                    q = off + k - s
                    acc = acc + w_ref[k:k + 1, cs] * ws[q:q + rows]
            acc_ref[pl.ds(r0, rows), cs] = acc
        return 0

    lax.fori_loop(0, ts // rows, chunk, 0)
    v = _ln(acc_ref[...] + bdw_ref[...], g_ref[...], b_ref[...])
    o_ref[0] = _silu(v).astype(o_ref.dtype)


def _conv_ln_silu(u, w_dw, b_dw, ln_g, ln_b, ts):
    bsz, s, d = u.shape
    width = w_dw.shape[0]
    halo = -(-(width - 1) // SUBLANES) * SUBLANES
    hb = ts // halo
    kern = functools.partial(_conv_ln_kernel, width=width, halo=halo, ts=ts, rows=32)
    vec = lambda a: a.reshape(1, d)
    return pl.pallas_call(
        kern,
        grid=(bsz, s // ts),
        in_specs=[
            pl.BlockSpec((1, ts, d), lambda b, i: (b, i, 0)),
            pl.BlockSpec((1, halo, d), lambda b, i: (b, jnp.maximum(i * hb - 1, 0), 0)),
            pl.BlockSpec((width, d), lambda b, i: (0, 0)),
            pl.BlockSpec((1, d), lambda b, i: (0, 0)),
            pl.BlockSpec((1, d), lambda b, i: (0, 0)),
            pl.BlockSpec((1, d), lambda b, i: (0, 0)),
        ],
        out_specs=pl.BlockSpec((1, ts, d), lambda b, i: (b, i, 0)),
        out_shape=jax.ShapeDtypeStruct((bsz, s, d), BF16),
        scratch_shapes=[pltpu.VMEM((halo + ts, d), F32), pltpu.VMEM((ts, d), F32)],
        compiler_params=_params("parallel", "arbitrary"),
        name="conv_ln_silu",
    )(u, u, w_dw, vec(b_dw), vec(ln_g), vec(ln_b))


def _gated_conv_kernel(cv_ref, halo_ref, bg_ref, w_ref, o_ref, buf_ref, *, width, halo, ts):
    _fill_window(buf_ref, halo_ref, cv_ref, halo)
    off = halo - (width - 1)
    acc = w_ref[0:1, :] * buf_ref[off:off + ts, :]
    for k in range(1, width):
        acc = acc + w_ref[k:k + 1, :] * buf_ref[off + k:off + k + ts, :]
    o_ref[0] = (bg_ref[0] * acc).astype(o_ref.dtype)


def _gated_short_conv(bg, cv, w_conv, ts):
    bsz, s, d = cv.shape
    width = w_conv.shape[0]
    halo = SUBLANES
    hb = ts // halo
    kern = functools.partial(_gated_conv_kernel, width=width, halo=halo, ts=ts)
    return pl.pallas_call(
        kern,
        grid=(bsz, s // ts),
        in_specs=[
            pl.BlockSpec((1, ts, d), lambda b, i: (b, i, 0)),
            pl.BlockSpec((1, halo, d), lambda b, i: (b, jnp.maximum(i * hb - 1, 0), 0)),
            pl.BlockSpec((1, ts, d), lambda b, i: (b, i, 0)),
            pl.BlockSpec((width, d), lambda b, i: (0, 0)),
        ],
        out_specs=pl.BlockSpec((1, ts, d), lambda b, i: (b, i, 0)),
        out_shape=jax.ShapeDtypeStruct((bsz, s, d), BF16),
        scratch_shapes=[pltpu.VMEM((halo + ts, d), F32)],
        compiler_params=_params("parallel", "arbitrary"),
        name="gated_short_conv",
    )(cv, cv, bg, w_conv)


def _proj_ln_kernel(v_ref, w_ref, bias_ref, h_ref, g_ref, b_ref, o_ref, *, alpha):
    m = _dot(v_ref[...], w_ref[...]) + bias_ref[...]
    o_ref[...] = _ln(alpha * h_ref[...] + m, g_ref[...], b_ref[...])


def _proj_residual_ln(v, w_out, b_out, h, ln_g, ln_b, alpha, bm):
    t, d = h.shape
    vec = lambda a: a.reshape(1, d)
    return pl.pallas_call(
        functools.partial(_proj_ln_kernel, alpha=alpha),
        grid=(t // bm,),
        in_specs=[
            pl.BlockSpec((bm, d), lambda i: (i, 0)),
            pl.BlockSpec((d, d), lambda i: (0, 0)),
            pl.BlockSpec((1, d), lambda i: (0, 0)),
            pl.BlockSpec((bm, d), lambda i: (i, 0)),
            pl.BlockSpec((1, d), lambda i: (0, 0)),
            pl.BlockSpec((1, d), lambda i: (0, 0)),
        ],
        out_specs=pl.BlockSpec((bm, d), lambda i: (i, 0)),
        out_shape=jax.ShapeDtypeStruct((t, d), F32),
        compiler_params=_params("parallel"),
        name="proj_residual_ln",
    )(v, w_out, vec(b_out), h, vec(ln_g), vec(ln_b))


def _pool_kernel(h_ref, halo_ref, w_ref, scale_ref, g_ref, b_ref, o_ref, buf_ref,
                 *, halo, ts, alpha):
    _fill_window(buf_ref, halo_ref, h_ref, halo)
    d = buf_ref.shape[1]
    gd = d // len(POOL_WINDOWS)
    pos = pl.program_id(1) * ts + lax.broadcasted_iota(jnp.int32, (ts, 1), 0) + 1
    ys = []
    for gi, win in enumerate(POOL_WINDOWS):
        cs = slice(gi * gd, (gi + 1) * gd)
        x = buf_ref[halo:halo + ts, cs]
        tot = x
        for j in range(1, win):
            tot = tot + buf_ref[halo - j:halo - j + ts, cs]
        cnt = jnp.minimum(pos, win).astype(F32)
        z = tot / cnt - x
        ys.append(_dot(z.astype(BF16), w_ref[gi]))
    y = jnp.concatenate(ys, axis=-1) * scale_ref[...]
    o_ref[0] = _ln(alpha * h_ref[0] + y, g_ref[...], b_ref[...])


def _pool_mixer_ln(h, w_grp, scale, ln_g, ln_b, alpha, ts):
    bsz, s, d = h.shape
    halo = max(POOL_WINDOWS)
    hb = ts // halo
    vec = lambda a: a.reshape(1, d)
    return pl.pallas_call(
        functools.partial(_pool_kernel, halo=halo, ts=ts, alpha=alpha),
        grid=(bsz, s // ts),
        in_specs=[
            pl.BlockSpec((1, ts, d), lambda b, i: (b, i, 0)),
            pl.BlockSpec((1, halo, d), lambda b, i: (b, jnp.maximum(i * hb - 1, 0), 0)),
            pl.BlockSpec(w_grp.shape, lambda b, i: (0, 0, 0)),
            pl.BlockSpec((1, d), lambda b, i: (0, 0)),
            pl.BlockSpec((1, d), lambda b, i: (0, 0)),
            pl.BlockSpec((1, d), lambda b, i: (0, 0)),
        ],
        out_specs=pl.BlockSpec((1, ts, d), lambda b, i: (b, i, 0)),
        out_shape=jax.ShapeDtypeStruct((bsz, s, d), F32),
        scratch_shapes=[pltpu.VMEM((halo + ts, d), F32)],
        compiler_params=_params("parallel", "arbitrary"),
        name="pool_mixer_ln",
    )(h, h, w_grp, vec(scale), vec(ln_g), vec(ln_b))


def _rank_desc(m):
    n = m.shape[0]
    row = lax.broadcasted_iota(jnp.int32, m.shape, 0)
    rank = jnp.zeros(m.shape, jnp.int32)
    for j in range(n):
        mj = m[j:j + 1, :]
        beats = (mj > m) | ((mj == m) & (row > j))
        rank = rank + beats.astype(jnp.int32)
    return rank


def _router_kernel(h_ref, wt_ref, b_ref, idx_ref, pos_ref, gate_ref, cnt_ref, *, tm):
    step = pl.program_id(0)

    @pl.when(step == 0)
    def _():
        cnt_ref[...] = jnp.zeros_like(cnt_ref)

    n_e = wt_ref.shape[0]
    epg = n_e // N_EXPERT_GROUPS
    x = h_ref[...].astype(BF16)
    logits = lax.dot_general(wt_ref[...], x, (((1,), (1,)), ((), ())),
                             preferred_element_type=F32)
    scores = jax.nn.sigmoid(logits)
    biased = scores + b_ref[...]

    sub = lax.broadcasted_iota(jnp.int32, (epg, tm), 0)
    gscore = []
    for g in range(N_EXPERT_GROUPS):
        blk = biased[g * epg:(g + 1) * epg, :]
        m1 = jnp.max(blk, axis=0, keepdims=True)
        first = jnp.min(jnp.where(blk == m1, sub, epg), axis=0, keepdims=True)
        m2 = jnp.max(jnp.where(sub == first, -jnp.inf, blk), axis=0, keepdims=True)
        gscore.append(m1 + m2)
    gsel = _rank_desc(jnp.concatenate(gscore, axis=0)) < TOPK_GROUPS
    emask = jnp.concatenate(
        [jnp.broadcast_to(gsel[g:g + 1, :], (epg, tm)) for g in range(N_EXPERT_GROUPS)], axis=0)
    masked = jnp.where(emask, biased, -jnp.inf)
    sel = _rank_desc(masked) < TOP_K
    self32 = sel.astype(F32)

    gate = jnp.where(sel, scores, 0.0)
    gate = gate / jnp.sum(gate, axis=0, keepdims=True) * ROUTED_SCALE

    er = lax.broadcasted_iota(jnp.int32, (n_e, n_e), 0)
    ec = lax.broadcasted_iota(jnp.int32, (n_e, n_e), 1)
    lower = (ec < er).astype(BF16)
    slot = _dot(lower, self32.astype(BF16))
    tr = lax.broadcasted_iota(jnp.int32, (tm, tm), 0)
    tc = lax.broadcasted_iota(jnp.int32, (tm, tm), 1)
    before = (tr < tc).astype(BF16)
    carry = cnt_ref[:, 0:1].astype(F32)
    rank = _dot(self32.astype(BF16), before) + carry

    erow = lax.broadcasted_iota(jnp.int32, (n_e, tm), 0).astype(F32)
    idx_rows, pos_rows, gate_rows = [], [], []
    for k in range(TOP_K):
        pick = sel & (slot == float(k))
        idx_rows.append(jnp.sum(jnp.where(pick, erow, 0.0), axis=0, keepdims=True))
        pos_rows.append(jnp.sum(jnp.where(pick, rank, 0.0), axis=0, keepdims=True))
        gate_rows.append(jnp.sum(jnp.where(pick, gate, 0.0), axis=0, keepdims=True))
    idx_ref[...] = jnp.concatenate(idx_rows, axis=0).astype(jnp.int32)
    pos_ref[...] = jnp.concatenate(pos_rows, axis=0).astype(jnp.int32)
    gate_ref[...] = jnp.concatenate(gate_rows, axis=0)
    total = jnp.sum(self32, axis=1, keepdims=True).astype(jnp.int32)
    cnt_ref[...] = cnt_ref[...] + total


def _router(h1, w_router_t, b_router, tm):
    t, d = h1.shape
    n_e = w_router_t.shape[0]
    idx, pos, gate, cnt = pl.pallas_call(
        functools.partial(_router_kernel, tm=tm),
        grid=(t // tm,),
        in_specs=[
            pl.BlockSpec((tm, d), lambda i: (i, 0)),
            pl.BlockSpec((n_e, d), lambda i: (0, 0)),
            pl.BlockSpec((n_e, 1), lambda i: (0, 0)),
        ],
        out_specs=[
            pl.BlockSpec((TOP_K, tm), lambda i: (0, i)),
            pl.BlockSpec((TOP_K, tm), lambda i: (0, i)),
            pl.BlockSpec((TOP_K, tm), lambda i: (0, i)),
            pl.BlockSpec((n_e, LANES), lambda i: (0, 0)),
        ],
        out_shape=[
            jax.ShapeDtypeStruct((TOP_K, t), jnp.int32),
            jax.ShapeDtypeStruct((TOP_K, t), jnp.int32),
            jax.ShapeDtypeStruct((TOP_K, t), F32),
            jax.ShapeDtypeStruct((n_e, LANES), jnp.int32),
        ],
        compiler_params=_params("arbitrary"),
        name="router",
    )(h1, w_router_t, b_router.reshape(n_e, 1))
    return idx, pos, gate, cnt[:, 0]


def _row_copy(src, dst, sem):
    return pltpu.make_async_copy(src, dst, sem)


def _dispatch_kernel(pstart_ref, idx_ref, pos_ref, h_ref, xs_ref, sem, *, tm):
    def issue(t, _):
        for k in range(TOP_K):
            dest = pstart_ref[idx_ref[k, t]] + pos_ref[k, t]
            _row_copy(h_ref.at[t], xs_ref.at[dest], sem).start()
        return 0

    lax.fori_loop(0, tm, issue, 0)

    def drain(t, _):
        for k in range(TOP_K):
            _row_copy(h_ref.at[0], xs_ref.at[0], sem).wait()
        return 0

    lax.fori_loop(0, tm, drain, 0)


def _dispatch(h_rows, idx, pos, pstart, n_rows, tm):
    t, nc, _ = h_rows.shape
    grid_spec = pltpu.PrefetchScalarGridSpec(
        num_scalar_prefetch=1,
        grid=(t // tm,),
        in_specs=[
            pl.BlockSpec((TOP_K, tm), lambda i, ps: (0, i), memory_space=pltpu.SMEM),
            pl.BlockSpec((TOP_K, tm), lambda i, ps: (0, i), memory_space=pltpu.SMEM),
            pl.BlockSpec((tm, nc, LANES), lambda i, ps: (i, 0, 0)),
        ],
        out_specs=pl.BlockSpec(memory_space=pl.ANY),
        scratch_shapes=[pltpu.SemaphoreType.DMA(())],
    )
    return pl.pallas_call(
        functools.partial(_dispatch_kernel, tm=tm),
        grid_spec=grid_spec,
        out_shape=jax.ShapeDtypeStruct((n_rows, nc, LANES), F32),
        compiler_params=_params("arbitrary"),
        name="dispatch",
    )(pstart, idx, pos, h_rows)


def _expert_kernel(be_ref, nv_ref, x_ref, wg_ref, wu_ref, wd_ref, o_ref):
    i = pl.program_id(0)
    nv = nv_ref[i]
    blk, nc, _ = x_ref.shape

    @pl.when(nv > 0)
    def _():
        x = jnp.concatenate([x_ref[:, c, :] for c in range(nc)], axis=-1)
        row = lax.broadcasted_iota(jnp.int32, (blk, 1), 0)
        x = jnp.where(row < nv, x, 0.0).astype(BF16)
        g = _dot(x, wg_ref[0].astype(BF16))
        u = _dot(x, wu_ref[0].astype(BF16))
        hmid = (_silu(g) * u).astype(BF16)
        y = _dot(hmid, wd_ref[0].astype(BF16))
        for c in range(nc):
            o_ref[:, c, :] = y[:, c * LANES:(c + 1) * LANES]


def _experts(xs, blk_e, blk_nv, w_gate, w_up, w_down):
    n_rows, nc, _ = xs.shape
    _, d, f = w_gate.shape
    n_blocks = n_rows // EXPERT_BLOCK
    grid_spec = pltpu.PrefetchScalarGridSpec(
        num_scalar_prefetch=2,
        grid=(n_blocks,),
        in_specs=[
            pl.BlockSpec((EXPERT_BLOCK, nc, LANES), lambda i, be, nv: (i, 0, 0)),
            pl.BlockSpec((1, d, f), lambda i, be, nv: (be[i], 0, 0)),
            pl.BlockSpec((1, d, f), lambda i, be, nv: (be[i], 0, 0)),
            pl.BlockSpec((1, f, d), lambda i, be, nv: (be[i], 0, 0)),
        ],
        out_specs=pl.BlockSpec((EXPERT_BLOCK, nc, LANES), lambda i, be, nv: (i, 0, 0)),
    )
    return pl.pallas_call(
        _expert_kernel,
        grid_spec=grid_spec,
        out_shape=jax.ShapeDtypeStruct((n_rows, nc, LANES), F32),
        compiler_params=_params("arbitrary"),
        name="experts",
    )(blk_e, blk_nv, xs, w_gate, w_up, w_down)


def _combine_kernel(pstart_ref, idx_ref, pos_ref, gate_ref, h_ref, wsg_ref, wsu_ref, wsd_ref,
                    g_ref, b_ref, ys_ref, o_ref, buf_ref, sem, *, tm, alpha):
    def issue(t, _):
        for k in range(TOP_K):
            src = pstart_ref[idx_ref[k, t]] + pos_ref[k, t]
            _row_copy(ys_ref.at[src], buf_ref.at[k, t], sem).start()
        return 0

    lax.fori_loop(0, tm, issue, 0)

    h = h_ref[...]
    x = h.astype(BF16)
    mid = (_silu(_dot(x, wsg_ref[...])) * _dot(x, wsu_ref[...])).astype(BF16)
    shared = _dot(mid, wsd_ref[...])

    def drain(t, _):
        for k in range(TOP_K):
            _row_copy(ys_ref.at[0], buf_ref.at[0, 0], sem).wait()
        return 0

    lax.fori_loop(0, tm, drain, 0)

    nc = buf_ref.shape[2]
    routed = None
    for k in range(TOP_K):
        yk = jnp.concatenate([buf_ref[k, :, c, :] for c in range(nc)], axis=-1)
        term = yk * gate_ref[:, k:k + 1]
        routed = term if routed is None else routed + term
    o_ref[...] = _ln(alpha * h + (routed + shared), g_ref[...], b_ref[...])


def _combine_ln(ys, idx, pos, gate_t, pstart, h1, ws_gate, ws_up, ws_down, ln_g, ln_b, alpha, tm):
    t, d = h1.shape
    nc = d // LANES
    f = ws_gate.shape[1]
    vec = lambda a: a.reshape(1, d)
    grid_spec = pltpu.PrefetchScalarGridSpec(
        num_scalar_prefetch=1,
        grid=(t // tm,),
        in_specs=[
            pl.BlockSpec((TOP_K, tm), lambda i, ps: (0, i), memory_space=pltpu.SMEM),
            pl.BlockSpec((TOP_K, tm), lambda i, ps: (0, i), memory_space=pltpu.SMEM),
            pl.BlockSpec((tm, TOP_K), lambda i, ps: (i, 0)),
            pl.BlockSpec((tm, d), lambda i, ps: (i, 0)),
            pl.BlockSpec((d, f), lambda i, ps: (0, 0)),
            pl.BlockSpec((d, f), lambda i, ps: (0, 0)),
            pl.BlockSpec((f, d), lambda i, ps: (0, 0)),
            pl.BlockSpec((1, d), lambda i, ps: (0, 0)),
            pl.BlockSpec((1, d), lambda i, ps: (0, 0)),
            pl.BlockSpec(memory_space=pl.ANY),
        ],
        out_specs=pl.BlockSpec((tm, d), lambda i, ps: (i, 0)),
        scratch_shapes=[pltpu.VMEM((TOP_K, tm, nc, LANES), F32), pltpu.SemaphoreType.DMA(())],
    )
    return pl.pallas_call(
        functools.partial(_combine_kernel, tm=tm, alpha=alpha),
        grid_spec=grid_spec,
        out_shape=jax.ShapeDtypeStruct((t, d), F32),
        compiler_params=_params("arbitrary"),
        name="combine_ln",
    )(pstart, idx, pos, gate_t, h1, ws_gate, ws_up, ws_down, vec(ln_g), vec(ln_b), ys)


def _ple_kernel(h_ref, p_ref, wp_ref, wg_ref, g_ref, b_ref, o_ref):
    h = h_ref[...]
    e = _dot(p_ref[...].astype(BF16), wp_ref[...])
    gate = jax.nn.sigmoid(_dot(h.astype(BF16), wg_ref[...]))
    o_ref[...] = h + _ln(gate * e, g_ref[...], b_ref[...])


def _ple(h2, p, w_proj, w_gate, ln_g, ln_b, bm):
    t, d = h2.shape
    pd = p.shape[1]
    vec = lambda a: a.reshape(1, d)
    return pl.pallas_call(
        _ple_kernel,
        grid=(t // bm,),
        in_specs=[
            pl.BlockSpec((bm, d), lambda i: (i, 0)),
            pl.BlockSpec((bm, pd), lambda i: (i, 0)),
            pl.BlockSpec((pd, d), lambda i: (0, 0)),
            pl.BlockSpec((d, d), lambda i: (0, 0)),
            pl.BlockSpec((1, d), lambda i: (0, 0)),
            pl.BlockSpec((1, d), lambda i: (0, 0)),
        ],
        out_specs=pl.BlockSpec((bm, d), lambda i: (i, 0)),
        out_shape=jax.ShapeDtypeStruct((t, d), F32),
        compiler_params=_params("parallel"),
        name="ple",
    )(h2, p, w_proj, w_gate, vec(ln_g), vec(ln_b))


def _block_tables(counts, n_blocks):
    pcounts = (counts + EXPERT_BLOCK - 1) // EXPERT_BLOCK * EXPERT_BLOCK
    pends = jnp.cumsum(pcounts)
    pstarts = pends - pcounts
    blk_start = jnp.arange(n_blocks, dtype=jnp.int32) * EXPERT_BLOCK
    n_e = counts.shape[0]
    blk_e = jnp.minimum(jnp.searchsorted(pends, blk_start, side="right"), n_e - 1).astype(jnp.int32)
    blk_nv = jnp.clip(pstarts[blk_e] + counts[blk_e] - blk_start, 0, EXPERT_BLOCK).astype(jnp.int32)
    return pstarts.astype(jnp.int32), blk_e, blk_nv


def _moe_ln(h1, w_router, b_router, w_gate, w_up, w_down, ws_gate, ws_up, ws_down,
            ln_g, ln_b, alpha, tiles):
    t, d = h1.shape
    n_e = w_router.shape[1]
    n_blocks = -(-t * TOP_K // EXPERT_BLOCK) + n_e
    idx, pos, gate, counts = _router(h1, w_router.T.astype(BF16), b_router, tiles["router"])
    pstart, blk_e, blk_nv = _block_tables(counts, n_blocks)
    h_rows = h1.reshape(t, d // LANES, LANES)
    xs = _dispatch(h_rows, idx, pos, pstart, n_blocks * EXPERT_BLOCK, tiles["dispatch"])
    ys = _experts(xs, blk_e, blk_nv, w_gate, w_up, w_down)
    return _combine_ln(ys, idx, pos, gate.T, pstart, h1,
                       ws_gate.astype(BF16), ws_up.astype(BF16), ws_down.astype(BF16),
                       ln_g, ln_b, alpha, tiles["combine"])


def _tiles(t, s):
    return {
        "mm_m": min(1024, t), "mm_n": 512, "proj_m": min(512, t), "seq": min(512, s),
        "router": min(512, t), "dispatch": min(256, t), "combine": min(128, t),
    }


def kernel(x, p, a_w_in, a_b_in, a_w_dw, a_b_dw, a_ln_g, a_ln_b, a_w_out, a_b_out, b_w_in, b_w_conv, b_w_out, c_w_grp, c_scale, ln1_g, ln1_b, ln2_g, ln2_b, router_w, router_b, exp_w_gate, exp_w_up, exp_w_down, sh_w_gate, sh_w_up, sh_w_down, ple_w_proj, ple_w_gate, ple_ln_g, ple_ln_b):
    bsz, s, d = x.shape
    depth = ln1_g.shape[0]
    t = bsz * s
    alpha = (2 * depth) ** 0.25
    tl = _tiles(t, s)
    bn = min(tl["mm_n"], d)
    h = x.reshape(t, d)
    for i in range(depth):
        kind, j = i % 3, i // 3
        if kind == 0:
            u = _glu_proj(h, a_w_in[j].astype(BF16), a_b_in[j], tl["mm_m"], bn)
            v = _conv_ln_silu(u.reshape(bsz, s, d), a_w_dw[j], a_b_dw[j], a_ln_g[j], a_ln_b[j],
                              tl["seq"])
            h1 = _proj_residual_ln(v.reshape(t, d), a_w_out[j].astype(BF16), a_b_out[j], h,
                                   ln1_g[i], ln1_b[i], alpha, tl["proj_m"])
        elif kind == 1:
            bg, cv = _bcv_proj(h, b_w_in[j].astype(BF16), tl["mm_m"], bn)
            v = _gated_short_conv(bg.reshape(bsz, s, d), cv.reshape(bsz, s, d), b_w_conv[j],
                                  tl["seq"])
            h1 = _proj_residual_ln(v.reshape(t, d), b_w_out[j].astype(BF16),
                                   jnp.zeros((d,), F32), h, ln1_g[i], ln1_b[i], alpha,
                                   tl["proj_m"])
        else:
            h1 = _pool_mixer_ln(h.reshape(bsz, s, d), c_w_grp[j].astype(BF16), c_scale[j],
                                ln1_g[i], ln1_b[i], alpha, tl["seq"]).reshape(t, d)
        h2 = _moe_ln(h1, router_w[i], router_b[i], exp_w_gate[i], exp_w_up[i], exp_w_down[i],
                     sh_w_gate[i], sh_w_up[i], sh_w_down[i], ln2_g[i], ln2_b[i], alpha, tl)
        h = _ple(h2, p[i].reshape(t, -1), ple_w_proj[i].astype(BF16),
                 ple_w_gate[i].astype(BF16), ple_ln_g[i], ple_ln_b[i], tl["proj_m"])
    return h.reshape(bsz, s, d)
```

```python
import functools

import jax
import jax.numpy as jnp
from jax import lax
from jax.experimental import pallas as pl
from jax.experimental.pallas import tpu as pltpu

LN_EPS = 1e-5
TOP_K = 8
N_EXPERT_GROUPS = 8
TOPK_GROUPS = 4
ROUTED_SCALE = 2.5
POOL_WINDOWS = (2, 4, 8, 16)
EXPERT_BLOCK = 512
LANES = 128
SUBLANES = 8
VMEM_LIMIT = 56 * 1024 * 1024

BF16 = jnp.bfloat16
PACKED = jnp.bfloat16
F32 = jnp.float32


def _params(*sem):
    return pltpu.CompilerParams(dimension_semantics=sem, vmem_limit_bytes=VMEM_LIMIT)


def _ln(x, g, b):
    mu = jnp.mean(x, axis=-1, keepdims=True)
    xc = x - mu
    var = jnp.mean(xc * xc, axis=-1, keepdims=True)
    return xc * lax.rsqrt(var + LN_EPS) * g + b


def _dot(a, b):
    return jnp.dot(a, b, preferred_element_type=F32)


def _silu(x):
    return x * jax.nn.sigmoid(x)


def _glu_kernel(x_ref, wa_ref, wg_ref, ba_ref, bg_ref, o_ref):
    x = x_ref[...].astype(BF16)
    a = _dot(x, wa_ref[...]) + ba_ref[...]
    g = _dot(x, wg_ref[...]) + bg_ref[...]
    o_ref[...] = a * jax.nn.sigmoid(g)


def _glu_proj(h, w_in, b_in, bm, bn):
    t, d = h.shape
    nb = d // bn
    b2 = b_in.reshape(1, 2 * d)
    return pl.pallas_call(
        _glu_kernel,
        grid=(t // bm, nb),
        in_specs=[
            pl.BlockSpec((bm, d), lambda i, j: (i, 0)),
            pl.BlockSpec((d, bn), lambda i, j: (0, j)),
            pl.BlockSpec((d, bn), lambda i, j: (0, j + nb)),
            pl.BlockSpec((1, bn), lambda i, j: (0, j)),
            pl.BlockSpec((1, bn), lambda i, j: (0, j + nb)),
        ],
        out_specs=pl.BlockSpec((bm, bn), lambda i, j: (i, j)),
        out_shape=jax.ShapeDtypeStruct((t, d), F32),
        compiler_params=_params("parallel", "arbitrary"),
        name="glu_proj",
    )(h, w_in, w_in, b2, b2)


def _bcv_kernel(x_ref, wb_ref, wc_ref, wv_ref, b_ref, cv_ref):
    x = x_ref[...].astype(BF16)
    b_ref[...] = _dot(x, wb_ref[...])
    cv_ref[...] = _dot(x, wc_ref[...]) * _dot(x, wv_ref[...])


def _bcv_proj(h, w_in, bm, bn):
    t, d = h.shape
    nb = d // bn
    return pl.pallas_call(
        _bcv_kernel,
        grid=(t // bm, nb),
        in_specs=[
            pl.BlockSpec((bm, d), lambda i, j: (i, 0)),
            pl.BlockSpec((d, bn), lambda i, j: (0, j)),
            pl.BlockSpec((d, bn), lambda i, j: (0, j + nb)),
            pl.BlockSpec((d, bn), lambda i, j: (0, j + 2 * nb)),
        ],
        out_specs=[pl.BlockSpec((bm, bn), lambda i, j: (i, j))] * 2,
        out_shape=[jax.ShapeDtypeStruct((t, d), F32)] * 2,
        compiler_params=_params("parallel", "arbitrary"),
        name="bcv_proj",
    )(h, w_in, w_in, w_in)


def _fill_window(buf_ref, halo_ref, main_ref, halo):
    first = pl.program_id(1) == 0
    buf_ref[0:halo, :] = jnp.where(first, 0.0, halo_ref[0])
    buf_ref[halo:, :] = main_ref[0]


def _conv_ln_kernel(u_ref, halo_ref, w_ref, bdw_ref, g_ref, b_ref, o_ref, buf_ref, acc_ref,
                    *, width, halo, ts, rows):
    _fill_window(buf_ref, halo_ref, u_ref, halo)
    d = buf_ref.shape[1]
    off = halo - (width - 1)

    def chunk(r, _):
        r0 = pl.multiple_of(r * rows, rows)
        for c in range(d // LANES):
            cs = slice(c * LANES, (c + 1) * LANES)
            win = buf_ref[pl.ds(r0, rows + halo), cs]
            acc = jnp.zeros((rows, LANES), F32)
            for s in range(SUBLANES):
                taps = [k for k in range(width) if (off + k) % SUBLANES == s]
                if not taps:
                    continue
                span = max(off + k for k in taps) - s + rows
                ws = win[s:s + span]
                for k in taps:
                    q = off + k - s
                    acc = acc + w_ref[k:k + 1, cs] * ws[q:q + rows]
            acc_ref[pl.ds(r0, rows), cs] = acc
        return 0

    lax.fori_loop(0, ts // rows, chunk, 0)
    v = _ln(acc_ref[...] + bdw_ref[...], g_ref[...], b_ref[...])
    o_ref[0] = _silu(v).astype(o_ref.dtype)


def _conv_ln_silu(u, w_dw, b_dw, ln_g, ln_b, ts):
    bsz, s, d = u.shape
    width = w_dw.shape[0]
    halo = -(-(width - 1) // SUBLANES) * SUBLANES
    hb = ts // halo
    kern = functools.partial(_conv_ln_kernel, width=width, halo=halo, ts=ts, rows=32)
    vec = lambda a: a.reshape(1, d)
    return pl.pallas_call(
        kern,
        grid=(bsz, s // ts),
        in_specs=[
            pl.BlockSpec((1, ts, d), lambda b, i: (b, i, 0)),
            pl.BlockSpec((1, halo, d), lambda b, i: (b, jnp.maximum(i * hb - 1, 0), 0)),
            pl.BlockSpec((width, d), lambda b, i: (0, 0)),
            pl.BlockSpec((1, d), lambda b, i: (0, 0)),
            pl.BlockSpec((1, d), lambda b, i: (0, 0)),
            pl.BlockSpec((1, d), lambda b, i: (0, 0)),
        ],
        out_specs=pl.BlockSpec((1, ts, d), lambda b, i: (b, i, 0)),
        out_shape=jax.ShapeDtypeStruct((bsz, s, d), BF16),
        scratch_shapes=[pltpu.VMEM((halo + ts, d), F32), pltpu.VMEM((ts, d), F32)],
        compiler_params=_params("parallel", "arbitrary"),
        name="conv_ln_silu",
    )(u, u, w_dw, vec(b_dw), vec(ln_g), vec(ln_b))


def _gated_conv_kernel(cv_ref, halo_ref, bg_ref, w_ref, o_ref, buf_ref, *, width, halo, ts):
    _fill_window(buf_ref, halo_ref, cv_ref, halo)
    off = halo - (width - 1)
    acc = w_ref[0:1, :] * buf_ref[off:off + ts, :]
    for k in range(1, width):
        acc = acc + w_ref[k:k + 1, :] * buf_ref[off + k:off + k + ts, :]
    o_ref[0] = (bg_ref[0] * acc).astype(o_ref.dtype)


def _gated_short_conv(bg, cv, w_conv, ts):
    bsz, s, d = cv.shape
    width = w_conv.shape[0]
    halo = SUBLANES
    hb = ts // halo
    kern = functools.partial(_gated_conv_kernel, width=width, halo=halo, ts=ts)
    return pl.pallas_call(
        kern,
        grid=(bsz, s // ts),
        in_specs=[
            pl.BlockSpec((1, ts, d), lambda b, i: (b, i, 0)),
            pl.BlockSpec((1, halo, d), lambda b, i: (b, jnp.maximum(i * hb - 1, 0), 0)),
            pl.BlockSpec((1, ts, d), lambda b, i: (b, i, 0)),
            pl.BlockSpec((width, d), lambda b, i: (0, 0)),
        ],
        out_specs=pl.BlockSpec((1, ts, d), lambda b, i: (b, i, 0)),
        out_shape=jax.ShapeDtypeStruct((bsz, s, d), BF16),
        scratch_shapes=[pltpu.VMEM((halo + ts, d), F32)],
        compiler_params=_params("parallel", "arbitrary"),
        name="gated_short_conv",
    )(cv, cv, bg, w_conv)


def _proj_ln_kernel(v_ref, w_ref, bias_ref, h_ref, g_ref, b_ref, o_ref, *, alpha):
    m = _dot(v_ref[...], w_ref[...]) + bias_ref[...]
    o_ref[...] = _ln(alpha * h_ref[...] + m, g_ref[...], b_ref[...])


def _proj_residual_ln(v, w_out, b_out, h, ln_g, ln_b, alpha, bm):
    t, d = h.shape
    vec = lambda a: a.reshape(1, d)
    return pl.pallas_call(
        functools.partial(_proj_ln_kernel, alpha=alpha),
        grid=(t // bm,),
        in_specs=[
            pl.BlockSpec((bm, d), lambda i: (i, 0)),
            pl.BlockSpec((d, d), lambda i: (0, 0)),
            pl.BlockSpec((1, d), lambda i: (0, 0)),
            pl.BlockSpec((bm, d), lambda i: (i, 0)),
            pl.BlockSpec((1, d), lambda i: (0, 0)),
            pl.BlockSpec((1, d), lambda i: (0, 0)),
        ],
        out_specs=pl.BlockSpec((bm, d), lambda i: (i, 0)),
        out_shape=jax.ShapeDtypeStruct((t, d), F32),
        compiler_params=_params("parallel"),
        name="proj_residual_ln",
    )(v, w_out, vec(b_out), h, vec(ln_g), vec(ln_b))


def _pool_kernel(h_ref, halo_ref, w_ref, scale_ref, g_ref, b_ref, o_ref, buf_ref,
                 *, halo, ts, alpha):
    _fill_window(buf_ref, halo_ref, h_ref, halo)
    d = buf_ref.shape[1]
    gd = d // len(POOL_WINDOWS)
    pos = pl.program_id(1) * ts + lax.broadcasted_iota(jnp.int32, (ts, 1), 0) + 1
    ys = []
    for gi, win in enumerate(POOL_WINDOWS):
        cs = slice(gi * gd, (gi + 1) * gd)
        x = buf_ref[halo:halo + ts, cs]
        tot = x
        for j in range(1, win):
            tot = tot + buf_ref[halo - j:halo - j + ts, cs]
        cnt = jnp.minimum(pos, win).astype(F32)
        z = tot / cnt - x
        ys.append(_dot(z.astype(BF16), w_ref[gi]))
    y = jnp.concatenate(ys, axis=-1) * scale_ref[...]
    o_ref[0] = _ln(alpha * h_ref[0] + y, g_ref[...], b_ref[...])


def _pool_mixer_ln(h, w_grp, scale, ln_g, ln_b, alpha, ts):
    bsz, s, d = h.shape
    halo = max(POOL_WINDOWS)
    hb = ts // halo
    vec = lambda a: a.reshape(1, d)
    return pl.pallas_call(
        functools.partial(_pool_kernel, halo=halo, ts=ts, alpha=alpha),
        grid=(bsz, s // ts),
        in_specs=[
            pl.BlockSpec((1, ts, d), lambda b, i: (b, i, 0)),
            pl.BlockSpec((1, halo, d), lambda b, i: (b, jnp.maximum(i * hb - 1, 0), 0)),
            pl.BlockSpec(w_grp.shape, lambda b, i: (0, 0, 0)),
            pl.BlockSpec((1, d), lambda b, i: (0, 0)),
            pl.BlockSpec((1, d), lambda b, i: (0, 0)),
            pl.BlockSpec((1, d), lambda b, i: (0, 0)),
        ],
        out_specs=pl.BlockSpec((1, ts, d), lambda b, i: (b, i, 0)),
        out_shape=jax.ShapeDtypeStruct((bsz, s, d), F32),
        scratch_shapes=[pltpu.VMEM((halo + ts, d), F32)],
        compiler_params=_params("parallel", "arbitrary"),
        name="pool_mixer_ln",
    )(h, h, w_grp, vec(scale), vec(ln_g), vec(ln_b))


def _rank_desc(m):
    n = m.shape[0]
    row = lax.broadcasted_iota(jnp.int32, m.shape, 0)
    rank = jnp.zeros(m.shape, jnp.int32)
    for j in range(n):
        mj = m[j:j + 1, :]
        beats = (mj > m) | ((mj == m) & (row > j))
        rank = rank + beats.astype(jnp.int32)
    return rank


def _router_kernel(h_ref, wt_ref, b_ref, lpos_ref, gate_ref, cnt_ref, *, tm):
    n_e = wt_ref.shape[0]
    epg = n_e // N_EXPERT_GROUPS
    x = h_ref[...].astype(BF16)
    logits = lax.dot_general(wt_ref[...], x, (((1,), (1,)), ((), ())),
                             preferred_element_type=F32)
    scores = jax.nn.sigmoid(logits)
    biased = scores + b_ref[...]

    sub = lax.broadcasted_iota(jnp.int32, (epg, tm), 0)
    gscore = []
    for g in range(N_EXPERT_GROUPS):
        blk = biased[g * epg:(g + 1) * epg, :]
        m1 = jnp.max(blk, axis=0, keepdims=True)
        first = jnp.min(jnp.where(blk == m1, sub, epg), axis=0, keepdims=True)
        m2 = jnp.max(jnp.where(sub == first, -jnp.inf, blk), axis=0, keepdims=True)
        gscore.append(m1 + m2)
    gsel = _rank_desc(jnp.concatenate(gscore, axis=0)) < TOPK_GROUPS
    emask = jnp.concatenate(
        [jnp.broadcast_to(gsel[g:g + 1, :], (epg, tm)) for g in range(N_EXPERT_GROUPS)], axis=0)
    masked = jnp.where(emask, biased, -jnp.inf)
    sel = _rank_desc(masked) < TOP_K
    self32 = sel.astype(F32)
    selb = self32.astype(BF16)

    gate = jnp.where(sel, scores, 0.0)
    gate = gate / jnp.sum(gate, axis=0, keepdims=True) * ROUTED_SCALE

    er = lax.broadcasted_iota(jnp.int32, (n_e, n_e), 0)
    ec = lax.broadcasted_iota(jnp.int32, (n_e, n_e), 1)
    lower = (ec < er).astype(BF16)
    slot = _dot(lower, selb)
    tr = lax.broadcasted_iota(jnp.int32, (tm, tm), 0)
    tc = lax.broadcasted_iota(jnp.int32, (tm, tm), 1)
    before = (tr < tc).astype(BF16)
    lrank = _dot(selb, before)
    count = jnp.sum(self32, axis=1, keepdims=True)
    count_l = jnp.broadcast_to(count, (n_e, LANES))
    lstart = _dot(lower, count_l.astype(BF16))[:, 0:1]
    lpos = lstart + lrank

    lpos_rows, gate_rows = [], []
    for k in range(TOP_K):
        pick = sel & (slot == float(k))
        lpos_rows.append(jnp.sum(jnp.where(pick, lpos, 0.0), axis=0, keepdims=True))
        gate_rows.append(jnp.sum(jnp.where(pick, gate, 0.0), axis=0, keepdims=True))
    lpos_ref[...] = jnp.concatenate(lpos_rows, axis=0).astype(jnp.int32)
    gate_ref[...] = jnp.concatenate(gate_rows, axis=0)
    cnt_ref[0] = count_l.astype(jnp.int32)


def _router(h1, w_router_t, b_router, tm):
    t, d = h1.shape
    n_e = w_router_t.shape[0]
    assert tm <= 256, "tile counts must stay exactly representable in bf16"
    lpos, gate, cnt = pl.pallas_call(
        functools.partial(_router_kernel, tm=tm),
        grid=(t // tm,),
        in_specs=[
            pl.BlockSpec((tm, d), lambda i: (i, 0)),
            pl.BlockSpec((n_e, d), lambda i: (0, 0)),
            pl.BlockSpec((n_e, 1), lambda i: (0, 0)),
        ],
        out_specs=[
            pl.BlockSpec((TOP_K, tm), lambda i: (0, i)),
            pl.BlockSpec((TOP_K, tm), lambda i: (0, i)),
            pl.BlockSpec((1, n_e, LANES), lambda i: (i, 0, 0)),
        ],
        out_shape=[
            jax.ShapeDtypeStruct((TOP_K, t), jnp.int32),
            jax.ShapeDtypeStruct((TOP_K, t), F32),
            jax.ShapeDtypeStruct((t // tm, n_e, LANES), jnp.int32),
        ],
        compiler_params=_params("parallel"),
        name="router",
    )(h1, w_router_t, b_router.reshape(n_e, 1))
    return lpos, gate, cnt[:, :, 0]


HIGH_HALF = -65536


def _pack_words(lo, hi):
    bits = lambda v: lax.bitcast_convert_type(v.astype(PACKED).astype(F32), jnp.int32)
    return (bits(hi) & HIGH_HALF) | lax.shift_right_logical(bits(lo), 16)


def _word_lo(w):
    return lax.bitcast_convert_type(lax.shift_left(w, 16), F32)


def _word_hi(w):
    return lax.bitcast_convert_type(w & HIGH_HALF, F32)


def _pack_rows(x, dst_ref, n):
    half = x.shape[1] // 2
    rc = half // LANES
    words = _pack_words(x[:, :half], x[:, half:])
    for c in range(rc):
        dst_ref[pl.ds(c, n, stride=rc), :] = words[:, c * LANES:(c + 1) * LANES]


def _unpack_rows(src_ref, n, rc):
    lo, hi = [], []
    for c in range(rc):
        w = src_ref[pl.ds(c, n, stride=rc), :]
        lo.append(_word_lo(w))
        hi.append(_word_hi(w))
    return jnp.concatenate(lo + hi, axis=-1)


def _segment_copies(local_ref, hbm_ref, sem, tables, tile, n_e, rc, tm, to_hbm):
    cnt_ref, lstart_ref, gstart_ref = tables

    def per_expert(e, _):
        j = tile * n_e + e
        n, ls, gs = cnt_ref[j], lstart_ref[j], gstart_ref[j]
        size = tm
        while size >= 1:
            @pl.when((n & size) != 0)
            def _(size=size):
                done = n & ~(2 * size - 1)
                lo = pl.multiple_of((ls + done) * rc, rc)
                go = pl.multiple_of((gs + done) * rc, rc)
                loc = local_ref.at[pl.ds(lo, size * rc), :]
                glob = hbm_ref.at[pl.ds(go, size * rc), :]
                if to_hbm:
                    pltpu.make_async_copy(loc, glob, sem).start()
                else:
                    pltpu.make_async_copy(glob, loc, sem).start()
            size //= 2
        return 0

    lax.fori_loop(0, n_e, per_expert, 0)


def _wait_segments(local_ref, hbm_ref, sem):
    rows = local_ref.shape[0]
    pltpu.make_async_copy(local_ref, hbm_ref.at[pl.ds(0, rows), :], sem).wait()


def _dispatch_kernel(cnt_ref, lstart_ref, gstart_ref, lpos_ref, h_ref, xs_ref,
                     q_ref, s_ref, sem, *, tm, n_e, rc):
    i = pl.program_id(0)
    n = pl.num_programs(0)
    slot = i % 2

    @pl.when(i >= 2)
    def _():
        _wait_segments(s_ref.at[slot], xs_ref, sem.at[slot])

    _pack_rows(h_ref[...], q_ref, tm)

    def place(t, _):
        row = q_ref[pl.ds(pl.multiple_of(t * rc, rc), rc), :]
        for k in range(TOP_K):
            dst = pl.multiple_of(lpos_ref[k, t] * rc, rc)
            s_ref[slot, pl.ds(dst, rc), :] = row
        return 0

    lax.fori_loop(0, tm, place, 0)
    _segment_copies(s_ref.at[slot], xs_ref, sem.at[slot], (cnt_ref, lstart_ref, gstart_ref),
                    i, n_e, rc, tm, to_hbm=True)

    @pl.when(i == n - 1)
    def _():
        _wait_segments(s_ref.at[slot], xs_ref, sem.at[slot])

        @pl.when(n > 1)
        def _():
            _wait_segments(s_ref.at[1 - slot], xs_ref, sem.at[1 - slot])


def _dispatch(h1, lpos, tables, n_rows, tm, n_e):
    t, d = h1.shape
    rc = d // 2 // LANES
    grid_spec = pltpu.PrefetchScalarGridSpec(
        num_scalar_prefetch=3,
        grid=(t // tm,),
        in_specs=[
            pl.BlockSpec((TOP_K, tm), lambda i, *_: (0, i), memory_space=pltpu.SMEM),
            pl.BlockSpec((tm, d), lambda i, *_: (i, 0)),
        ],
        out_specs=pl.BlockSpec(memory_space=pl.ANY),
        scratch_shapes=[
            pltpu.VMEM((tm * rc, LANES), jnp.int32),
            pltpu.VMEM((2, TOP_K * tm * rc, LANES), jnp.int32),
            pltpu.SemaphoreType.DMA((2,)),
        ],
    )
    return pl.pallas_call(
        functools.partial(_dispatch_kernel, tm=tm, n_e=n_e, rc=rc),
        grid_spec=grid_spec,
        out_shape=jax.ShapeDtypeStruct((n_rows * rc, LANES), jnp.int32),
        compiler_params=_params("arbitrary"),
        name="dispatch",
    )(*tables, lpos, h1)


def _expert_kernel(be_ref, nv_ref, last_ref, x_ref, wg_ref, wu_ref, wd_ref, o_ref,
                   wgu_s, wd_s, *, blk, rc):
    i = pl.program_id(0)
    nv = nv_ref[i]
    f = wd_s.shape[0]
    fresh = (i == 0) | (be_ref[i] != be_ref[jnp.maximum(i - 1, 0)])

    @pl.when(fresh)
    def _():
        wgu_s[:, :f] = wg_ref[0, 0].astype(BF16)
        wgu_s[:, f:] = wu_ref[0, 0].astype(BF16)
        wd_s[...] = wd_ref[0, 0].astype(BF16)

    @pl.when(nv > 0)
    def _():
        x = _unpack_rows(x_ref, blk, rc).astype(BF16)
        gu = _dot(x, wgu_s[...])
        mid = _silu(gu[:, :f]) * gu[:, f:]
        row = lax.broadcasted_iota(jnp.int32, (blk, 1), 0)
        mid = jnp.where(row < nv, mid, 0.0).astype(BF16)
        _pack_rows(_dot(mid, wd_s[...]), o_ref, blk)


def _experts(xs, blk_e, blk_nv, blk_last, w_gate, w_up, w_down, layer, d):
    rc = d // 2 // LANES
    f = w_gate.shape[-1]
    n_blocks = xs.shape[0] // (EXPERT_BLOCK * rc)
    rows = lambda i, be, nv, last: (jnp.minimum(i, last[0]), 0)
    grid_spec = pltpu.PrefetchScalarGridSpec(
        num_scalar_prefetch=3,
        grid=(n_blocks,),
        in_specs=[
            pl.BlockSpec((EXPERT_BLOCK * rc, LANES), rows),
            pl.BlockSpec((1, 1, d, f), lambda i, be, nv, last: (layer, be[i], 0, 0)),
            pl.BlockSpec((1, 1, d, f), lambda i, be, nv, last: (layer, be[i], 0, 0)),
            pl.BlockSpec((1, 1, f, d), lambda i, be, nv, last: (layer, be[i], 0, 0)),
        ],
        out_specs=pl.BlockSpec((EXPERT_BLOCK * rc, LANES), rows),
        scratch_shapes=[pltpu.VMEM((d, 2 * f), BF16), pltpu.VMEM((f, d), BF16)],
    )
    return pl.pallas_call(
        functools.partial(_expert_kernel, blk=EXPERT_BLOCK, rc=rc),
        grid_spec=grid_spec,
        out_shape=jax.ShapeDtypeStruct(xs.shape, jnp.int32),
        compiler_params=_params("arbitrary"),
        name="experts",
    )(blk_e, blk_nv, blk_last, xs, w_gate, w_up, w_down)


def _combine_kernel(cnt_ref, lstart_ref, gstart_ref, lpos_ref, gate_ref, h_ref, wsgu_ref,
                    wsd_ref, g_ref, b_ref, ys_ref, o_ref, l_ref, r_ref, sem,
                    *, tm, n_e, rc, alpha):
    i = pl.program_id(0)
    n = pl.num_programs(0)
    slot = i % 2
    tables = (cnt_ref, lstart_ref, gstart_ref)

    @pl.when(i == 0)
    def _():
        _segment_copies(l_ref.at[0], ys_ref, sem.at[0], tables, 0, n_e, rc, tm, to_hbm=False)

    @pl.when(i + 1 < n)
    def _():
        _segment_copies(l_ref.at[1 - slot], ys_ref, sem.at[1 - slot], tables, i + 1, n_e, rc, tm,
                        to_hbm=False)

    h = h_ref[...]
    f = wsd_ref.shape[0]
    su = _dot(h.astype(BF16), wsgu_ref[...])
    shared = _dot((_silu(su[:, :f]) * su[:, f:]).astype(BF16), wsd_ref[...])

    _wait_segments(l_ref.at[slot], ys_ref, sem.at[slot])

    def gather(t, _):
        lo = jnp.zeros((rc, LANES), F32)
        hi = jnp.zeros((rc, LANES), F32)
        for k in range(TOP_K):
            src = pl.multiple_of(lpos_ref[k, t] * rc, rc)
            w = l_ref[slot, pl.ds(src, rc), :]
            gk = gate_ref[k, t]
            lo = lo + gk * _word_lo(w)
            hi = hi + gk * _word_hi(w)
        base = pl.multiple_of(t * 2 * rc, 2 * rc)
        r_ref[pl.ds(base, rc), :] = lo
        r_ref[pl.ds(base + rc, rc), :] = hi
        return 0

    lax.fori_loop(0, tm, gather, 0)
    routed = jnp.concatenate(
        [r_ref[pl.ds(c, tm, stride=2 * rc), :] for c in range(2 * rc)], axis=-1)
    o_ref[...] = _ln(alpha * h + (routed + shared), g_ref[...], b_ref[...])


def _combine_ln(ys, lpos, gate, tables, h1, ws_gu, ws_down, ln_g, ln_b, alpha, tm, n_e):
    t, d = h1.shape
    rc = d // 2 // LANES
    f = ws_down.shape[0]
    vec = lambda a: a.reshape(1, d)
    grid_spec = pltpu.PrefetchScalarGridSpec(
        num_scalar_prefetch=3,
        grid=(t // tm,),
        in_specs=[
            pl.BlockSpec((TOP_K, tm), lambda i, *_: (0, i), memory_space=pltpu.SMEM),
            pl.BlockSpec((TOP_K, tm), lambda i, *_: (0, i), memory_space=pltpu.SMEM),
            pl.BlockSpec((tm, d), lambda i, *_: (i, 0)),
            pl.BlockSpec((d, 2 * f), lambda i, *_: (0, 0)),
            pl.BlockSpec((f, d), lambda i, *_: (0, 0)),
            pl.BlockSpec((1, d), lambda i, *_: (0, 0)),
            pl.BlockSpec((1, d), lambda i, *_: (0, 0)),
            pl.BlockSpec(memory_space=pl.ANY),
        ],
        out_specs=pl.BlockSpec((tm, d), lambda i, *_: (i, 0)),
        scratch_shapes=[
            pltpu.VMEM((2, TOP_K * tm * rc, LANES), jnp.int32),
            pltpu.VMEM((tm * 2 * rc, LANES), F32),
            pltpu.SemaphoreType.DMA((2,)),
        ],
    )
    return pl.pallas_call(
        functools.partial(_combine_kernel, tm=tm, n_e=n_e, rc=rc, alpha=alpha),
        grid_spec=grid_spec,
        out_shape=jax.ShapeDtypeStruct((t, d), F32),
        compiler_params=_params("arbitrary"),
        name="combine_ln",
    )(*tables, lpos, gate, h1, ws_gu, ws_down, vec(ln_g), vec(ln_b), ys)


def _ple_kernel(h_ref, p_ref, wp_ref, wg_ref, g_ref, b_ref, o_ref):
    h = h_ref[...]
    e = _dot(p_ref[0].astype(BF16), wp_ref[...])
    gate = jax.nn.sigmoid(_dot(h.astype(BF16), wg_ref[...]))
    o_ref[...] = h + _ln(gate * e, g_ref[...], b_ref[...])


def _ple(h2, p, layer, w_proj, w_gate, ln_g, ln_b, bm):
    t, d = h2.shape
    pd = p.shape[-1]
    vec = lambda a: a.reshape(1, d)
    return pl.pallas_call(
        _ple_kernel,
        grid=(t // bm,),
        in_specs=[
            pl.BlockSpec((bm, d), lambda i: (i, 0)),
            pl.BlockSpec((1, bm, pd), lambda i: (layer, i, 0)),
            pl.BlockSpec((pd, d), lambda i: (0, 0)),
            pl.BlockSpec((d, d), lambda i: (0, 0)),
            pl.BlockSpec((1, d), lambda i: (0, 0)),
            pl.BlockSpec((1, d), lambda i: (0, 0)),
        ],
        out_specs=pl.BlockSpec((bm, d), lambda i: (i, 0)),
        out_shape=jax.ShapeDtypeStruct((t, d), F32),
        compiler_params=_params("parallel"),
        name="ple",
    )(h2, p, w_proj, w_gate, vec(ln_g), vec(ln_b))


def _moe_tables(cnt, n_blocks):
    n_e = cnt.shape[1]
    total = jnp.sum(cnt, axis=0)
    padded = (total + EXPERT_BLOCK - 1) // EXPERT_BLOCK * EXPERT_BLOCK
    ends = jnp.cumsum(padded)
    starts = ends - padded
    gstart = starts[None, :] + jnp.cumsum(cnt, axis=0) - cnt
    lstart = jnp.cumsum(cnt, axis=1) - cnt
    blk_start = jnp.arange(n_blocks, dtype=jnp.int32) * EXPERT_BLOCK
    blk_e = jnp.minimum(jnp.sum(ends[None, :] <= blk_start[:, None], axis=1), n_e - 1)
    onehot = blk_e[:, None] == jnp.arange(n_e)[None, :]
    used = jnp.sum(jnp.where(onehot, (starts + total)[None, :], 0), axis=1)
    blk_nv = jnp.clip(used - blk_start, 0, EXPERT_BLOCK)
    live = blk_start < ends[-1]
    blk_e = jnp.where(live, blk_e, jnp.max(jnp.where(live, blk_e, 0)))
    blk_last = (ends[-1:] // EXPERT_BLOCK - 1)
    i32 = lambda a: a.astype(jnp.int32)
    tables = (i32(cnt).reshape(-1), i32(lstart).reshape(-1), i32(gstart).reshape(-1))
    return tables, i32(blk_e), i32(blk_nv), i32(blk_last)


def _moe_ln(h1, layer, w_router, b_router, w_gate, w_up, w_down, ws_gate, ws_up, ws_down,
            ln_g, ln_b, alpha, tm):
    t, d = h1.shape
    n_e = w_router.shape[1]
    n_blocks = t * TOP_K // EXPERT_BLOCK + n_e
    lpos, gate, cnt = _router(h1, w_router.T.astype(BF16), b_router, tm)
    tables, blk_e, blk_nv, blk_last = _moe_tables(cnt, n_blocks)
    xs = _dispatch(h1, lpos, tables, n_blocks * EXPERT_BLOCK, tm, n_e)
    ys = _experts(xs, blk_e, blk_nv, blk_last, w_gate, w_up, w_down, layer, d)
    ws_gu = jnp.concatenate([ws_gate, ws_up], axis=1).astype(BF16)
    return _combine_ln(ys, lpos, gate, tables, h1, ws_gu, ws_down.astype(BF16),
                       ln_g, ln_b, alpha, tm, n_e)


def _tiles(t, s):
    return {"mm_m": min(1024, t), "mm_n": 512, "proj_m": min(512, t), "seq": min(512, s),
            "moe": min(256, t)}


def kernel(x, p, a_w_in, a_b_in, a_w_dw, a_b_dw, a_ln_g, a_ln_b, a_w_out, a_b_out, b_w_in, b_w_conv, b_w_out, c_w_grp, c_scale, ln1_g, ln1_b, ln2_g, ln2_b, router_w, router_b, exp_w_gate, exp_w_up, exp_w_down, sh_w_gate, sh_w_up, sh_w_down, ple_w_proj, ple_w_gate, ple_ln_g, ple_ln_b):
    bsz, s, d = x.shape
    depth = ln1_g.shape[0]
    t = bsz * s
    assert (t * TOP_K) % EXPERT_BLOCK == 0 and d % (2 * LANES) == 0
    alpha = (2 * depth) ** 0.25
    tl = _tiles(t, s)
    bn = min(tl["mm_n"], d)
    h = x.reshape(t, d)
    p_rows = p.reshape(depth, t, p.shape[-1])
    for i in range(depth):
        kind, j = i % 3, i // 3
        if kind == 0:
            u = _glu_proj(h, a_w_in[j].astype(BF16), a_b_in[j], tl["mm_m"], bn)
            v = _conv_ln_silu(u.reshape(bsz, s, d), a_w_dw[j], a_b_dw[j], a_ln_g[j], a_ln_b[j],
                              tl["seq"])
            h1 = _proj_residual_ln(v.reshape(t, d), a_w_out[j].astype(BF16), a_b_out[j], h,
                                   ln1_g[i], ln1_b[i], alpha, tl["proj_m"])
        elif kind == 1:
            bg, cv = _bcv_proj(h, b_w_in[j].astype(BF16), tl["mm_m"], bn)
            v = _gated_short_conv(bg.reshape(bsz, s, d), cv.reshape(bsz, s, d), b_w_conv[j],
                                  tl["seq"])
            h1 = _proj_residual_ln(v.reshape(t, d), b_w_out[j].astype(BF16),
                                   jnp.zeros((d,), F32), h, ln1_g[i], ln1_b[i], alpha,
                                   tl["proj_m"])
        else:
            h1 = _pool_mixer_ln(h.reshape(bsz, s, d), c_w_grp[j].astype(BF16), c_scale[j],
                                ln1_g[i], ln1_b[i], alpha, tl["seq"]).reshape(t, d)
        h2 = _moe_ln(h1, i, router_w[i], router_b[i], exp_w_gate, exp_w_up, exp_w_down,
                     sh_w_gate[i], sh_w_up[i], sh_w_down[i], ln2_g[i], ln2_b[i], alpha,
                     tl["moe"])
        h = _ple(h2, p_rows, i, ple_w_proj[i].astype(BF16), ple_w_gate[i].astype(BF16),
                 ple_ln_g[i], ple_ln_b[i], tl["proj_m"])
    return h.reshape(bsz, s, d)
```

```python
import functools

import jax
import jax.numpy as jnp
from jax import lax
from jax.experimental import pallas as pl
from jax.experimental.pallas import tpu as pltpu

LN_EPS = 1e-5
TOP_K = 8
N_EXPERT_GROUPS = 8
TOPK_GROUPS = 4
ROUTED_SCALE = 2.5
POOL_WINDOWS = (2, 4, 8, 16)
EXPERT_BLOCK = 512
EXPERT_SUBBLOCKS = 4
ROW_UNROLL = 4
LANES = 128
SUBLANES = 8
VMEM_LIMIT = 56 * 1024 * 1024

BF16 = jnp.bfloat16
PACKED = jnp.bfloat16
F32 = jnp.float32


def _params(*sem):
    return pltpu.CompilerParams(dimension_semantics=sem, vmem_limit_bytes=VMEM_LIMIT)


def _ln(x, g, b):
    mu = jnp.mean(x, axis=-1, keepdims=True)
    xc = x - mu
    var = jnp.mean(xc * xc, axis=-1, keepdims=True)
    return xc * lax.rsqrt(var + LN_EPS) * g + b


def _dot(a, b):
    return jnp.dot(a, b, preferred_element_type=F32)


def _silu(x):
    return x * jax.nn.sigmoid(x)


def _glu_kernel(x_ref, wa_ref, wg_ref, ba_ref, bg_ref, o_ref):
    x = x_ref[...].astype(BF16)
    a = _dot(x, wa_ref[...]) + ba_ref[...]
    g = _dot(x, wg_ref[...]) + bg_ref[...]
    o_ref[...] = a * jax.nn.sigmoid(g)


def _glu_proj(h, w_in, b_in, bm, bn):
    t, d = h.shape
    nb = d // bn
    b2 = b_in.reshape(1, 2 * d)
    return pl.pallas_call(
        _glu_kernel,
        grid=(t // bm, nb),
        in_specs=[
            pl.BlockSpec((bm, d), lambda i, j: (i, 0)),
            pl.BlockSpec((d, bn), lambda i, j: (0, j)),
            pl.BlockSpec((d, bn), lambda i, j: (0, j + nb)),
            pl.BlockSpec((1, bn), lambda i, j: (0, j)),
            pl.BlockSpec((1, bn), lambda i, j: (0, j + nb)),
        ],
        out_specs=pl.BlockSpec((bm, bn), lambda i, j: (i, j)),
        out_shape=jax.ShapeDtypeStruct((t, d), F32),
        compiler_params=_params("parallel", "arbitrary"),
        name="glu_proj",
    )(h, w_in, w_in, b2, b2)


def _bcv_kernel(x_ref, wb_ref, wc_ref, wv_ref, b_ref, cv_ref):
    x = x_ref[...].astype(BF16)
    b_ref[...] = _dot(x, wb_ref[...])
    cv_ref[...] = _dot(x, wc_ref[...]) * _dot(x, wv_ref[...])


def _bcv_proj(h, w_in, bm, bn):
    t, d = h.shape
    nb = d // bn
    return pl.pallas_call(
        _bcv_kernel,
        grid=(t // bm, nb),
        in_specs=[
            pl.BlockSpec((bm, d), lambda i, j: (i, 0)),
            pl.BlockSpec((d, bn), lambda i, j: (0, j)),
            pl.BlockSpec((d, bn), lambda i, j: (0, j + nb)),
            pl.BlockSpec((d, bn), lambda i, j: (0, j + 2 * nb)),
        ],
        out_specs=[pl.BlockSpec((bm, bn), lambda i, j: (i, j))] * 2,
        out_shape=[jax.ShapeDtypeStruct((t, d), F32)] * 2,
        compiler_params=_params("parallel", "arbitrary"),
        name="bcv_proj",
    )(h, w_in, w_in, w_in)


def _fill_window(buf_ref, halo_ref, main_ref, halo):
    first = pl.program_id(1) == 0
    buf_ref[0:halo, :] = jnp.where(first, 0.0, halo_ref[0])
    buf_ref[halo:, :] = main_ref[0]


def _conv_ln_kernel(u_ref, halo_ref, w_ref, bdw_ref, g_ref, b_ref, o_ref, buf_ref, acc_ref,
                    *, width, halo, ts, rows):
    _fill_window(buf_ref, halo_ref, u_ref, halo)
    d = buf_ref.shape[1]
    off = halo - (width - 1)

    def chunk(r, _):
        r0 = pl.multiple_of(r * rows, rows)
        for c in range(d // LANES):
            cs = slice(c * LANES, (c + 1) * LANES)
            win = buf_ref[pl.ds(r0, rows + halo), cs]
            acc = jnp.zeros((rows, LANES), F32)
            for s in range(SUBLANES):
                taps = [k for k in range(width) if (off + k) % SUBLANES == s]
                if not taps:
                    continue
                ws = pltpu.roll(win, rows + halo - s, axis=0) if s else win
                for k in taps:
                    q = off + k - s
                    acc = acc + w_ref[k:k + 1, cs] * ws[q:q + rows]
            acc_ref[pl.ds(r0, rows), cs] = acc
        return 0

    lax.fori_loop(0, ts // rows, chunk, 0)
    v = _ln(acc_ref[...] + bdw_ref[...], g_ref[...], b_ref[...])
    o_ref[0] = _silu(v).astype(o_ref.dtype)


def _conv_ln_silu(u, w_dw, b_dw, ln_g, ln_b, ts):
    bsz, s, d = u.shape
    width = w_dw.shape[0]
    halo = -(-(width - 1) // SUBLANES) * SUBLANES
    hb = ts // halo
    kern = functools.partial(_conv_ln_kernel, width=width, halo=halo, ts=ts, rows=32)
    vec = lambda a: a.reshape(1, d)
    return pl.pallas_call(
        kern,
        grid=(bsz, s // ts),
        in_specs=[
            pl.BlockSpec((1, ts, d), lambda b, i: (b, i, 0)),
            pl.BlockSpec((1, halo, d), lambda b, i: (b, jnp.maximum(i * hb - 1, 0), 0)),
            pl.BlockSpec((width, d), lambda b, i: (0, 0)),
            pl.BlockSpec((1, d), lambda b, i: (0, 0)),
            pl.BlockSpec((1, d), lambda b, i: (0, 0)),
            pl.BlockSpec((1, d), lambda b, i: (0, 0)),
        ],
        out_specs=pl.BlockSpec((1, ts, d), lambda b, i: (b, i, 0)),
        out_shape=jax.ShapeDtypeStruct((bsz, s, d), BF16),
        scratch_shapes=[pltpu.VMEM((halo + ts, d), F32), pltpu.VMEM((ts, d), F32)],
        compiler_params=_params("parallel", "arbitrary"),
        name="conv_ln_silu",
    )(u, u, w_dw, vec(b_dw), vec(ln_g), vec(ln_b))


def _gated_conv_kernel(cv_ref, halo_ref, bg_ref, w_ref, o_ref, buf_ref, *, width, halo, ts):
    _fill_window(buf_ref, halo_ref, cv_ref, halo)
    off = halo - (width - 1)
    acc = w_ref[0:1, :] * buf_ref[off:off + ts, :]
    for k in range(1, width):
        acc = acc + w_ref[k:k + 1, :] * buf_ref[off + k:off + k + ts, :]
    o_ref[0] = (bg_ref[0] * acc).astype(o_ref.dtype)


def _gated_short_conv(bg, cv, w_conv, ts):
    bsz, s, d = cv.shape
    width = w_conv.shape[0]
    halo = SUBLANES
    hb = ts // halo
    kern = functools.partial(_gated_conv_kernel, width=width, halo=halo, ts=ts)
    return pl.pallas_call(
        kern,
        grid=(bsz, s // ts),
        in_specs=[
            pl.BlockSpec((1, ts, d), lambda b, i: (b, i, 0)),
            pl.BlockSpec((1, halo, d), lambda b, i: (b, jnp.maximum(i * hb - 1, 0), 0)),
            pl.BlockSpec((1, ts, d), lambda b, i: (b, i, 0)),
            pl.BlockSpec((width, d), lambda b, i: (0, 0)),
        ],
        out_specs=pl.BlockSpec((1, ts, d), lambda b, i: (b, i, 0)),
        out_shape=jax.ShapeDtypeStruct((bsz, s, d), BF16),
        scratch_shapes=[pltpu.VMEM((halo + ts, d), F32)],
        compiler_params=_params("parallel", "arbitrary"),
        name="gated_short_conv",
    )(cv, cv, bg, w_conv)


def _proj_ln_kernel(v_ref, w_ref, bias_ref, h_ref, g_ref, b_ref, o_ref, *, alpha):
    m = _dot(v_ref[...], w_ref[...]) + bias_ref[...]
    o_ref[...] = _ln(alpha * h_ref[...] + m, g_ref[...], b_ref[...])


def _proj_residual_ln(v, w_out, b_out, h, ln_g, ln_b, alpha, bm):
    t, d = h.shape
    vec = lambda a: a.reshape(1, d)
    return pl.pallas_call(
        functools.partial(_proj_ln_kernel, alpha=alpha),
        grid=(t // bm,),
        in_specs=[
            pl.BlockSpec((bm, d), lambda i: (i, 0)),
            pl.BlockSpec((d, d), lambda i: (0, 0)),
            pl.BlockSpec((1, d), lambda i: (0, 0)),
            pl.BlockSpec((bm, d), lambda i: (i, 0)),
            pl.BlockSpec((1, d), lambda i: (0, 0)),
            pl.BlockSpec((1, d), lambda i: (0, 0)),
        ],
        out_specs=pl.BlockSpec((bm, d), lambda i: (i, 0)),
        out_shape=jax.ShapeDtypeStruct((t, d), F32),
        compiler_params=_params("parallel"),
        name="proj_residual_ln",
    )(v, w_out, vec(b_out), h, vec(ln_g), vec(ln_b))


def _pool_kernel(h_ref, halo_ref, w_ref, scale_ref, g_ref, b_ref, o_ref, buf_ref,
                 *, halo, ts, alpha):
    _fill_window(buf_ref, halo_ref, h_ref, halo)
    d = buf_ref.shape[1]
    gd = d // len(POOL_WINDOWS)
    pos = pl.program_id(1) * ts + lax.broadcasted_iota(jnp.int32, (ts, 1), 0) + 1
    ys = []
    for gi, win in enumerate(POOL_WINDOWS):
        cs = slice(gi * gd, (gi + 1) * gd)
        x = buf_ref[halo:halo + ts, cs]
        tot = x
        for j in range(1, win):
            tot = tot + buf_ref[halo - j:halo - j + ts, cs]
        cnt = jnp.minimum(pos, win).astype(F32)
        z = tot / cnt - x
        ys.append(_dot(z.astype(BF16), w_ref[gi]))
    y = jnp.concatenate(ys, axis=-1) * scale_ref[...]
    o_ref[0] = _ln(alpha * h_ref[0] + y, g_ref[...], b_ref[...])


def _pool_mixer_ln(h, w_grp, scale, ln_g, ln_b, alpha, ts):
    bsz, s, d = h.shape
    halo = max(POOL_WINDOWS)
    hb = ts // halo
    vec = lambda a: a.reshape(1, d)
    return pl.pallas_call(
        functools.partial(_pool_kernel, halo=halo, ts=ts, alpha=alpha),
        grid=(bsz, s // ts),
        in_specs=[
            pl.BlockSpec((1, ts, d), lambda b, i: (b, i, 0)),
            pl.BlockSpec((1, halo, d), lambda b, i: (b, jnp.maximum(i * hb - 1, 0), 0)),
            pl.BlockSpec(w_grp.shape, lambda b, i: (0, 0, 0)),
            pl.BlockSpec((1, d), lambda b, i: (0, 0)),
            pl.BlockSpec((1, d), lambda b, i: (0, 0)),
            pl.BlockSpec((1, d), lambda b, i: (0, 0)),
        ],
        out_specs=pl.BlockSpec((1, ts, d), lambda b, i: (b, i, 0)),
        out_shape=jax.ShapeDtypeStruct((bsz, s, d), F32),
        scratch_shapes=[pltpu.VMEM((halo + ts, d), F32)],
        compiler_params=_params("parallel", "arbitrary"),
        name="pool_mixer_ln",
    )(h, h, w_grp, vec(scale), vec(ln_g), vec(ln_b))


def _rank_desc(m):
    n = m.shape[0]
    row = lax.broadcasted_iota(jnp.int32, m.shape, 0)
    rank = jnp.zeros(m.shape, jnp.int32)
    for j in range(n):
        mj = m[j:j + 1, :]
        beats = (mj > m) | ((mj == m) & (row > j))
        rank = rank + beats.astype(jnp.int32)
    return rank


def _router_kernel(h_ref, wt_ref, b_ref, lpos_ref, gate_ref, cnt_ref, *, tm):
    n_e = wt_ref.shape[0]
    epg = n_e // N_EXPERT_GROUPS
    x = h_ref[...].astype(BF16)
    logits = lax.dot_general(wt_ref[...], x, (((1,), (1,)), ((), ())),
                             preferred_element_type=F32)
    scores = jax.nn.sigmoid(logits)
    biased = scores + b_ref[...]

    sub = lax.broadcasted_iota(jnp.int32, (epg, tm), 0)
    gscore = []
    for g in range(N_EXPERT_GROUPS):
        blk = biased[g * epg:(g + 1) * epg, :]
        m1 = jnp.max(blk, axis=0, keepdims=True)
        first = jnp.min(jnp.where(blk == m1, sub, epg), axis=0, keepdims=True)
        m2 = jnp.max(jnp.where(sub == first, -jnp.inf, blk), axis=0, keepdims=True)
        gscore.append(m1 + m2)
    gsel = _rank_desc(jnp.concatenate(gscore, axis=0)) < TOPK_GROUPS
    emask = jnp.concatenate(
        [jnp.broadcast_to(gsel[g:g + 1, :], (epg, tm)) for g in range(N_EXPERT_GROUPS)], axis=0)
    masked = jnp.where(emask, biased, -jnp.inf)
    sel = _rank_desc(masked) < TOP_K
    self32 = sel.astype(F32)
    selb = self32.astype(BF16)

    gate = jnp.where(sel, scores, 0.0)
    gate = gate / jnp.sum(gate, axis=0, keepdims=True) * ROUTED_SCALE

    er = lax.broadcasted_iota(jnp.int32, (n_e, n_e), 0)
    ec = lax.broadcasted_iota(jnp.int32, (n_e, n_e), 1)
    lower = (ec < er).astype(BF16)
    slot = _dot(lower, selb)
    tr = lax.broadcasted_iota(jnp.int32, (tm, tm), 0)
    tc = lax.broadcasted_iota(jnp.int32, (tm, tm), 1)
    before = (tr < tc).astype(BF16)
    lrank = _dot(selb, before)
    count = jnp.sum(self32, axis=1, keepdims=True)
    count_l = jnp.broadcast_to(count, (n_e, LANES))
    lstart = _dot(lower, count_l.astype(BF16))[:, 0:1]
    lpos = lstart + lrank

    lpos_rows, gate_rows = [], []
    for k in range(TOP_K):
        pick = sel & (slot == float(k))
        lpos_rows.append(jnp.sum(jnp.where(pick, lpos, 0.0), axis=0, keepdims=True))
        gate_rows.append(jnp.sum(jnp.where(pick, gate, 0.0), axis=0, keepdims=True))
    lpos_ref[...] = jnp.concatenate(lpos_rows, axis=0).astype(jnp.int32)
    gate_ref[...] = jnp.concatenate(gate_rows, axis=0)
    cnt_ref[0] = count_l.astype(jnp.int32)


def _router(h1, w_router_t, b_router, tm):
    t, d = h1.shape
    n_e = w_router_t.shape[0]
    assert tm <= 256, "tile counts must stay exactly representable in bf16"
    lpos, gate, cnt = pl.pallas_call(
        functools.partial(_router_kernel, tm=tm),
        grid=(t // tm,),
        in_specs=[
            pl.BlockSpec((tm, d), lambda i: (i, 0)),
            pl.BlockSpec((n_e, d), lambda i: (0, 0)),
            pl.BlockSpec((n_e, 1), lambda i: (0, 0)),
        ],
        out_specs=[
            pl.BlockSpec((TOP_K, tm), lambda i: (0, i)),
            pl.BlockSpec((TOP_K, tm), lambda i: (0, i)),
            pl.BlockSpec((1, n_e, LANES), lambda i: (i, 0, 0)),
        ],
        out_shape=[
            jax.ShapeDtypeStruct((TOP_K, t), jnp.int32),
            jax.ShapeDtypeStruct((TOP_K, t), F32),
            jax.ShapeDtypeStruct((t // tm, n_e, LANES), jnp.int32),
        ],
        compiler_params=_params("parallel"),
        name="router",
    )(h1, w_router_t, b_router.reshape(n_e, 1))
    return lpos, gate, cnt[:, :, 0]


HIGH_HALF = -65536


def _pack_words(lo, hi):
    bits = lambda v: lax.bitcast_convert_type(v.astype(PACKED).astype(F32), jnp.int32)
    return (bits(hi) & HIGH_HALF) | lax.shift_right_logical(bits(lo), 16)


def _word_lo(w):
    return lax.bitcast_convert_type(lax.shift_left(w, 16), F32)


def _word_hi(w):
    return lax.bitcast_convert_type(w & HIGH_HALF, F32)


def _pack_rows(x, dst_ref, r0=0):
    n, half = x.shape[0], x.shape[1] // 2
    rc = half // LANES
    words = _pack_words(x[:, :half], x[:, half:])
    for c in range(rc):
        dst_ref[pl.ds(r0 * rc + c, n, stride=rc), :] = words[:, c * LANES:(c + 1) * LANES]


def _unpack_rows(src_ref, n, rc, r0=0):
    lo, hi = [], []
    for c in range(rc):
        w = src_ref[pl.ds(r0 * rc + c, n, stride=rc), :]
        lo.append(_word_lo(w))
        hi.append(_word_hi(w))
    return jnp.concatenate(lo + hi, axis=-1)


def _segment_copies(local_ref, hbm_ref, sem, tables, tile, n_e, rc, tm, to_hbm):
    cnt_ref, lstart_ref, gstart_ref = tables

    def per_expert(e, _):
        j = tile * n_e + e
        n, ls, gs = cnt_ref[j], lstart_ref[j], gstart_ref[j]
        size = tm
        while size >= 1:
            @pl.when((n & size) != 0)
            def _(size=size):
                done = (n & ~(2 * size - 1)) * rc
                lo = pl.multiple_of(ls + done, rc)
                go = pl.multiple_of(gs + done, rc)
                loc = local_ref.at[pl.ds(lo, size * rc), :]
                glob = hbm_ref.at[pl.ds(go, size * rc), :]
                if to_hbm:
                    pltpu.make_async_copy(loc, glob, sem).start()
                else:
                    pltpu.make_async_copy(glob, loc, sem).start()
            size //= 2
        return 0

    lax.fori_loop(0, n_e, per_expert, 0)


def _wait_segments(local_ref, hbm_ref, sem):
    rows = local_ref.shape[0]
    pltpu.make_async_copy(local_ref, hbm_ref.at[pl.ds(0, rows), :], sem).wait()


def _dispatch_kernel(cnt_ref, lstart_ref, gstart_ref, lpos_ref, h_ref, xs_ref,
                     q_ref, s_ref, sem, *, tm, n_e, rc):
    i = pl.program_id(0)
    n = pl.num_programs(0)
    slot = i % 2

    @pl.when(i >= 2)
    def _():
        _wait_segments(s_ref.at[slot], xs_ref, sem.at[slot])

    _pack_rows(h_ref[...], q_ref)

    def place_into(sl):
        def place(j, _):
            for u in range(ROW_UNROLL):
                t = j * ROW_UNROLL + u
                row = q_ref[pl.ds(pl.multiple_of(t * rc, rc), rc), :]
                for k in range(TOP_K):
                    dst = pl.multiple_of(lpos_ref[t * TOP_K + k], rc)
                    s_ref[sl, pl.ds(dst, rc), :] = row
            return 0

        lax.fori_loop(0, tm // ROW_UNROLL, place, 0)

    for sl in range(2):
        pl.when(slot == sl)(functools.partial(place_into, sl))
    _segment_copies(s_ref.at[slot], xs_ref, sem.at[slot], (cnt_ref, lstart_ref, gstart_ref),
                    i, n_e, rc, tm, to_hbm=True)

    @pl.when(i == n - 1)
    def _():
        _wait_segments(s_ref.at[slot], xs_ref, sem.at[slot])

        @pl.when(n > 1)
        def _():
            _wait_segments(s_ref.at[1 - slot], xs_ref, sem.at[1 - slot])


def _dispatch(h1, lpos, tables, n_rows, tm, n_e):
    t, d = h1.shape
    rc = d // 2 // LANES
    grid_spec = pltpu.PrefetchScalarGridSpec(
        num_scalar_prefetch=3,
        grid=(t // tm,),
        in_specs=[
            pl.BlockSpec((TOP_K * tm,), lambda i, *_: (i,), memory_space=pltpu.SMEM),
            pl.BlockSpec((tm, d), lambda i, *_: (i, 0)),
        ],
        out_specs=pl.BlockSpec(memory_space=pl.ANY),
        scratch_shapes=[
            pltpu.VMEM((tm * rc, LANES), jnp.int32),
            pltpu.VMEM((2, TOP_K * tm * rc, LANES), jnp.int32),
            pltpu.SemaphoreType.DMA((2,)),
        ],
    )
    return pl.pallas_call(
        functools.partial(_dispatch_kernel, tm=tm, n_e=n_e, rc=rc),
        grid_spec=grid_spec,
        out_shape=jax.ShapeDtypeStruct((n_rows * rc, LANES), jnp.int32),
        compiler_params=_params("arbitrary"),
        name="dispatch",
    )(*tables, lpos, h1)


def _expert_kernel(be_ref, nv_ref, last_ref, x_ref, wg_ref, wu_ref, wd_ref, o_ref,
                   wgu_s, wd_s, *, blk, rc):
    i = pl.program_id(0)
    nv = nv_ref[i]
    f = wd_s.shape[0]
    fresh = (i == 0) | (be_ref[i] != be_ref[jnp.maximum(i - 1, 0)])

    @pl.when(fresh)
    def _():
        wgu_s[:, :f] = wg_ref[0, 0].astype(BF16)
        wgu_s[:, f:] = wu_ref[0, 0].astype(BF16)
        wd_s[...] = wd_ref[0, 0].astype(BF16)

    @pl.when(nv > 0)
    def _():
        sub = blk // EXPERT_SUBBLOCKS
        for sb in range(EXPERT_SUBBLOCKS):
            r0 = sb * sub
            x = _unpack_rows(x_ref, sub, rc, r0).astype(BF16)
            gu = _dot(x, wgu_s[...])
            mid = _silu(gu[:, :f]) * gu[:, f:]
            row = r0 + lax.broadcasted_iota(jnp.int32, (sub, 1), 0)
            mid = jnp.where(row < nv, mid, 0.0).astype(BF16)
            _pack_rows(_dot(mid, wd_s[...]), o_ref, r0)


def _experts(xs, blk_e, blk_nv, blk_last, w_gate, w_up, w_down, layer, d):
    rc = d // 2 // LANES
    f = w_gate.shape[-1]
    n_blocks = xs.shape[0] // (EXPERT_BLOCK * rc)
    rows = lambda i, be, nv, last: (jnp.minimum(i, last[0]), 0)
    grid_spec = pltpu.PrefetchScalarGridSpec(
        num_scalar_prefetch=3,
        grid=(n_blocks,),
        in_specs=[
            pl.BlockSpec((EXPERT_BLOCK * rc, LANES), rows),
            pl.BlockSpec((1, 1, d, f), lambda i, be, nv, last: (layer, be[i], 0, 0)),
            pl.BlockSpec((1, 1, d, f), lambda i, be, nv, last: (layer, be[i], 0, 0)),
            pl.BlockSpec((1, 1, f, d), lambda i, be, nv, last: (layer, be[i], 0, 0)),
        ],
        out_specs=pl.BlockSpec((EXPERT_BLOCK * rc, LANES), rows),
        scratch_shapes=[pltpu.VMEM((d, 2 * f), BF16), pltpu.VMEM((f, d), BF16)],
    )
    return pl.pallas_call(
        functools.partial(_expert_kernel, blk=EXPERT_BLOCK, rc=rc),
        grid_spec=grid_spec,
        out_shape=jax.ShapeDtypeStruct(xs.shape, jnp.int32),
        compiler_params=_params("arbitrary"),
        name="experts",
    )(blk_e, blk_nv, blk_last, xs, w_gate, w_up, w_down)


def _combine_kernel(cnt_ref, lstart_ref, gstart_ref, lpos_ref, gate_ref, h_ref, wsgu_ref,
                    wsd_ref, g_ref, b_ref, ys_ref, o_ref, l_ref, r_ref, sem,
                    *, tm, n_e, rc, alpha):
    i = pl.program_id(0)
    n = pl.num_programs(0)
    slot = i % 2
    tables = (cnt_ref, lstart_ref, gstart_ref)

    @pl.when(i == 0)
    def _():
        _segment_copies(l_ref.at[0], ys_ref, sem.at[0], tables, 0, n_e, rc, tm, to_hbm=False)

    @pl.when(i + 1 < n)
    def _():
        _segment_copies(l_ref.at[1 - slot], ys_ref, sem.at[1 - slot], tables, i + 1, n_e, rc, tm,
                        to_hbm=False)

    h = h_ref[...]
    f = wsd_ref.shape[0]
    su = _dot(h.astype(BF16), wsgu_ref[...])
    shared = _dot((_silu(su[:, :f]) * su[:, f:]).astype(BF16), wsd_ref[...])

    _wait_segments(l_ref.at[slot], ys_ref, sem.at[slot])

    def gather_from(sl):
        def gather(j, _):
            for u in range(ROW_UNROLL):
                t = j * ROW_UNROLL + u
                lo = hi = None
                for k in range(TOP_K):
                    src = pl.multiple_of(lpos_ref[t * TOP_K + k], rc)
                    w = l_ref[sl, pl.ds(src, rc), :]
                    gk = gate_ref[t * TOP_K + k]
                    lo = gk * _word_lo(w) if lo is None else lo + gk * _word_lo(w)
                    hi = gk * _word_hi(w) if hi is None else hi + gk * _word_hi(w)
                base = pl.multiple_of(t * 2 * rc, 2 * rc)
                r_ref[pl.ds(base, rc), :] = lo
                r_ref[pl.ds(base + rc, rc), :] = hi
            return 0

        lax.fori_loop(0, tm // ROW_UNROLL, gather, 0)

    for sl in range(2):
        pl.when(slot == sl)(functools.partial(gather_from, sl))
    routed = jnp.concatenate(
        [r_ref[pl.ds(c, tm, stride=2 * rc), :] for c in range(2 * rc)], axis=-1)
    o_ref[...] = _ln(alpha * h + (routed + shared), g_ref[...], b_ref[...])


def _combine_ln(ys, lpos, gate, tables, h1, ws_gu, ws_down, ln_g, ln_b, alpha, tm, n_e):
    t, d = h1.shape
    rc = d // 2 // LANES
    f = ws_down.shape[0]
    vec = lambda a: a.reshape(1, d)
    grid_spec = pltpu.PrefetchScalarGridSpec(
        num_scalar_prefetch=3,
        grid=(t // tm,),
        in_specs=[
            pl.BlockSpec((TOP_K * tm,), lambda i, *_: (i,), memory_space=pltpu.SMEM),
            pl.BlockSpec((TOP_K * tm,), lambda i, *_: (i,), memory_space=pltpu.SMEM),
            pl.BlockSpec((tm, d), lambda i, *_: (i, 0)),
            pl.BlockSpec((d, 2 * f), lambda i, *_: (0, 0)),
            pl.BlockSpec((f, d), lambda i, *_: (0, 0)),
            pl.BlockSpec((1, d), lambda i, *_: (0, 0)),
            pl.BlockSpec((1, d), lambda i, *_: (0, 0)),
            pl.BlockSpec(memory_space=pl.ANY),
        ],
        out_specs=pl.BlockSpec((tm, d), lambda i, *_: (i, 0)),
        scratch_shapes=[
            pltpu.VMEM((2, TOP_K * tm * rc, LANES), jnp.int32),
            pltpu.VMEM((tm * 2 * rc, LANES), F32),
            pltpu.SemaphoreType.DMA((2,)),
        ],
    )
    return pl.pallas_call(
        functools.partial(_combine_kernel, tm=tm, n_e=n_e, rc=rc, alpha=alpha),
        grid_spec=grid_spec,
        out_shape=jax.ShapeDtypeStruct((t, d), F32),
        compiler_params=_params("arbitrary"),
        name="combine_ln",
    )(*tables, lpos, gate, h1, ws_gu, ws_down, vec(ln_g), vec(ln_b), ys)


def _ple_kernel(h_ref, p_ref, wp_ref, wg_ref, g_ref, b_ref, o_ref):
    h = h_ref[...]
    e = _dot(p_ref[0].astype(BF16), wp_ref[...])
    gate = jax.nn.sigmoid(_dot(h.astype(BF16), wg_ref[...]))
    o_ref[...] = h + _ln(gate * e, g_ref[...], b_ref[...])


def _ple(h2, p, layer, w_proj, w_gate, ln_g, ln_b, bm):
    t, d = h2.shape
    pd = p.shape[-1]
    vec = lambda a: a.reshape(1, d)
    return pl.pallas_call(
        _ple_kernel,
        grid=(t // bm,),
        in_specs=[
            pl.BlockSpec((bm, d), lambda i: (i, 0)),
            pl.BlockSpec((1, bm, pd), lambda i: (layer, i, 0)),
            pl.BlockSpec((pd, d), lambda i: (0, 0)),
            pl.BlockSpec((d, d), lambda i: (0, 0)),
            pl.BlockSpec((1, d), lambda i: (0, 0)),
            pl.BlockSpec((1, d), lambda i: (0, 0)),
        ],
        out_specs=pl.BlockSpec((bm, d), lambda i: (i, 0)),
        out_shape=jax.ShapeDtypeStruct((t, d), F32),
        compiler_params=_params("parallel"),
        name="ple",
    )(h2, p, w_proj, w_gate, vec(ln_g), vec(ln_b))


def _moe_tables(cnt, n_blocks, rc):
    n_e = cnt.shape[1]
    total = jnp.sum(cnt, axis=0)
    padded = (total + EXPERT_BLOCK - 1) // EXPERT_BLOCK * EXPERT_BLOCK
    ends = jnp.cumsum(padded)
    starts = ends - padded
    gstart = starts[None, :] + jnp.cumsum(cnt, axis=0) - cnt
    lstart = jnp.cumsum(cnt, axis=1) - cnt
    blk_start = jnp.arange(n_blocks, dtype=jnp.int32) * EXPERT_BLOCK
    blk_e = jnp.minimum(jnp.sum(ends[None, :] <= blk_start[:, None], axis=1), n_e - 1)
    onehot = blk_e[:, None] == jnp.arange(n_e)[None, :]
    used = jnp.sum(jnp.where(onehot, (starts + total)[None, :], 0), axis=1)
    blk_nv = jnp.clip(used - blk_start, 0, EXPERT_BLOCK)
    live = blk_start < ends[-1]
    blk_e = jnp.where(live, blk_e, jnp.max(jnp.where(live, blk_e, 0)))
    blk_last = (ends[-1:] // EXPERT_BLOCK - 1)
    i32 = lambda a: a.astype(jnp.int32)
    tables = (i32(cnt).reshape(-1), i32(lstart * rc).reshape(-1), i32(gstart * rc).reshape(-1))
    return tables, i32(blk_e), i32(blk_nv), i32(blk_last)


def _moe_ln(h1, layer, w_router, b_router, w_gate, w_up, w_down, ws_gate, ws_up, ws_down,
            ln_g, ln_b, alpha, tm):
    t, d = h1.shape
    n_e = w_router.shape[1]
    n_blocks = t * TOP_K // EXPERT_BLOCK + n_e
    rc = d // 2 // LANES
    lpos, gate, cnt = _router(h1, w_router.T.astype(BF16), b_router, tm)
    tables, blk_e, blk_nv, blk_last = _moe_tables(cnt, n_blocks, rc)
    lpos = (lpos * rc).T.reshape(-1)
    gate = gate.T.reshape(-1)
    xs = _dispatch(h1, lpos, tables, n_blocks * EXPERT_BLOCK, tm, n_e)
    ys = _experts(xs, blk_e, blk_nv, blk_last, w_gate, w_up, w_down, layer, d)
    ws_gu = jnp.concatenate([ws_gate, ws_up], axis=1).astype(BF16)
    return _combine_ln(ys, lpos, gate, tables, h1, ws_gu, ws_down.astype(BF16),
                       ln_g, ln_b, alpha, tm, n_e)


def _tiles(t, s):
    return {"mm_m": min(1024, t), "mm_n": 512, "proj_m": min(512, t), "seq": min(512, s),
            "moe": min(256, t)}


def kernel(x, p, a_w_in, a_b_in, a_w_dw, a_b_dw, a_ln_g, a_ln_b, a_w_out, a_b_out, b_w_in, b_w_conv, b_w_out, c_w_grp, c_scale, ln1_g, ln1_b, ln2_g, ln2_b, router_w, router_b, exp_w_gate, exp_w_up, exp_w_down, sh_w_gate, sh_w_up, sh_w_down, ple_w_proj, ple_w_gate, ple_ln_g, ple_ln_b):
    bsz, s, d = x.shape
    depth = ln1_g.shape[0]
    t = bsz * s
    assert (t * TOP_K) % EXPERT_BLOCK == 0 and d % (2 * LANES) == 0
    alpha = (2 * depth) ** 0.25
    tl = _tiles(t, s)
    bn = min(tl["mm_n"], d)
    h = x.reshape(t, d)
    p_rows = p.reshape(depth, t, p.shape[-1])
    for i in range(depth):
        kind, j = i % 3, i // 3
        if kind == 0:
            u = _glu_proj(h, a_w_in[j].astype(BF16), a_b_in[j], tl["mm_m"], bn)
            v = _conv_ln_silu(u.reshape(bsz, s, d), a_w_dw[j], a_b_dw[j], a_ln_g[j], a_ln_b[j],
                              tl["seq"])
            h1 = _proj_residual_ln(v.reshape(t, d), a_w_out[j].astype(BF16), a_b_out[j], h,
                                   ln1_g[i], ln1_b[i], alpha, tl["proj_m"])
        elif kind == 1:
            bg, cv = _bcv_proj(h, b_w_in[j].astype(BF16), tl["mm_m"], bn)
            v = _gated_short_conv(bg.reshape(bsz, s, d), cv.reshape(bsz, s, d), b_w_conv[j],
                                  tl["seq"])
            h1 = _proj_residual_ln(v.reshape(t, d), b_w_out[j].astype(BF16),
                                   jnp.zeros((d,), F32), h, ln1_g[i], ln1_b[i], alpha,
                                   tl["proj_m"])
        else:
            h1 = _pool_mixer_ln(h.reshape(bsz, s, d), c_w_grp[j].astype(BF16), c_scale[j],
                                ln1_g[i], ln1_b[i], alpha, tl["seq"]).reshape(t, d)
        h2 = _moe_ln(h1, i, router_w[i], router_b[i], exp_w_gate, exp_w_up, exp_w_down,
                     sh_w_gate[i], sh_w_up[i], sh_w_down[i], ln2_g[i], ln2_b[i], alpha,
                     tl["moe"])
        h = _ple(h2, p_rows, i, ple_w_proj[i].astype(BF16), ple_w_gate[i].astype(BF16),
                 ple_ln_g[i], ple_ln_b[i], tl["proj_m"])
    return h.reshape(bsz, s, d)
```

```python
import functools

import jax
import jax.numpy as jnp
from jax import lax
from jax.experimental import pallas as pl
from jax.experimental.pallas import tpu as pltpu

LN_EPS = 1e-5
TOP_K = 8
N_EXPERT_GROUPS = 8
TOPK_GROUPS = 4
ROUTED_SCALE = 2.5
POOL_WINDOWS = (2, 4, 8, 16)
EXPERT_BLOCK = 512
EXPERT_SUBBLOCKS = 4
ROW_UNROLL = 4
LANES = 128
SUBLANES = 8
VMEM_LIMIT = 56 * 1024 * 1024

BF16 = jnp.bfloat16
PACKED = jnp.bfloat16
F32 = jnp.float32


def _params(*sem):
    return pltpu.CompilerParams(dimension_semantics=sem, vmem_limit_bytes=VMEM_LIMIT)


def _ln(x, g, b):
    mu = jnp.mean(x, axis=-1, keepdims=True)
    xc = x - mu
    var = jnp.mean(xc * xc, axis=-1, keepdims=True)
    return xc * lax.rsqrt(var + LN_EPS) * g + b


def _dot(a, b):
    return jnp.dot(a, b, preferred_element_type=F32)


def _silu(x):
    return x * jax.nn.sigmoid(x)


def _glu_kernel(x_ref, wa_ref, wg_ref, ba_ref, bg_ref, o_ref):
    x = x_ref[...].astype(BF16)
    a = _dot(x, wa_ref[...]) + ba_ref[...]
    g = _dot(x, wg_ref[...]) + bg_ref[...]
    o_ref[...] = a * jax.nn.sigmoid(g)


def _glu_proj(h, w_in, b_in, bm, bn):
    t, d = h.shape
    nb = d // bn
    b2 = b_in.reshape(1, 2 * d)
    return pl.pallas_call(
        _glu_kernel,
        grid=(t // bm, nb),
        in_specs=[
            pl.BlockSpec((bm, d), lambda i, j: (i, 0)),
            pl.BlockSpec((d, bn), lambda i, j: (0, j)),
            pl.BlockSpec((d, bn), lambda i, j: (0, j + nb)),
            pl.BlockSpec((1, bn), lambda i, j: (0, j)),
            pl.BlockSpec((1, bn), lambda i, j: (0, j + nb)),
        ],
        out_specs=pl.BlockSpec((bm, bn), lambda i, j: (i, j)),
        out_shape=jax.ShapeDtypeStruct((t, d), F32),
        compiler_params=_params("parallel", "arbitrary"),
        name="glu_proj",
    )(h, w_in, w_in, b2, b2)


def _bcv_kernel(x_ref, wb_ref, wc_ref, wv_ref, b_ref, cv_ref):
    x = x_ref[...].astype(BF16)
    b_ref[...] = _dot(x, wb_ref[...])
    cv_ref[...] = _dot(x, wc_ref[...]) * _dot(x, wv_ref[...])


def _bcv_proj(h, w_in, bm, bn):
    t, d = h.shape
    nb = d // bn
    return pl.pallas_call(
        _bcv_kernel,
        grid=(t // bm, nb),
        in_specs=[
            pl.BlockSpec((bm, d), lambda i, j: (i, 0)),
            pl.BlockSpec((d, bn), lambda i, j: (0, j)),
            pl.BlockSpec((d, bn), lambda i, j: (0, j + nb)),
            pl.BlockSpec((d, bn), lambda i, j: (0, j + 2 * nb)),
        ],
        out_specs=[pl.BlockSpec((bm, bn), lambda i, j: (i, j))] * 2,
        out_shape=[jax.ShapeDtypeStruct((t, d), F32)] * 2,
        compiler_params=_params("parallel", "arbitrary"),
        name="bcv_proj",
    )(h, w_in, w_in, w_in)


def _fill_window(buf_ref, halo_ref, main_ref, halo):
    first = pl.program_id(1) == 0
    buf_ref[0:halo, :] = jnp.where(first, 0.0, halo_ref[0])
    buf_ref[halo:, :] = main_ref[0]


def _conv_ln_kernel(u_ref, halo_ref, w_ref, bdw_ref, g_ref, b_ref, o_ref, buf_ref, acc_ref,
                    *, width, halo, ts, rows):
    _fill_window(buf_ref, halo_ref, u_ref, halo)
    d = buf_ref.shape[1]
    off = halo - (width - 1)

    def chunk(r, _):
        r0 = pl.multiple_of(r * rows, rows)
        for c in range(d // LANES):
            cs = slice(c * LANES, (c + 1) * LANES)
            win = buf_ref[pl.ds(r0, rows + halo), cs]
            acc = jnp.zeros((rows, LANES), F32)
            for s in range(SUBLANES):
                taps = [k for k in range(width) if (off + k) % SUBLANES == s]
                if not taps:
                    continue
                ws = pltpu.roll(win, rows + halo - s, axis=0) if s else win
                for k in taps:
                    q = off + k - s
                    acc = acc + w_ref[k:k + 1, cs] * ws[q:q + rows]
            acc_ref[pl.ds(r0, rows), cs] = acc
        return 0

    lax.fori_loop(0, ts // rows, chunk, 0)
    v = _ln(acc_ref[...] + bdw_ref[...], g_ref[...], b_ref[...])
    o_ref[0] = _silu(v).astype(o_ref.dtype)


def _conv_ln_silu(u, w_dw, b_dw, ln_g, ln_b, ts):
    bsz, s, d = u.shape
    width = w_dw.shape[0]
    halo = -(-(width - 1) // SUBLANES) * SUBLANES
    hb = ts // halo
    kern = functools.partial(_conv_ln_kernel, width=width, halo=halo, ts=ts, rows=32)
    vec = lambda a: a.reshape(1, d)
    return pl.pallas_call(
        kern,
        grid=(bsz, s // ts),
        in_specs=[
            pl.BlockSpec((1, ts, d), lambda b, i: (b, i, 0)),
            pl.BlockSpec((1, halo, d), lambda b, i: (b, jnp.maximum(i * hb - 1, 0), 0)),
            pl.BlockSpec((width, d), lambda b, i: (0, 0)),
            pl.BlockSpec((1, d), lambda b, i: (0, 0)),
            pl.BlockSpec((1, d), lambda b, i: (0, 0)),
            pl.BlockSpec((1, d), lambda b, i: (0, 0)),
        ],
        out_specs=pl.BlockSpec((1, ts, d), lambda b, i: (b, i, 0)),
        out_shape=jax.ShapeDtypeStruct((bsz, s, d), BF16),
        scratch_shapes=[pltpu.VMEM((halo + ts, d), F32), pltpu.VMEM((ts, d), F32)],
        compiler_params=_params("parallel", "arbitrary"),
        name="conv_ln_silu",
    )(u, u, w_dw, vec(b_dw), vec(ln_g), vec(ln_b))


def _gated_conv_kernel(cv_ref, halo_ref, bg_ref, w_ref, o_ref, buf_ref, *, width, halo, ts):
    _fill_window(buf_ref, halo_ref, cv_ref, halo)
    off = halo - (width - 1)
    acc = w_ref[0:1, :] * buf_ref[off:off + ts, :]
    for k in range(1, width):
        acc = acc + w_ref[k:k + 1, :] * buf_ref[off + k:off + k + ts, :]
    o_ref[0] = (bg_ref[0] * acc).astype(o_ref.dtype)


def _gated_short_conv(bg, cv, w_conv, ts):
    bsz, s, d = cv.shape
    width = w_conv.shape[0]
    halo = SUBLANES
    hb = ts // halo
    kern = functools.partial(_gated_conv_kernel, width=width, halo=halo, ts=ts)
    return pl.pallas_call(
        kern,
        grid=(bsz, s // ts),
        in_specs=[
            pl.BlockSpec((1, ts, d), lambda b, i: (b, i, 0)),
            pl.BlockSpec((1, halo, d), lambda b, i: (b, jnp.maximum(i * hb - 1, 0), 0)),
            pl.BlockSpec((1, ts, d), lambda b, i: (b, i, 0)),
            pl.BlockSpec((width, d), lambda b, i: (0, 0)),
        ],
        out_specs=pl.BlockSpec((1, ts, d), lambda b, i: (b, i, 0)),
        out_shape=jax.ShapeDtypeStruct((bsz, s, d), BF16),
        scratch_shapes=[pltpu.VMEM((halo + ts, d), F32)],
        compiler_params=_params("parallel", "arbitrary"),
        name="gated_short_conv",
    )(cv, cv, bg, w_conv)


def _proj_ln_kernel(v_ref, w_ref, bias_ref, h_ref, g_ref, b_ref, o_ref, *, alpha):
    m = _dot(v_ref[...], w_ref[...]) + bias_ref[...]
    o_ref[...] = _ln(alpha * h_ref[...] + m, g_ref[...], b_ref[...])


def _proj_residual_ln(v, w_out, b_out, h, ln_g, ln_b, alpha, bm):
    t, d = h.shape
    vec = lambda a: a.reshape(1, d)
    return pl.pallas_call(
        functools.partial(_proj_ln_kernel, alpha=alpha),
        grid=(t // bm,),
        in_specs=[
            pl.BlockSpec((bm, d), lambda i: (i, 0)),
            pl.BlockSpec((d, d), lambda i: (0, 0)),
            pl.BlockSpec((1, d), lambda i: (0, 0)),
            pl.BlockSpec((bm, d), lambda i: (i, 0)),
            pl.BlockSpec((1, d), lambda i: (0, 0)),
            pl.BlockSpec((1, d), lambda i: (0, 0)),
        ],
        out_specs=pl.BlockSpec((bm, d), lambda i: (i, 0)),
        out_shape=jax.ShapeDtypeStruct((t, d), F32),
        compiler_params=_params("parallel"),
        name="proj_residual_ln",
    )(v, w_out, vec(b_out), h, vec(ln_g), vec(ln_b))


def _pool_kernel(h_ref, halo_ref, w_ref, scale_ref, g_ref, b_ref, o_ref, buf_ref,
                 *, halo, ts, alpha):
    _fill_window(buf_ref, halo_ref, h_ref, halo)
    d = buf_ref.shape[1]
    gd = d // len(POOL_WINDOWS)
    pos = pl.program_id(1) * ts + lax.broadcasted_iota(jnp.int32, (ts, 1), 0) + 1
    ys = []
    for gi, win in enumerate(POOL_WINDOWS):
        cs = slice(gi * gd, (gi + 1) * gd)
        x = buf_ref[halo:halo + ts, cs]
        tot = x
        for j in range(1, win):
            tot = tot + buf_ref[halo - j:halo - j + ts, cs]
        cnt = jnp.minimum(pos, win).astype(F32)
        z = tot / cnt - x
        ys.append(_dot(z.astype(BF16), w_ref[gi]))
    y = jnp.concatenate(ys, axis=-1) * scale_ref[...]
    o_ref[0] = _ln(alpha * h_ref[0] + y, g_ref[...], b_ref[...])


def _pool_mixer_ln(h, w_grp, scale, ln_g, ln_b, alpha, ts):
    bsz, s, d = h.shape
    halo = max(POOL_WINDOWS)
    hb = ts // halo
    vec = lambda a: a.reshape(1, d)
    return pl.pallas_call(
        functools.partial(_pool_kernel, halo=halo, ts=ts, alpha=alpha),
        grid=(bsz, s // ts),
        in_specs=[
            pl.BlockSpec((1, ts, d), lambda b, i: (b, i, 0)),
            pl.BlockSpec((1, halo, d), lambda b, i: (b, jnp.maximum(i * hb - 1, 0), 0)),
            pl.BlockSpec(w_grp.shape, lambda b, i: (0, 0, 0)),
            pl.BlockSpec((1, d), lambda b, i: (0, 0)),
            pl.BlockSpec((1, d), lambda b, i: (0, 0)),
            pl.BlockSpec((1, d), lambda b, i: (0, 0)),
        ],
        out_specs=pl.BlockSpec((1, ts, d), lambda b, i: (b, i, 0)),
        out_shape=jax.ShapeDtypeStruct((bsz, s, d), F32),
        scratch_shapes=[pltpu.VMEM((halo + ts, d), F32)],
        compiler_params=_params("parallel", "arbitrary"),
        name="pool_mixer_ln",
    )(h, h, w_grp, vec(scale), vec(ln_g), vec(ln_b))


def _rank_desc(m):
    n = m.shape[0]
    row = lax.broadcasted_iota(jnp.int32, m.shape, 0)
    rank = jnp.zeros(m.shape, jnp.int32)
    for j in range(n):
        mj = m[j:j + 1, :]
        beats = (mj > m) | ((mj == m) & (row > j))
        rank = rank + beats.astype(jnp.int32)
    return rank


def _router_kernel(h_ref, wt_ref, b_ref, lpos_ref, gate_ref, cnt_ref, *, tm):
    n_e = wt_ref.shape[0]
    epg = n_e // N_EXPERT_GROUPS
    x = h_ref[...].astype(BF16)
    logits = lax.dot_general(wt_ref[...], x, (((1,), (1,)), ((), ())),
                             preferred_element_type=F32)
    scores = jax.nn.sigmoid(logits)
    biased = scores + b_ref[...]

    sub = lax.broadcasted_iota(jnp.int32, (epg, tm), 0)
    gscore = []
    for g in range(N_EXPERT_GROUPS):
        blk = biased[g * epg:(g + 1) * epg, :]
        m1 = jnp.max(blk, axis=0, keepdims=True)
        first = jnp.min(jnp.where(blk == m1, sub, epg), axis=0, keepdims=True)
        m2 = jnp.max(jnp.where(sub == first, -jnp.inf, blk), axis=0, keepdims=True)
        gscore.append(m1 + m2)
    gsel = _rank_desc(jnp.concatenate(gscore, axis=0)) < TOPK_GROUPS
    emask = jnp.concatenate(
        [jnp.broadcast_to(gsel[g:g + 1, :], (epg, tm)) for g in range(N_EXPERT_GROUPS)], axis=0)
    masked = jnp.where(emask, biased, -jnp.inf)
    sel = _rank_desc(masked) < TOP_K
    self32 = sel.astype(F32)
    selb = self32.astype(BF16)

    gate = jnp.where(sel, scores, 0.0)
    gate = gate / jnp.sum(gate, axis=0, keepdims=True) * ROUTED_SCALE

    er = lax.broadcasted_iota(jnp.int32, (n_e, n_e), 0)
    ec = lax.broadcasted_iota(jnp.int32, (n_e, n_e), 1)
    lower = (ec < er).astype(BF16)
    slot = _dot(lower, selb)
    tr = lax.broadcasted_iota(jnp.int32, (tm, tm), 0)
    tc = lax.broadcasted_iota(jnp.int32, (tm, tm), 1)
    before = (tr < tc).astype(BF16)
    lrank = _dot(selb, before)
    count = jnp.sum(self32, axis=1, keepdims=True)
    count_l = jnp.broadcast_to(count, (n_e, LANES))
    lstart = _dot(lower, count_l.astype(BF16))[:, 0:1]
    lpos = lstart + lrank

    lpos_rows, gate_rows = [], []
    for k in range(TOP_K):
        pick = sel & (slot == float(k))
        lpos_rows.append(jnp.sum(jnp.where(pick, lpos, 0.0), axis=0, keepdims=True))
        gate_rows.append(jnp.sum(jnp.where(pick, gate, 0.0), axis=0, keepdims=True))
    lpos_ref[...] = jnp.concatenate(lpos_rows, axis=0).astype(jnp.int32)
    gate_ref[...] = jnp.concatenate(gate_rows, axis=0)
    cnt_ref[0] = count_l.astype(jnp.int32)


def _router(h1, w_router_t, b_router, tm):
    t, d = h1.shape
    n_e = w_router_t.shape[0]
    assert tm <= 256, "tile counts must stay exactly representable in bf16"
    lpos, gate, cnt = pl.pallas_call(
        functools.partial(_router_kernel, tm=tm),
        grid=(t // tm,),
        in_specs=[
            pl.BlockSpec((tm, d), lambda i: (i, 0)),
            pl.BlockSpec((n_e, d), lambda i: (0, 0)),
            pl.BlockSpec((n_e, 1), lambda i: (0, 0)),
        ],
        out_specs=[
            pl.BlockSpec((TOP_K, tm), lambda i: (0, i)),
            pl.BlockSpec((TOP_K, tm), lambda i: (0, i)),
            pl.BlockSpec((1, n_e, LANES), lambda i: (i, 0, 0)),
        ],
        out_shape=[
            jax.ShapeDtypeStruct((TOP_K, t), jnp.int32),
            jax.ShapeDtypeStruct((TOP_K, t), F32),
            jax.ShapeDtypeStruct((t // tm, n_e, LANES), jnp.int32),
        ],
        compiler_params=_params("parallel"),
        name="router",
    )(h1, w_router_t, b_router.reshape(n_e, 1))
    return lpos, gate, cnt[:, :, 0]


HIGH_HALF = -65536


def _pack_words(lo, hi):
    bits = lambda v: lax.bitcast_convert_type(v.astype(PACKED).astype(F32), jnp.int32)
    return (bits(hi) & HIGH_HALF) | lax.shift_right_logical(bits(lo), 16)


def _word_lo(w):
    return lax.bitcast_convert_type(lax.shift_left(w, 16), F32)


def _word_hi(w):
    return lax.bitcast_convert_type(w & HIGH_HALF, F32)


def _pack_rows(x, dst_ref, r0=0):
    n, half = x.shape[0], x.shape[1] // 2
    rc = half // LANES
    words = _pack_words(x[:, :half], x[:, half:])
    for c in range(rc):
        dst_ref[pl.ds(r0 * rc + c, n, stride=rc), :] = words[:, c * LANES:(c + 1) * LANES]


def _unpack_rows(src_ref, n, rc, r0=0):
    lo, hi = [], []
    for c in range(rc):
        w = src_ref[pl.ds(r0 * rc + c, n, stride=rc), :]
        lo.append(_word_lo(w))
        hi.append(_word_hi(w))
    return jnp.concatenate(lo + hi, axis=-1)


def _segment_copies(local_ref, hbm_ref, sem, tables, tile, n_e, rc, tm, to_hbm):
    cnt_ref, lstart_ref, gstart_ref = tables

    def per_expert(e, _):
        j = tile * n_e + e
        n, ls, gs = cnt_ref[j], lstart_ref[j], gstart_ref[j]
        size = tm
        while size >= 1:
            @pl.when((n & size) != 0)
            def _(size=size):
                done = (n & ~(2 * size - 1)) * rc
                lo = pl.multiple_of(ls + done, rc)
                go = pl.multiple_of(gs + done, rc)
                loc = local_ref.at[pl.ds(lo, size * rc), :]
                glob = hbm_ref.at[pl.ds(go, size * rc), :]
                if to_hbm:
                    pltpu.make_async_copy(loc, glob, sem).start()
                else:
                    pltpu.make_async_copy(glob, loc, sem).start()
            size //= 2
        return 0

    lax.fori_loop(0, n_e, per_expert, 0)


def _wait_segments(local_ref, hbm_ref, sem):
    rows = local_ref.shape[0]
    pltpu.make_async_copy(local_ref, hbm_ref.at[pl.ds(0, rows), :], sem).wait()


def _dispatch_kernel(cnt_ref, lstart_ref, gstart_ref, lpos_ref, h_ref, xs_ref,
                     q_ref, s_ref, sem, *, tm, n_e, rc):
    i = pl.program_id(0)
    n = pl.num_programs(0)
    slot = i % 2

    @pl.when(i >= 2)
    def _():
        _wait_segments(s_ref.at[slot], xs_ref, sem.at[slot])

    _pack_rows(h_ref[...], q_ref)

    def place_into(sl):
        def place(j, _):
            for u in range(ROW_UNROLL):
                t = j * ROW_UNROLL + u
                row = q_ref[pl.ds(pl.multiple_of(t * rc, rc), rc), :]
                for k in range(TOP_K):
                    dst = pl.multiple_of(lpos_ref[t * TOP_K + k], rc)
                    s_ref[sl, pl.ds(dst, rc), :] = row
            return 0

        lax.fori_loop(0, tm // ROW_UNROLL, place, 0)

    for sl in range(2):
        pl.when(slot == sl)(functools.partial(place_into, sl))
    _segment_copies(s_ref.at[slot], xs_ref, sem.at[slot], (cnt_ref, lstart_ref, gstart_ref),
                    i, n_e, rc, tm, to_hbm=True)

    @pl.when(i == n - 1)
    def _():
        _wait_segments(s_ref.at[slot], xs_ref, sem.at[slot])

        @pl.when(n > 1)
        def _():
            _wait_segments(s_ref.at[1 - slot], xs_ref, sem.at[1 - slot])


def _dispatch(h1, lpos, tables, n_rows, tm, n_e):
    t, d = h1.shape
    rc = d // 2 // LANES
    grid_spec = pltpu.PrefetchScalarGridSpec(
        num_scalar_prefetch=3,
        grid=(t // tm,),
        in_specs=[
            pl.BlockSpec((TOP_K * tm,), lambda i, *_: (i,), memory_space=pltpu.SMEM),
            pl.BlockSpec((tm, d), lambda i, *_: (i, 0)),
        ],
        out_specs=pl.BlockSpec(memory_space=pl.ANY),
        scratch_shapes=[
            pltpu.VMEM((tm * rc, LANES), jnp.int32),
            pltpu.VMEM((2, TOP_K * tm * rc, LANES), jnp.int32),
            pltpu.SemaphoreType.DMA((2,)),
        ],
    )
    return pl.pallas_call(
        functools.partial(_dispatch_kernel, tm=tm, n_e=n_e, rc=rc),
        grid_spec=grid_spec,
        out_shape=jax.ShapeDtypeStruct((n_rows * rc, LANES), jnp.int32),
        compiler_params=_params("arbitrary"),
        name="dispatch",
    )(*tables, lpos, h1)


def _expert_kernel(be_ref, nv_ref, nxt_ref, live_ref, xs_ref, wg_ref, wu_ref, wd_ref, ys_ref,
                   xbuf, ybuf, wg_f, wu_f, wd_f, wgu_s, wd_s, sem_x, sem_y, sem_w,
                   *, layer, blk, rc):
    n_live = live_ref[0]
    f = wd_s.shape[0]
    lines = blk * rc

    def x_copy(g, slot):
        src = xs_ref.at[pl.ds(pl.multiple_of(g * lines, lines), lines), :]
        return pltpu.make_async_copy(src, xbuf.at[slot], sem_x.at[slot])

    def y_copy(g, slot):
        dst = ys_ref.at[pl.ds(pl.multiple_of(g * lines, lines), lines), :]
        return pltpu.make_async_copy(ybuf.at[slot], dst, sem_y.at[slot])

    def w_copies(e, slot):
        return (pltpu.make_async_copy(wg_ref.at[layer, e], wg_f.at[slot], sem_w.at[slot]),
                pltpu.make_async_copy(wu_ref.at[layer, e], wu_f.at[slot], sem_w.at[slot]),
                pltpu.make_async_copy(wd_ref.at[layer, e], wd_f.at[slot], sem_w.at[slot]))

    x_copy(0, 0).start()
    for cp in w_copies(be_ref[0], 0):
        cp.start()

    def block(g, wslot):
        slot = g % 2
        e = be_ref[g]
        fresh = (g == 0) | (e != be_ref[jnp.maximum(g - 1, 0)])

        @pl.when(fresh)
        def _():
            for cp in w_copies(e, wslot):
                cp.wait()
            nxt = nxt_ref[g]

            @pl.when(nxt >= 0)
            def _():
                for cp in w_copies(nxt, 1 - wslot):
                    cp.start()

            wgu_s[:, :f] = wg_f[wslot].astype(BF16)
            wgu_s[:, f:] = wu_f[wslot].astype(BF16)
            wd_s[...] = wd_f[wslot].astype(BF16)

        x_copy(g, slot).wait()

        @pl.when(g + 1 < n_live)
        def _():
            x_copy(g + 1, 1 - slot).start()

        @pl.when(g >= 2)
        def _():
            y_copy(g - 2, slot).wait()

        nv = nv_ref[g]
        xin, yout = xbuf.at[slot], ybuf.at[slot]
        sub = blk // EXPERT_SUBBLOCKS
        for sb in range(EXPERT_SUBBLOCKS):
            r0 = sb * sub
            x = _unpack_rows(xin, sub, rc, r0).astype(BF16)
            gu = _dot(x, wgu_s[...])
            mid = _silu(gu[:, :f]) * gu[:, f:]
            row = r0 + lax.broadcasted_iota(jnp.int32, (sub, 1), 0)
            mid = jnp.where(row < nv, mid, 0.0).astype(BF16)
            _pack_rows(_dot(mid, wd_s[...]), yout, r0)
        y_copy(g, slot).start()
        return jnp.where(fresh, 1 - wslot, wslot)

    lax.fori_loop(0, n_live, block, 0)
    last = n_live - 1
    y_copy(last, last % 2).wait()

    @pl.when(n_live >= 2)
    def _():
        y_copy(last - 1, (last - 1) % 2).wait()


def _experts(xs, blk_e, blk_nv, blk_nxt, n_live, w_gate, w_up, w_down, layer, d):
    rc = d // 2 // LANES
    f = w_gate.shape[-1]
    lines = EXPERT_BLOCK * rc
    anyspace = pl.BlockSpec(memory_space=pl.ANY)
    grid_spec = pltpu.PrefetchScalarGridSpec(
        num_scalar_prefetch=4,
        grid=(1,),
        in_specs=[anyspace] * 4,
        out_specs=anyspace,
        scratch_shapes=[
            pltpu.VMEM((2, lines, LANES), jnp.int32),
            pltpu.VMEM((2, lines, LANES), jnp.int32),
            pltpu.VMEM((2, d, f), F32),
            pltpu.VMEM((2, d, f), F32),
            pltpu.VMEM((2, f, d), F32),
            pltpu.VMEM((d, 2 * f), BF16),
            pltpu.VMEM((f, d), BF16),
            pltpu.SemaphoreType.DMA((2,)),
            pltpu.SemaphoreType.DMA((2,)),
            pltpu.SemaphoreType.DMA((2,)),
        ],
    )
    return pl.pallas_call(
        functools.partial(_expert_kernel, layer=layer, blk=EXPERT_BLOCK, rc=rc),
        grid_spec=grid_spec,
        out_shape=jax.ShapeDtypeStruct(xs.shape, jnp.int32),
        compiler_params=_params("arbitrary"),
        name="experts",
    )(blk_e, blk_nv, blk_nxt, n_live, xs, w_gate, w_up, w_down)


def _combine_kernel(cnt_ref, lstart_ref, gstart_ref, lpos_ref, gate_ref, h_ref, wsgu_ref,
                    wsd_ref, g_ref, b_ref, ys_ref, o_ref, l_ref, r_ref, sem,
                    *, tm, n_e, rc, alpha):
    i = pl.program_id(0)
    n = pl.num_programs(0)
    slot = i % 2
    tables = (cnt_ref, lstart_ref, gstart_ref)

    @pl.when(i == 0)
    def _():
        _segment_copies(l_ref.at[0], ys_ref, sem.at[0], tables, 0, n_e, rc, tm, to_hbm=False)

    @pl.when(i + 1 < n)
    def _():
        _segment_copies(l_ref.at[1 - slot], ys_ref, sem.at[1 - slot], tables, i + 1, n_e, rc, tm,
                        to_hbm=False)

    h = h_ref[...]
    f = wsd_ref.shape[0]
    su = _dot(h.astype(BF16), wsgu_ref[...])
    shared = _dot((_silu(su[:, :f]) * su[:, f:]).astype(BF16), wsd_ref[...])

    _wait_segments(l_ref.at[slot], ys_ref, sem.at[slot])

    def gather_from(sl):
        def gather(j, _):
            for u in range(ROW_UNROLL):
                t = j * ROW_UNROLL + u
                lo = hi = None
                for k in range(TOP_K):
                    src = pl.multiple_of(lpos_ref[t * TOP_K + k], rc)
                    w = l_ref[sl, pl.ds(src, rc), :]
                    gk = gate_ref[t * TOP_K + k]
                    lo = gk * _word_lo(w) if lo is None else lo + gk * _word_lo(w)
                    hi = gk * _word_hi(w) if hi is None else hi + gk * _word_hi(w)
                base = pl.multiple_of(t * 2 * rc, 2 * rc)
                r_ref[pl.ds(base, rc), :] = lo
                r_ref[pl.ds(base + rc, rc), :] = hi
            return 0

        lax.fori_loop(0, tm // ROW_UNROLL, gather, 0)

    for sl in range(2):
        pl.when(slot == sl)(functools.partial(gather_from, sl))
    routed = jnp.concatenate(
        [r_ref[pl.ds(c, tm, stride=2 * rc), :] for c in range(2 * rc)], axis=-1)
    o_ref[...] = _ln(alpha * h + (routed + shared), g_ref[...], b_ref[...])


def _combine_ln(ys, lpos, gate, tables, h1, ws_gu, ws_down, ln_g, ln_b, alpha, tm, n_e):
    t, d = h1.shape
    rc = d // 2 // LANES
    f = ws_down.shape[0]
    vec = lambda a: a.reshape(1, d)
    grid_spec = pltpu.PrefetchScalarGridSpec(
        num_scalar_prefetch=3,
        grid=(t // tm,),
        in_specs=[
            pl.BlockSpec((TOP_K * tm,), lambda i, *_: (i,), memory_space=pltpu.SMEM),
            pl.BlockSpec((TOP_K * tm,), lambda i, *_: (i,), memory_space=pltpu.SMEM),
            pl.BlockSpec((tm, d), lambda i, *_: (i, 0)),
            pl.BlockSpec((d, 2 * f), lambda i, *_: (0, 0)),
            pl.BlockSpec((f, d), lambda i, *_: (0, 0)),
            pl.BlockSpec((1, d), lambda i, *_: (0, 0)),
            pl.BlockSpec((1, d), lambda i, *_: (0, 0)),
            pl.BlockSpec(memory_space=pl.ANY),
        ],
        out_specs=pl.BlockSpec((tm, d), lambda i, *_: (i, 0)),
        scratch_shapes=[
            pltpu.VMEM((2, TOP_K * tm * rc, LANES), jnp.int32),
            pltpu.VMEM((tm * 2 * rc, LANES), F32),
            pltpu.SemaphoreType.DMA((2,)),
        ],
    )
    return pl.pallas_call(
        functools.partial(_combine_kernel, tm=tm, n_e=n_e, rc=rc, alpha=alpha),
        grid_spec=grid_spec,
        out_shape=jax.ShapeDtypeStruct((t, d), F32),
        compiler_params=_params("arbitrary"),
        name="combine_ln",
    )(*tables, lpos, gate, h1, ws_gu, ws_down, vec(ln_g), vec(ln_b), ys)


def _ple_kernel(h_ref, p_ref, wp_ref, wg_ref, g_ref, b_ref, o_ref):
    h = h_ref[...]
    e = _dot(p_ref[0].astype(BF16), wp_ref[...])
    gate = jax.nn.sigmoid(_dot(h.astype(BF16), wg_ref[...]))
    o_ref[...] = h + _ln(gate * e, g_ref[...], b_ref[...])


def _ple(h2, p, layer, w_proj, w_gate, ln_g, ln_b, bm):
    t, d = h2.shape
    pd = p.shape[-1]
    vec = lambda a: a.reshape(1, d)
    return pl.pallas_call(
        _ple_kernel,
        grid=(t // bm,),
        in_specs=[
            pl.BlockSpec((bm, d), lambda i: (i, 0)),
            pl.BlockSpec((1, bm, pd), lambda i: (layer, i, 0)),
            pl.BlockSpec((pd, d), lambda i: (0, 0)),
            pl.BlockSpec((d, d), lambda i: (0, 0)),
            pl.BlockSpec((1, d), lambda i: (0, 0)),
            pl.BlockSpec((1, d), lambda i: (0, 0)),
        ],
        out_specs=pl.BlockSpec((bm, d), lambda i: (i, 0)),
        out_shape=jax.ShapeDtypeStruct((t, d), F32),
        compiler_params=_params("parallel"),
        name="ple",
    )(h2, p, w_proj, w_gate, vec(ln_g), vec(ln_b))


def _moe_tables(cnt, n_blocks, rc):
    n_e = cnt.shape[1]
    total = jnp.sum(cnt, axis=0)
    padded = (total + EXPERT_BLOCK - 1) // EXPERT_BLOCK * EXPERT_BLOCK
    ends = jnp.cumsum(padded)
    starts = ends - padded
    gstart = starts[None, :] + jnp.cumsum(cnt, axis=0) - cnt
    lstart = jnp.cumsum(cnt, axis=1) - cnt
    blk_start = jnp.arange(n_blocks, dtype=jnp.int32) * EXPERT_BLOCK
    blk_e = jnp.minimum(jnp.sum(ends[None, :] <= blk_start[:, None], axis=1), n_e - 1)
    onehot = blk_e[:, None] == jnp.arange(n_e)[None, :]
    used = jnp.sum(jnp.where(onehot, (starts + total)[None, :], 0), axis=1)
    blk_nv = jnp.clip(used - blk_start, 0, EXPERT_BLOCK)
    ids = jnp.arange(n_e)
    later = (ids[None, :] > ids[:, None]) & (total[None, :] > 0)
    nxt_e = jnp.min(jnp.where(later, ids[None, :], n_e), axis=1)
    nxt_e = jnp.where(nxt_e == n_e, -1, nxt_e)
    blk_nxt = jnp.sum(jnp.where(onehot, nxt_e[None, :], 0), axis=1)
    n_live = ends[-1:] // EXPERT_BLOCK
    i32 = lambda a: a.astype(jnp.int32)
    tables = (i32(cnt).reshape(-1), i32(lstart * rc).reshape(-1), i32(gstart * rc).reshape(-1))
    return tables, i32(blk_e), i32(blk_nv), i32(blk_nxt), i32(n_live)


def _moe_ln(h1, layer, w_router, b_router, w_gate, w_up, w_down, ws_gate, ws_up, ws_down,
            ln_g, ln_b, alpha, tm):
    t, d = h1.shape
    n_e = w_router.shape[1]
    n_blocks = t * TOP_K // EXPERT_BLOCK + n_e
    rc = d // 2 // LANES
    lpos, gate, cnt = _router(h1, w_router.T.astype(BF16), b_router, tm)
    tables, blk_e, blk_nv, blk_nxt, n_live = _moe_tables(cnt, n_blocks, rc)
    lpos = (lpos * rc).T.reshape(-1)
    gate = gate.T.reshape(-1)
    xs = _dispatch(h1, lpos, tables, n_blocks * EXPERT_BLOCK, tm, n_e)
    ys = _experts(xs, blk_e, blk_nv, blk_nxt, n_live, w_gate, w_up, w_down, layer, d)
    ws_gu = jnp.concatenate([ws_gate, ws_up], axis=1).astype(BF16)
    return _combine_ln(ys, lpos, gate, tables, h1, ws_gu, ws_down.astype(BF16),
                       ln_g, ln_b, alpha, tm, n_e)


def _tiles(t, s):
    return {"mm_m": min(1024, t), "mm_n": 512, "glu_n": 1024, "proj_m": min(512, t),
            "seq": min(512, s),
            "moe": min(256, t)}


def kernel(x, p, a_w_in, a_b_in, a_w_dw, a_b_dw, a_ln_g, a_ln_b, a_w_out, a_b_out, b_w_in, b_w_conv, b_w_out, c_w_grp, c_scale, ln1_g, ln1_b, ln2_g, ln2_b, router_w, router_b, exp_w_gate, exp_w_up, exp_w_down, sh_w_gate, sh_w_up, sh_w_down, ple_w_proj, ple_w_gate, ple_ln_g, ple_ln_b):
    bsz, s, d = x.shape
    depth = ln1_g.shape[0]
    t = bsz * s
    assert (t * TOP_K) % EXPERT_BLOCK == 0 and d % (2 * LANES) == 0
    alpha = (2 * depth) ** 0.25
    tl = _tiles(t, s)
    bn = min(tl["mm_n"], d)
    h = x.reshape(t, d)
    p_rows = p.reshape(depth, t, p.shape[-1])
    for i in range(depth):
        kind, j = i % 3, i // 3
        if kind == 0:
            u = _glu_proj(h, a_w_in[j].astype(BF16), a_b_in[j], tl["mm_m"], min(tl["glu_n"], d))
            v = _conv_ln_silu(u.reshape(bsz, s, d), a_w_dw[j], a_b_dw[j], a_ln_g[j], a_ln_b[j],
                              tl["seq"])
            h1 = _proj_residual_ln(v.reshape(t, d), a_w_out[j].astype(BF16), a_b_out[j], h,
                                   ln1_g[i], ln1_b[i], alpha, tl["proj_m"])
        elif kind == 1:
            bg, cv = _bcv_proj(h, b_w_in[j].astype(BF16), tl["mm_m"], bn)
            v = _gated_short_conv(bg.reshape(bsz, s, d), cv.reshape(bsz, s, d), b_w_conv[j],
                                  tl["seq"])
            h1 = _proj_residual_ln(v.reshape(t, d), b_w_out[j].astype(BF16),
                                   jnp.zeros((d,), F32), h, ln1_g[i], ln1_b[i], alpha,
                                   tl["proj_m"])
        else:
            h1 = _pool_mixer_ln(h.reshape(bsz, s, d), c_w_grp[j].astype(BF16), c_scale[j],
                                ln1_g[i], ln1_b[i], alpha, tl["seq"]).reshape(t, d)
        h2 = _moe_ln(h1, i, router_w[i], router_b[i], exp_w_gate, exp_w_up, exp_w_down,
                     sh_w_gate[i], sh_w_up[i], sh_w_down[i], ln2_g[i], ln2_b[i], alpha,
                     tl["moe"])
        h = _ple(h2, p_rows, i, ple_w_proj[i].astype(BF16), ple_w_gate[i].astype(BF16),
                 ple_ln_g[i], ple_ln_b[i], tl["proj_m"])
    return h.reshape(bsz, s, d)
```

```python
import functools

import jax
import jax.numpy as jnp
from jax import lax
from jax.experimental import pallas as pl
from jax.experimental.pallas import tpu as pltpu

LN_EPS = 1e-5
TOP_K = 8
N_EXPERT_GROUPS = 8
TOPK_GROUPS = 4
ROUTED_SCALE = 2.5
POOL_WINDOWS = (2, 4, 8, 16)
EXPERT_BLOCK = 512
EXPERT_SUBBLOCKS = 4
ROW_UNROLL = 8
LANES = 128
SUBLANES = 8
VMEM_LIMIT = 56 * 1024 * 1024

BF16 = jnp.bfloat16
PACKED = jnp.bfloat16
F32 = jnp.float32


def _params(*sem):
    return pltpu.CompilerParams(dimension_semantics=sem, vmem_limit_bytes=VMEM_LIMIT)


def _ln(x, g, b):
    mu = jnp.mean(x, axis=-1, keepdims=True)
    xc = x - mu
    var = jnp.mean(xc * xc, axis=-1, keepdims=True)
    return xc * lax.rsqrt(var + LN_EPS) * g + b


def _dot(a, b):
    return jnp.dot(a, b, preferred_element_type=F32)


def _silu(x):
    return x * jax.nn.sigmoid(x)


def _glu_kernel(x_ref, wa_ref, wg_ref, ba_ref, bg_ref, o_ref):
    x = x_ref[...].astype(BF16)
    a = _dot(x, wa_ref[...]) + ba_ref[...]
    g = _dot(x, wg_ref[...]) + bg_ref[...]
    o_ref[...] = a * jax.nn.sigmoid(g)


def _glu_proj(h, w_in, b_in, bm, bn):
    t, d = h.shape
    nb = d // bn
    b2 = b_in.reshape(1, 2 * d)
    return pl.pallas_call(
        _glu_kernel,
        grid=(t // bm, nb),
        in_specs=[
            pl.BlockSpec((bm, d), lambda i, j: (i, 0)),
            pl.BlockSpec((d, bn), lambda i, j: (0, j)),
            pl.BlockSpec((d, bn), lambda i, j: (0, j + nb)),
            pl.BlockSpec((1, bn), lambda i, j: (0, j)),
            pl.BlockSpec((1, bn), lambda i, j: (0, j + nb)),
        ],
        out_specs=pl.BlockSpec((bm, bn), lambda i, j: (i, j)),
        out_shape=jax.ShapeDtypeStruct((t, d), F32),
        compiler_params=_params("parallel", "arbitrary"),
        name="glu_proj",
    )(h, w_in, w_in, b2, b2)


def _bcv_kernel(x_ref, wb_ref, wc_ref, wv_ref, b_ref, cv_ref):
    x = x_ref[...].astype(BF16)
    b_ref[...] = _dot(x, wb_ref[...])
    cv_ref[...] = _dot(x, wc_ref[...]) * _dot(x, wv_ref[...])


def _bcv_proj(h, w_in, bm, bn):
    t, d = h.shape
    nb = d // bn
    return pl.pallas_call(
        _bcv_kernel,
        grid=(t // bm, nb),
        in_specs=[
            pl.BlockSpec((bm, d), lambda i, j: (i, 0)),
            pl.BlockSpec((d, bn), lambda i, j: (0, j)),
            pl.BlockSpec((d, bn), lambda i, j: (0, j + nb)),
            pl.BlockSpec((d, bn), lambda i, j: (0, j + 2 * nb)),
        ],
        out_specs=[pl.BlockSpec((bm, bn), lambda i, j: (i, j))] * 2,
        out_shape=[jax.ShapeDtypeStruct((t, d), F32)] * 2,
        compiler_params=_params("parallel", "arbitrary"),
        name="bcv_proj",
    )(h, w_in, w_in, w_in)


def _fill_window(buf_ref, halo_ref, main_ref, halo):
    first = pl.program_id(1) == 0
    buf_ref[0:halo, :] = jnp.where(first, 0.0, halo_ref[0])
    buf_ref[halo:, :] = main_ref[0]


def _conv_ln_kernel(u_ref, halo_ref, w_ref, bdw_ref, g_ref, b_ref, o_ref, buf_ref, acc_ref,
                    *, width, halo, ts, rows):
    _fill_window(buf_ref, halo_ref, u_ref, halo)
    d = buf_ref.shape[1]
    off = halo - (width - 1)

    def chunk(r, _):
        r0 = pl.multiple_of(r * rows, rows)
        for c in range(d // LANES):
            cs = slice(c * LANES, (c + 1) * LANES)
            win = buf_ref[pl.ds(r0, rows + halo), cs]
            acc = jnp.zeros((rows, LANES), F32)
            for s in range(SUBLANES):
                taps = [k for k in range(width) if (off + k) % SUBLANES == s]
                if not taps:
                    continue
                ws = pltpu.roll(win, rows + halo - s, axis=0) if s else win
                for k in taps:
                    q = off + k - s
                    acc = acc + w_ref[k:k + 1, cs] * ws[q:q + rows]
            acc_ref[pl.ds(r0, rows), cs] = acc
        return 0

    lax.fori_loop(0, ts // rows, chunk, 0)
    v = _ln(acc_ref[...] + bdw_ref[...], g_ref[...], b_ref[...])
    o_ref[0] = _silu(v).astype(o_ref.dtype)


def _conv_ln_silu(u, w_dw, b_dw, ln_g, ln_b, ts):
    bsz, s, d = u.shape
    width = w_dw.shape[0]
    halo = -(-(width - 1) // SUBLANES) * SUBLANES
    hb = ts // halo
    kern = functools.partial(_conv_ln_kernel, width=width, halo=halo, ts=ts, rows=32)
    vec = lambda a: a.reshape(1, d)
    return pl.pallas_call(
        kern,
        grid=(bsz, s // ts),
        in_specs=[
            pl.BlockSpec((1, ts, d), lambda b, i: (b, i, 0)),
            pl.BlockSpec((1, halo, d), lambda b, i: (b, jnp.maximum(i * hb - 1, 0), 0)),
            pl.BlockSpec((width, d), lambda b, i: (0, 0)),
            pl.BlockSpec((1, d), lambda b, i: (0, 0)),
            pl.BlockSpec((1, d), lambda b, i: (0, 0)),
            pl.BlockSpec((1, d), lambda b, i: (0, 0)),
        ],
        out_specs=pl.BlockSpec((1, ts, d), lambda b, i: (b, i, 0)),
        out_shape=jax.ShapeDtypeStruct((bsz, s, d), BF16),
        scratch_shapes=[pltpu.VMEM((halo + ts, d), F32), pltpu.VMEM((ts, d), F32)],
        compiler_params=_params("parallel", "arbitrary"),
        name="conv_ln_silu",
    )(u, u, w_dw, vec(b_dw), vec(ln_g), vec(ln_b))


def _gated_conv_kernel(cv_ref, halo_ref, bg_ref, w_ref, o_ref, buf_ref, *, width, halo, ts):
    _fill_window(buf_ref, halo_ref, cv_ref, halo)
    off = halo - (width - 1)
    acc = w_ref[0:1, :] * buf_ref[off:off + ts, :]
    for k in range(1, width):
        acc = acc + w_ref[k:k + 1, :] * buf_ref[off + k:off + k + ts, :]
    o_ref[0] = (bg_ref[0] * acc).astype(o_ref.dtype)


def _gated_short_conv(bg, cv, w_conv, ts):
    bsz, s, d = cv.shape
    width = w_conv.shape[0]
    halo = SUBLANES
    hb = ts // halo
    kern = functools.partial(_gated_conv_kernel, width=width, halo=halo, ts=ts)
    return pl.pallas_call(
        kern,
        grid=(bsz, s // ts),
        in_specs=[
            pl.BlockSpec((1, ts, d), lambda b, i: (b, i, 0)),
            pl.BlockSpec((1, halo, d), lambda b, i: (b, jnp.maximum(i * hb - 1, 0), 0)),
            pl.BlockSpec((1, ts, d), lambda b, i: (b, i, 0)),
            pl.BlockSpec((width, d), lambda b, i: (0, 0)),
        ],
        out_specs=pl.BlockSpec((1, ts, d), lambda b, i: (b, i, 0)),
        out_shape=jax.ShapeDtypeStruct((bsz, s, d), BF16),
        scratch_shapes=[pltpu.VMEM((halo + ts, d), F32)],
        compiler_params=_params("parallel", "arbitrary"),
        name="gated_short_conv",
    )(cv, cv, bg, w_conv)


def _proj_ln_kernel(v_ref, w_ref, bias_ref, h_ref, g_ref, b_ref, o_ref, *, alpha):
    m = _dot(v_ref[...], w_ref[...]) + bias_ref[...]
    o_ref[...] = _ln(alpha * h_ref[...] + m, g_ref[...], b_ref[...])


def _proj_residual_ln(v, w_out, b_out, h, ln_g, ln_b, alpha, bm):
    t, d = h.shape
    vec = lambda a: a.reshape(1, d)
    return pl.pallas_call(
        functools.partial(_proj_ln_kernel, alpha=alpha),
        grid=(t // bm,),
        in_specs=[
            pl.BlockSpec((bm, d), lambda i: (i, 0)),
            pl.BlockSpec((d, d), lambda i: (0, 0)),
            pl.BlockSpec((1, d), lambda i: (0, 0)),
            pl.BlockSpec((bm, d), lambda i: (i, 0)),
            pl.BlockSpec((1, d), lambda i: (0, 0)),
            pl.BlockSpec((1, d), lambda i: (0, 0)),
        ],
        out_specs=pl.BlockSpec((bm, d), lambda i: (i, 0)),
        out_shape=jax.ShapeDtypeStruct((t, d), F32),
        compiler_params=_params("parallel"),
        name="proj_residual_ln",
    )(v, w_out, vec(b_out), h, vec(ln_g), vec(ln_b))


def _pool_kernel(h_ref, halo_ref, w_ref, scale_ref, g_ref, b_ref, o_ref, buf_ref,
                 *, halo, ts, alpha):
    _fill_window(buf_ref, halo_ref, h_ref, halo)
    d = buf_ref.shape[1]
    gd = d // len(POOL_WINDOWS)
    pos = pl.program_id(1) * ts + lax.broadcasted_iota(jnp.int32, (ts, 1), 0) + 1
    ys = []
    for gi, win in enumerate(POOL_WINDOWS):
        cs = slice(gi * gd, (gi + 1) * gd)
        x = buf_ref[halo:halo + ts, cs]
        tot = x
        for j in range(1, win):
            tot = tot + buf_ref[halo - j:halo - j + ts, cs]
        cnt = jnp.minimum(pos, win).astype(F32)
        z = tot / cnt - x
        ys.append(_dot(z.astype(BF16), w_ref[gi]))
    y = jnp.concatenate(ys, axis=-1) * scale_ref[...]
    o_ref[0] = _ln(alpha * h_ref[0] + y, g_ref[...], b_ref[...])


def _pool_mixer_ln(h, w_grp, scale, ln_g, ln_b, alpha, ts):
    bsz, s, d = h.shape
    halo = max(POOL_WINDOWS)
    hb = ts // halo
    vec = lambda a: a.reshape(1, d)
    return pl.pallas_call(
        functools.partial(_pool_kernel, halo=halo, ts=ts, alpha=alpha),
        grid=(bsz, s // ts),
        in_specs=[
            pl.BlockSpec((1, ts, d), lambda b, i: (b, i, 0)),
            pl.BlockSpec((1, halo, d), lambda b, i: (b, jnp.maximum(i * hb - 1, 0), 0)),
            pl.BlockSpec(w_grp.shape, lambda b, i: (0, 0, 0)),
            pl.BlockSpec((1, d), lambda b, i: (0, 0)),
            pl.BlockSpec((1, d), lambda b, i: (0, 0)),
            pl.BlockSpec((1, d), lambda b, i: (0, 0)),
        ],
        out_specs=pl.BlockSpec((1, ts, d), lambda b, i: (b, i, 0)),
        out_shape=jax.ShapeDtypeStruct((bsz, s, d), F32),
        scratch_shapes=[pltpu.VMEM((halo + ts, d), F32)],
        compiler_params=_params("parallel", "arbitrary"),
        name="pool_mixer_ln",
    )(h, h, w_grp, vec(scale), vec(ln_g), vec(ln_b))


def _rank_desc(m):
    n = m.shape[0]
    row = lax.broadcasted_iota(jnp.int32, m.shape, 0)
    rank = jnp.zeros(m.shape, jnp.int32)
    for j in range(n):
        mj = m[j:j + 1, :]
        beats = (mj > m) | ((mj == m) & (row > j))
        rank = rank + beats.astype(jnp.int32)
    return rank


def _router_kernel(h_ref, wt_ref, b_ref, wsgu_ref, wsd_ref, lpos_ref, gate_ref, cnt_ref,
                   shared_ref, *, tm):
    n_e = wt_ref.shape[0]
    epg = n_e // N_EXPERT_GROUPS
    x = h_ref[...].astype(BF16)
    f = wsd_ref.shape[0]
    su = _dot(x, wsgu_ref[...])
    shared_ref[...] = _dot((_silu(su[:, :f]) * su[:, f:]).astype(BF16), wsd_ref[...])
    logits = lax.dot_general(wt_ref[...], x, (((1,), (1,)), ((), ())),
                             preferred_element_type=F32)
    scores = jax.nn.sigmoid(logits)
    biased = scores + b_ref[...]

    sub = lax.broadcasted_iota(jnp.int32, (epg, tm), 0)
    gscore = []
    for g in range(N_EXPERT_GROUPS):
        blk = biased[g * epg:(g + 1) * epg, :]
        m1 = jnp.max(blk, axis=0, keepdims=True)
        first = jnp.min(jnp.where(blk == m1, sub, epg), axis=0, keepdims=True)
        m2 = jnp.max(jnp.where(sub == first, -jnp.inf, blk), axis=0, keepdims=True)
        gscore.append(m1 + m2)
    gsel = _rank_desc(jnp.concatenate(gscore, axis=0)) < TOPK_GROUPS
    emask = jnp.concatenate(
        [jnp.broadcast_to(gsel[g:g + 1, :], (epg, tm)) for g in range(N_EXPERT_GROUPS)], axis=0)
    masked = jnp.where(emask, biased, -jnp.inf)
    sel = _rank_desc(masked) < TOP_K
    self32 = sel.astype(F32)
    selb = self32.astype(BF16)

    gate = jnp.where(sel, scores, 0.0)
    gate = gate / jnp.sum(gate, axis=0, keepdims=True) * ROUTED_SCALE

    er = lax.broadcasted_iota(jnp.int32, (n_e, n_e), 0)
    ec = lax.broadcasted_iota(jnp.int32, (n_e, n_e), 1)
    lower = (ec < er).astype(BF16)
    slot = _dot(lower, selb)
    tr = lax.broadcasted_iota(jnp.int32, (tm, tm), 0)
    tc = lax.broadcasted_iota(jnp.int32, (tm, tm), 1)
    before = (tr < tc).astype(BF16)
    lrank = _dot(selb, before)
    count = jnp.sum(self32, axis=1, keepdims=True)
    count_l = jnp.broadcast_to(count, (n_e, LANES))
    lstart = _dot(lower, count_l.astype(BF16))[:, 0:1]
    lpos = lstart + lrank

    lpos_rows, gate_rows = [], []
    for k in range(TOP_K):
        pick = sel & (slot == float(k))
        lpos_rows.append(jnp.sum(jnp.where(pick, lpos, 0.0), axis=0, keepdims=True))
        gate_rows.append(jnp.sum(jnp.where(pick, gate, 0.0), axis=0, keepdims=True))
    lpos_ref[...] = jnp.concatenate(lpos_rows, axis=0).astype(jnp.int32)
    gate_ref[...] = jnp.concatenate(gate_rows, axis=0)
    cnt_ref[0] = count_l.astype(jnp.int32)


def _router(h1, w_router_t, b_router, ws_gu, ws_down, tm):
    t, d = h1.shape
    n_e = w_router_t.shape[0]
    f = ws_down.shape[0]
    assert tm <= 256, "tile counts must stay exactly representable in bf16"
    lpos, gate, cnt, shared = pl.pallas_call(
        functools.partial(_router_kernel, tm=tm),
        grid=(t // tm,),
        in_specs=[
            pl.BlockSpec((tm, d), lambda i: (i, 0)),
            pl.BlockSpec((n_e, d), lambda i: (0, 0)),
            pl.BlockSpec((n_e, 1), lambda i: (0, 0)),
            pl.BlockSpec((d, 2 * f), lambda i: (0, 0)),
            pl.BlockSpec((f, d), lambda i: (0, 0)),
        ],
        out_specs=[
            pl.BlockSpec((TOP_K, tm), lambda i: (0, i)),
            pl.BlockSpec((TOP_K, tm), lambda i: (0, i)),
            pl.BlockSpec((1, n_e, LANES), lambda i: (i, 0, 0)),
            pl.BlockSpec((tm, d), lambda i: (i, 0)),
        ],
        out_shape=[
            jax.ShapeDtypeStruct((TOP_K, t), jnp.int32),
            jax.ShapeDtypeStruct((TOP_K, t), F32),
            jax.ShapeDtypeStruct((t // tm, n_e, LANES), jnp.int32),
            jax.ShapeDtypeStruct((t, d), F32),
        ],
        compiler_params=_params("parallel"),
        name="router",
    )(h1, w_router_t, b_router.reshape(n_e, 1), ws_gu, ws_down)
    return lpos, gate, cnt[:, :, 0], shared


HIGH_HALF = -65536


def _pack_words(lo, hi):
    bits = lambda v: lax.bitcast_convert_type(v.astype(PACKED).astype(F32), jnp.int32)
    return (bits(hi) & HIGH_HALF) | lax.shift_right_logical(bits(lo), 16)


def _word_lo(w):
    return lax.bitcast_convert_type(lax.shift_left(w, 16), F32)


def _word_hi(w):
    return lax.bitcast_convert_type(w & HIGH_HALF, F32)


def _pack_rows(x, dst_ref, r0=0):
    n, half = x.shape[0], x.shape[1] // 2
    rc = half // LANES
    words = _pack_words(x[:, :half], x[:, half:])
    for c in range(rc):
        dst_ref[pl.ds(r0 * rc + c, n, stride=rc), :] = words[:, c * LANES:(c + 1) * LANES]


def _unpack_rows(src_ref, n, rc, r0=0):
    lo, hi = [], []
    for c in range(rc):
        w = src_ref[pl.ds(r0 * rc + c, n, stride=rc), :]
        lo.append(_word_lo(w))
        hi.append(_word_hi(w))
    return jnp.concatenate(lo + hi, axis=-1)


def _pack_planes(x, dst_ref, r0):
    n, half = x.shape[0], x.shape[1] // 2
    words = _pack_words(x[:, :half], x[:, half:])
    for c in range(half // LANES):
        dst_ref[c, pl.ds(r0, n), :] = words[:, c * LANES:(c + 1) * LANES]


def _unpack_planes(src_ref, n, r0):
    lo, hi = [], []
    for c in range(src_ref.shape[0]):
        w = src_ref[c, pl.ds(r0, n), :]
        lo.append(_word_lo(w))
        hi.append(_word_hi(w))
    return jnp.concatenate(lo + hi, axis=-1)


def _segment_copies(local_ref, hbm_ref, sem, tables, tile, n_e, rc, tm, to_hbm):
    cnt_ref, lstart_ref, gstart_ref = tables

    def per_expert(e, _):
        j = tile * n_e + e
        n, ls, gs = cnt_ref[j], lstart_ref[j], gstart_ref[j]

        def chunk(size):
            @pl.when((n & size) != 0)
            def _():
                done = (n & ~(2 * size - 1)) * rc
                lo = pl.multiple_of(ls + done, rc)
                go = pl.multiple_of(gs + done, rc)
                loc = local_ref.at[pl.ds(lo, size * rc), :]
                glob = hbm_ref.at[pl.ds(go, size * rc), :]
                if to_hbm:
                    pltpu.make_async_copy(loc, glob, sem).start()
                else:
                    pltpu.make_async_copy(glob, loc, sem).start()

        size = tm
        while size >= 1:
            chunk(size)
            size //= 2
        return 0

    lax.fori_loop(0, n_e, per_expert, 0)


def _wait_segments(local_ref, hbm_ref, sem):
    rows = local_ref.shape[0]
    pltpu.make_async_copy(local_ref, hbm_ref.at[pl.ds(0, rows), :], sem).wait()


def _dispatch_kernel(cnt_ref, lstart_ref, gstart_ref, lpos_ref, h_ref, xs_ref,
                     q_ref, s_ref, sem, *, tm, n_e, rc):
    i = pl.program_id(0)
    n = pl.num_programs(0)
    slot = i % 2

    @pl.when(i >= 2)
    def _():
        _wait_segments(s_ref.at[slot], xs_ref, sem.at[slot])

    _pack_rows(h_ref[...], q_ref)

    def place_into(sl):
        def place(j, _):
            for u in range(ROW_UNROLL):
                t = j * ROW_UNROLL + u
                row = q_ref[pl.ds(pl.multiple_of(t * rc, rc), rc), :]
                for k in range(TOP_K):
                    dst = pl.multiple_of(lpos_ref[t * TOP_K + k], rc)
                    s_ref[sl, pl.ds(dst, rc), :] = row
            return 0

        lax.fori_loop(0, tm // ROW_UNROLL, place, 0)

    for sl in range(2):
        pl.when(slot == sl)(functools.partial(place_into, sl))
    _segment_copies(s_ref.at[slot], xs_ref, sem.at[slot], (cnt_ref, lstart_ref, gstart_ref),
                    i, n_e, rc, tm, to_hbm=True)

    @pl.when(i == n - 1)
    def _():
        _wait_segments(s_ref.at[slot], xs_ref, sem.at[slot])

        @pl.when(n > 1)
        def _():
            _wait_segments(s_ref.at[1 - slot], xs_ref, sem.at[1 - slot])


def _dispatch(h1, lpos, tables, n_rows, tm, n_e):
    t, d = h1.shape
    rc = d // 2 // LANES
    grid_spec = pltpu.PrefetchScalarGridSpec(
        num_scalar_prefetch=3,
        grid=(t // tm,),
        in_specs=[
            pl.BlockSpec((TOP_K * tm,), lambda i, *_: (i,), memory_space=pltpu.SMEM),
            pl.BlockSpec((tm, d), lambda i, *_: (i, 0)),
        ],
        out_specs=pl.BlockSpec(memory_space=pl.ANY),
        scratch_shapes=[
            pltpu.VMEM((tm * rc, LANES), jnp.int32),
            pltpu.VMEM((2, TOP_K * tm * rc, LANES), jnp.int32),
            pltpu.SemaphoreType.DMA((2,)),
        ],
    )
    return pl.pallas_call(
        functools.partial(_dispatch_kernel, tm=tm, n_e=n_e, rc=rc),
        grid_spec=grid_spec,
        out_shape=jax.ShapeDtypeStruct((n_rows * rc, LANES), jnp.int32),
        compiler_params=_params("arbitrary"),
        name="dispatch",
    )(*tables, lpos, h1)


def _expert_kernel(be_ref, nv_ref, nxt_ref, live_ref, xs_ref, wg_ref, wu_ref, wd_ref, ys_ref,
                   xbuf, ybuf, wg_f, wu_f, wd_f, wgu_s, wd_s, sem_x, sem_y, sem_w,
                   *, layer, blk, rc):
    n_live = live_ref[0]
    f = wd_s.shape[0]

    def x_copies(g, slot):
        r0 = pl.multiple_of(g * blk, blk)
        return [pltpu.make_async_copy(xs_ref.at[pl.ds(r0, blk), c, :], xbuf.at[slot, c],
                                      sem_x.at[slot]) for c in range(rc)]

    def y_copies(g, slot):
        r0 = pl.multiple_of(g * blk, blk)
        return [pltpu.make_async_copy(ybuf.at[slot, c], ys_ref.at[pl.ds(r0, blk), c, :],
                                      sem_y.at[slot]) for c in range(rc)]

    def w_copies(e, slot):
        return (pltpu.make_async_copy(wg_ref.at[layer, e], wg_f.at[slot], sem_w.at[slot]),
                pltpu.make_async_copy(wu_ref.at[layer, e], wu_f.at[slot], sem_w.at[slot]),
                pltpu.make_async_copy(wd_ref.at[layer, e], wd_f.at[slot], sem_w.at[slot]))

    def start(copies):
        for cp in copies:
            cp.start()

    def wait(copies):
        for cp in copies:
            cp.wait()

    start(x_copies(0, 0))
    start(w_copies(be_ref[0], 0))

    def block(g, wslot):
        slot = g % 2
        e = be_ref[g]
        fresh = (g == 0) | (e != be_ref[jnp.maximum(g - 1, 0)])

        @pl.when(fresh)
        def _():
            wait(w_copies(e, wslot))
            nxt = nxt_ref[g]

            @pl.when(nxt >= 0)
            def _():
                start(w_copies(nxt, 1 - wslot))

            wgu_s[:, :f] = wg_f[wslot].astype(BF16)
            wgu_s[:, f:] = wu_f[wslot].astype(BF16)
            wd_s[...] = wd_f[wslot].astype(BF16)

        wait(x_copies(g, slot))

        @pl.when(g + 1 < n_live)
        def _():
            start(x_copies(g + 1, 1 - slot))

        @pl.when(g >= 2)
        def _():
            wait(y_copies(g - 2, slot))

        nv = nv_ref[g]
        xin, yout = xbuf.at[slot], ybuf.at[slot]
        sub = blk // EXPERT_SUBBLOCKS
        for sb in range(EXPERT_SUBBLOCKS):
            r0 = sb * sub
            x = _unpack_planes(xin, sub, r0).astype(BF16)
            gu = _dot(x, wgu_s[...])
            mid = _silu(gu[:, :f]) * gu[:, f:]
            row = r0 + lax.broadcasted_iota(jnp.int32, (sub, 1), 0)
            mid = jnp.where(row < nv, mid, 0.0).astype(BF16)
            _pack_planes(_dot(mid, wd_s[...]), yout, r0)
        start(y_copies(g, slot))
        return jnp.where(fresh, 1 - wslot, wslot)

    lax.fori_loop(0, n_live, block, 0)
    last = n_live - 1
    wait(y_copies(last, last % 2))

    @pl.when(n_live >= 2)
    def _():
        wait(y_copies(last - 1, (last - 1) % 2))


def _experts(xs, blk_e, blk_nv, blk_nxt, n_live, w_gate, w_up, w_down, layer, d):
    rc = d // 2 // LANES
    f = w_gate.shape[-1]
    n_rows = xs.shape[0] // rc
    anyspace = pl.BlockSpec(memory_space=pl.ANY)
    grid_spec = pltpu.PrefetchScalarGridSpec(
        num_scalar_prefetch=4,
        grid=(1,),
        in_specs=[anyspace] * 4,
        out_specs=anyspace,
        scratch_shapes=[
            pltpu.VMEM((2, rc, EXPERT_BLOCK, LANES), jnp.int32),
            pltpu.VMEM((2, rc, EXPERT_BLOCK, LANES), jnp.int32),
            pltpu.VMEM((2, d, f), F32),
            pltpu.VMEM((2, d, f), F32),
            pltpu.VMEM((2, f, d), F32),
            pltpu.VMEM((d, 2 * f), BF16),
            pltpu.VMEM((f, d), BF16),
            pltpu.SemaphoreType.DMA((2,)),
            pltpu.SemaphoreType.DMA((2,)),
            pltpu.SemaphoreType.DMA((2,)),
        ],
    )
    ys = pl.pallas_call(
        functools.partial(_expert_kernel, layer=layer, blk=EXPERT_BLOCK, rc=rc),
        grid_spec=grid_spec,
        out_shape=jax.ShapeDtypeStruct((n_rows, rc, LANES), jnp.int32),
        compiler_params=_params("arbitrary"),
        name="experts",
    )(blk_e, blk_nv, blk_nxt, n_live, xs.reshape(n_rows, rc, LANES), w_gate, w_up, w_down)
    return ys.reshape(xs.shape)


def _combine_kernel(cnt_ref, lstart_ref, gstart_ref, lpos_ref, gate_ref, h_ref, shared_ref,
                    g_ref, b_ref, ys_ref, o_ref, l_ref, r_ref, sem, *, tm, n_e, rc, alpha):
    i = pl.program_id(0)
    n = pl.num_programs(0)
    slot = i % 2
    tables = (cnt_ref, lstart_ref, gstart_ref)

    @pl.when(i == 0)
    def _():
        _segment_copies(l_ref.at[0], ys_ref, sem.at[0], tables, 0, n_e, rc, tm, to_hbm=False)

    @pl.when(i + 1 < n)
    def _():
        _segment_copies(l_ref.at[1 - slot], ys_ref, sem.at[1 - slot], tables, i + 1, n_e, rc, tm,
                        to_hbm=False)

    _wait_segments(l_ref.at[slot], ys_ref, sem.at[slot])

    def gather_from(sl):
        def gather(j, _):
            for u in range(ROW_UNROLL):
                t = j * ROW_UNROLL + u
                lo = hi = None
                for k in range(TOP_K):
                    src = pl.multiple_of(lpos_ref[t * TOP_K + k], rc)
                    w = l_ref[sl, pl.ds(src, rc), :]
                    gk = gate_ref[t * TOP_K + k]
                    lo = gk * _word_lo(w) if lo is None else lo + gk * _word_lo(w)
                    hi = gk * _word_hi(w) if hi is None else hi + gk * _word_hi(w)
                base = pl.multiple_of(t * 2 * rc, 2 * rc)
                r_ref[pl.ds(base, rc), :] = lo
                r_ref[pl.ds(base + rc, rc), :] = hi
            return 0

        lax.fori_loop(0, tm // ROW_UNROLL, gather, 0)

    for sl in range(2):
        pl.when(slot == sl)(functools.partial(gather_from, sl))
    routed = jnp.concatenate(
        [r_ref[pl.ds(c, tm, stride=2 * rc), :] for c in range(2 * rc)], axis=-1)
    o_ref[...] = _ln(alpha * h_ref[...] + (routed + shared_ref[...]), g_ref[...], b_ref[...])


def _combine_ln(ys, lpos, gate, tables, h1, shared, ln_g, ln_b, alpha, tm, n_e):
    t, d = h1.shape
    rc = d // 2 // LANES
    vec = lambda a: a.reshape(1, d)
    grid_spec = pltpu.PrefetchScalarGridSpec(
        num_scalar_prefetch=3,
        grid=(t // tm,),
        in_specs=[
            pl.BlockSpec((TOP_K * tm,), lambda i, *_: (i,), memory_space=pltpu.SMEM),
            pl.BlockSpec((TOP_K * tm,), lambda i, *_: (i,), memory_space=pltpu.SMEM),
            pl.BlockSpec((tm, d), lambda i, *_: (i, 0)),
            pl.BlockSpec((tm, d), lambda i, *_: (i, 0)),
            pl.BlockSpec((1, d), lambda i, *_: (0, 0)),
            pl.BlockSpec((1, d), lambda i, *_: (0, 0)),
            pl.BlockSpec(memory_space=pl.ANY),
        ],
        out_specs=pl.BlockSpec((tm, d), lambda i, *_: (i, 0)),
        scratch_shapes=[
            pltpu.VMEM((2, TOP_K * tm * rc, LANES), jnp.int32),
            pltpu.VMEM((tm * 2 * rc, LANES), F32),
            pltpu.SemaphoreType.DMA((2,)),
        ],
    )
    return pl.pallas_call(
        functools.partial(_combine_kernel, tm=tm, n_e=n_e, rc=rc, alpha=alpha),
        grid_spec=grid_spec,
        out_shape=jax.ShapeDtypeStruct((t, d), F32),
        compiler_params=_params("arbitrary"),
        name="combine_ln",
    )(*tables, lpos, gate, h1, shared, vec(ln_g), vec(ln_b), ys)


def _ple_kernel(h_ref, p_ref, wp_ref, wg_ref, g_ref, b_ref, o_ref):
    h = h_ref[...]
    e = _dot(p_ref[0].astype(BF16), wp_ref[...])
    gate = jax.nn.sigmoid(_dot(h.astype(BF16), wg_ref[...]))
    o_ref[...] = h + _ln(gate * e, g_ref[...], b_ref[...])


def _ple(h2, p, layer, w_proj, w_gate, ln_g, ln_b, bm):
    t, d = h2.shape
    pd = p.shape[-1]
    vec = lambda a: a.reshape(1, d)
    return pl.pallas_call(
        _ple_kernel,
        grid=(t // bm,),
        in_specs=[
            pl.BlockSpec((bm, d), lambda i: (i, 0)),
            pl.BlockSpec((1, bm, pd), lambda i: (layer, i, 0)),
            pl.BlockSpec((pd, d), lambda i: (0, 0)),
            pl.BlockSpec((d, d), lambda i: (0, 0)),
            pl.BlockSpec((1, d), lambda i: (0, 0)),
            pl.BlockSpec((1, d), lambda i: (0, 0)),
        ],
        out_specs=pl.BlockSpec((bm, d), lambda i: (i, 0)),
        out_shape=jax.ShapeDtypeStruct((t, d), F32),
        compiler_params=_params("parallel"),
        name="ple",
    )(h2, p, w_proj, w_gate, vec(ln_g), vec(ln_b))


def _moe_tables(cnt, n_blocks, rc):
    n_e = cnt.shape[1]
    total = jnp.sum(cnt, axis=0)
    padded = (total + EXPERT_BLOCK - 1) // EXPERT_BLOCK * EXPERT_BLOCK
    ends = jnp.cumsum(padded)
    starts = ends - padded
    gstart = starts[None, :] + jnp.cumsum(cnt, axis=0) - cnt
    lstart = jnp.cumsum(cnt, axis=1) - cnt
    blk_start = jnp.arange(n_blocks, dtype=jnp.int32) * EXPERT_BLOCK
    blk_e = jnp.minimum(jnp.sum(ends[None, :] <= blk_start[:, None], axis=1), n_e - 1)
    onehot = blk_e[:, None] == jnp.arange(n_e)[None, :]
    used = jnp.sum(jnp.where(onehot, (starts + total)[None, :], 0), axis=1)
    blk_nv = jnp.clip(used - blk_start, 0, EXPERT_BLOCK)
    ids = jnp.arange(n_e)
    later = (ids[None, :] > ids[:, None]) & (total[None, :] > 0)
    nxt_e = jnp.min(jnp.where(later, ids[None, :], n_e), axis=1)
    nxt_e = jnp.where(nxt_e == n_e, -1, nxt_e)
    blk_nxt = jnp.sum(jnp.where(onehot, nxt_e[None, :], 0), axis=1)
    n_live = ends[-1:] // EXPERT_BLOCK
    i32 = lambda a: a.astype(jnp.int32)
    tables = (i32(cnt).reshape(-1), i32(lstart * rc).reshape(-1), i32(gstart * rc).reshape(-1))
    return tables, i32(blk_e), i32(blk_nv), i32(blk_nxt), i32(n_live)


def _moe_ln(h1, layer, w_router, b_router, w_gate, w_up, w_down, ws_gate, ws_up, ws_down,
            ln_g, ln_b, alpha, tm):
    t, d = h1.shape
    n_e = w_router.shape[1]
    n_blocks = t * TOP_K // EXPERT_BLOCK + n_e
    rc = d // 2 // LANES
    ws_gu = jnp.concatenate([ws_gate, ws_up], axis=1).astype(BF16)
    lpos, gate, cnt, shared = _router(h1, w_router.T.astype(BF16), b_router, ws_gu,
                                      ws_down.astype(BF16), tm)
    tables, blk_e, blk_nv, blk_nxt, n_live = _moe_tables(cnt, n_blocks, rc)
    lpos = (lpos * rc).T.reshape(-1)
    gate = gate.T.reshape(-1)
    xs = _dispatch(h1, lpos, tables, n_blocks * EXPERT_BLOCK, tm, n_e)
    ys = _experts(xs, blk_e, blk_nv, blk_nxt, n_live, w_gate, w_up, w_down, layer, d)
    return _combine_ln(ys, lpos, gate, tables, h1, shared, ln_g, ln_b, alpha, tm, n_e)


def _tiles(t, s):
    return {"mm_m": min(1024, t), "mm_n": 512, "glu_n": 1024, "proj_m": min(512, t),
            "seq": min(512, s),
            "moe": min(256, t)}


def kernel(x, p, a_w_in, a_b_in, a_w_dw, a_b_dw, a_ln_g, a_ln_b, a_w_out, a_b_out, b_w_in, b_w_conv, b_w_out, c_w_grp, c_scale, ln1_g, ln1_b, ln2_g, ln2_b, router_w, router_b, exp_w_gate, exp_w_up, exp_w_down, sh_w_gate, sh_w_up, sh_w_down, ple_w_proj, ple_w_gate, ple_ln_g, ple_ln_b):
    bsz, s, d = x.shape
    depth = ln1_g.shape[0]
    t = bsz * s
    assert (t * TOP_K) % EXPERT_BLOCK == 0 and d % (2 * LANES) == 0
    alpha = (2 * depth) ** 0.25
    tl = _tiles(t, s)
    bn = min(tl["mm_n"], d)
    h = x.reshape(t, d)
    p_rows = p.reshape(depth, t, p.shape[-1])
    for i in range(depth):
        kind, j = i % 3, i // 3
        if kind == 0:
            u = _glu_proj(h, a_w_in[j].astype(BF16), a_b_in[j], tl["mm_m"], min(tl["glu_n"], d))
            v = _conv_ln_silu(u.reshape(bsz, s, d), a_w_dw[j], a_b_dw[j], a_ln_g[j], a_ln_b[j],
                              tl["seq"])
            h1 = _proj_residual_ln(v.reshape(t, d), a_w_out[j].astype(BF16), a_b_out[j], h,
                                   ln1_g[i], ln1_b[i], alpha, tl["proj_m"])
        elif kind == 1:
            bg, cv = _bcv_proj(h, b_w_in[j].astype(BF16), tl["mm_m"], bn)
            v = _gated_short_conv(bg.reshape(bsz, s, d), cv.reshape(bsz, s, d), b_w_conv[j],
                                  tl["seq"])
            h1 = _proj_residual_ln(v.reshape(t, d), b_w_out[j].astype(BF16),
                                   jnp.zeros((d,), F32), h, ln1_g[i], ln1_b[i], alpha,
                                   tl["proj_m"])
        else:
            h1 = _pool_mixer_ln(h.reshape(bsz, s, d), c_w_grp[j].astype(BF16), c_scale[j],
                                ln1_g[i], ln1_b[i], alpha, tl["seq"]).reshape(t, d)
        h2 = _moe_ln(h1, i, router_w[i], router_b[i], exp_w_gate, exp_w_up, exp_w_down,
                     sh_w_gate[i], sh_w_up[i], sh_w_down[i], ln2_g[i], ln2_b[i], alpha,
                     tl["moe"])
        h = _ple(h2, p_rows, i, ple_w_proj[i].astype(BF16), ple_w_gate[i].astype(BF16),
                 ple_ln_g[i], ple_ln_b[i], tl["proj_m"])
    return h.reshape(bsz, s, d)
```

```python
import functools

import jax
import jax.numpy as jnp
from jax import lax
from jax.experimental import pallas as pl
from jax.experimental.pallas import tpu as pltpu

LN_EPS = 1e-5
TOP_K = 8
N_EXPERT_GROUPS = 8
TOPK_GROUPS = 4
ROUTED_SCALE = 2.5
POOL_WINDOWS = (2, 4, 8, 16)
EXPERT_BLOCK = 512
EXPERT_SUBBLOCKS = 4
PLACE_UNROLL = 4
GATHER_UNROLL = 8
LIST_ROWS = 16
LANES = 128
SUBLANES = 8
VMEM_LIMIT = 56 * 1024 * 1024

BF16 = jnp.bfloat16
PACKED = jnp.bfloat16
F32 = jnp.float32


def _params(*sem):
    return pltpu.CompilerParams(dimension_semantics=sem, vmem_limit_bytes=VMEM_LIMIT)


def _ln(x, g, b):
    mu = jnp.mean(x, axis=-1, keepdims=True)
    xc = x - mu
    var = jnp.mean(xc * xc, axis=-1, keepdims=True)
    return xc * lax.rsqrt(var + LN_EPS) * g + b


def _dot(a, b):
    return jnp.dot(a, b, preferred_element_type=F32)


def _silu(x):
    return x * jax.nn.sigmoid(x)


def _glu_kernel(x_ref, wa_ref, wg_ref, ba_ref, bg_ref, o_ref):
    x = x_ref[...].astype(BF16)
    a = _dot(x, wa_ref[...]) + ba_ref[...]
    g = _dot(x, wg_ref[...]) + bg_ref[...]
    o_ref[...] = a * jax.nn.sigmoid(g)


def _glu_proj(h, w_in, b_in, bm, bn):
    t, d = h.shape
    nb = d // bn
    b2 = b_in.reshape(1, 2 * d)
    return pl.pallas_call(
        _glu_kernel,
        grid=(t // bm, nb),
        in_specs=[
            pl.BlockSpec((bm, d), lambda i, j: (i, 0)),
            pl.BlockSpec((d, bn), lambda i, j: (0, j)),
            pl.BlockSpec((d, bn), lambda i, j: (0, j + nb)),
            pl.BlockSpec((1, bn), lambda i, j: (0, j)),
            pl.BlockSpec((1, bn), lambda i, j: (0, j + nb)),
        ],
        out_specs=pl.BlockSpec((bm, bn), lambda i, j: (i, j)),
        out_shape=jax.ShapeDtypeStruct((t, d), F32),
        compiler_params=_params("parallel", "arbitrary"),
        name="glu_proj",
    )(h, w_in, w_in, b2, b2)


def _bcv_kernel(x_ref, wb_ref, wc_ref, wv_ref, b_ref, cv_ref):
    x = x_ref[...].astype(BF16)
    b_ref[...] = _dot(x, wb_ref[...])
    cv_ref[...] = _dot(x, wc_ref[...]) * _dot(x, wv_ref[...])


def _bcv_proj(h, w_in, bm, bn):
    t, d = h.shape
    nb = d // bn
    return pl.pallas_call(
        _bcv_kernel,
        grid=(t // bm, nb),
        in_specs=[
            pl.BlockSpec((bm, d), lambda i, j: (i, 0)),
            pl.BlockSpec((d, bn), lambda i, j: (0, j)),
            pl.BlockSpec((d, bn), lambda i, j: (0, j + nb)),
            pl.BlockSpec((d, bn), lambda i, j: (0, j + 2 * nb)),
        ],
        out_specs=[pl.BlockSpec((bm, bn), lambda i, j: (i, j))] * 2,
        out_shape=[jax.ShapeDtypeStruct((t, d), F32)] * 2,
        compiler_params=_params("parallel", "arbitrary"),
        name="bcv_proj",
    )(h, w_in, w_in, w_in)


def _fill_window(buf_ref, halo_ref, main_ref, halo):
    first = pl.program_id(1) == 0
    buf_ref[0:halo, :] = jnp.where(first, 0.0, halo_ref[0])
    buf_ref[halo:, :] = main_ref[0]


def _conv_ln_kernel(u_ref, halo_ref, w_ref, bdw_ref, g_ref, b_ref, o_ref, buf_ref, acc_ref,
                    *, width, halo, ts, rows):
    _fill_window(buf_ref, halo_ref, u_ref, halo)
    d = buf_ref.shape[1]
    off = halo - (width - 1)

    def chunk(r, _):
        r0 = pl.multiple_of(r * rows, rows)
        for c in range(d // LANES):
            cs = slice(c * LANES, (c + 1) * LANES)
            win = buf_ref[pl.ds(r0, rows + halo), cs]
            acc = jnp.zeros((rows, LANES), F32)
            for s in range(SUBLANES):
                taps = [k for k in range(width) if (off + k) % SUBLANES == s]
                if not taps:
                    continue
                ws = pltpu.roll(win, rows + halo - s, axis=0) if s else win
                for k in taps:
                    q = off + k - s
                    acc = acc + w_ref[k:k + 1, cs] * ws[q:q + rows]
            acc_ref[pl.ds(r0, rows), cs] = acc
        return 0

    lax.fori_loop(0, ts // rows, chunk, 0)
    v = _ln(acc_ref[...] + bdw_ref[...], g_ref[...], b_ref[...])
    o_ref[0] = _silu(v).astype(o_ref.dtype)


def _conv_ln_silu(u, w_dw, b_dw, ln_g, ln_b, ts):
    bsz, s, d = u.shape
    width = w_dw.shape[0]
    halo = -(-(width - 1) // SUBLANES) * SUBLANES
    hb = ts // halo
    kern = functools.partial(_conv_ln_kernel, width=width, halo=halo, ts=ts, rows=32)
    vec = lambda a: a.reshape(1, d)
    return pl.pallas_call(
        kern,
        grid=(bsz, s // ts),
        in_specs=[
            pl.BlockSpec((1, ts, d), lambda b, i: (b, i, 0)),
            pl.BlockSpec((1, halo, d), lambda b, i: (b, jnp.maximum(i * hb - 1, 0), 0)),
            pl.BlockSpec((width, d), lambda b, i: (0, 0)),
            pl.BlockSpec((1, d), lambda b, i: (0, 0)),
            pl.BlockSpec((1, d), lambda b, i: (0, 0)),
            pl.BlockSpec((1, d), lambda b, i: (0, 0)),
        ],
        out_specs=pl.BlockSpec((1, ts, d), lambda b, i: (b, i, 0)),
        out_shape=jax.ShapeDtypeStruct((bsz, s, d), BF16),
        scratch_shapes=[pltpu.VMEM((halo + ts, d), F32), pltpu.VMEM((ts, d), F32)],
        compiler_params=_params("parallel", "arbitrary"),
        name="conv_ln_silu",
    )(u, u, w_dw, vec(b_dw), vec(ln_g), vec(ln_b))


def _gated_conv_kernel(cv_ref, halo_ref, bg_ref, w_ref, o_ref, buf_ref, *, width, halo, ts):
    _fill_window(buf_ref, halo_ref, cv_ref, halo)
    off = halo - (width - 1)
    acc = w_ref[0:1, :] * buf_ref[off:off + ts, :]
    for k in range(1, width):
        acc = acc + w_ref[k:k + 1, :] * buf_ref[off + k:off + k + ts, :]
    o_ref[0] = (bg_ref[0] * acc).astype(o_ref.dtype)


def _gated_short_conv(bg, cv, w_conv, ts):
    bsz, s, d = cv.shape
    width = w_conv.shape[0]
    halo = SUBLANES
    hb = ts // halo
    kern = functools.partial(_gated_conv_kernel, width=width, halo=halo, ts=ts)
    return pl.pallas_call(
        kern,
        grid=(bsz, s // ts),
        in_specs=[
            pl.BlockSpec((1, ts, d), lambda b, i: (b, i, 0)),
            pl.BlockSpec((1, halo, d), lambda b, i: (b, jnp.maximum(i * hb - 1, 0), 0)),
            pl.BlockSpec((1, ts, d), lambda b, i: (b, i, 0)),
            pl.BlockSpec((width, d), lambda b, i: (0, 0)),
        ],
        out_specs=pl.BlockSpec((1, ts, d), lambda b, i: (b, i, 0)),
        out_shape=jax.ShapeDtypeStruct((bsz, s, d), BF16),
        scratch_shapes=[pltpu.VMEM((halo + ts, d), F32)],
        compiler_params=_params("parallel", "arbitrary"),
        name="gated_short_conv",
    )(cv, cv, bg, w_conv)


def _proj_ln_kernel(v_ref, w_ref, bias_ref, h_ref, g_ref, b_ref, o_ref, *, alpha):
    m = _dot(v_ref[...], w_ref[...]) + bias_ref[...]
    o_ref[...] = _ln(alpha * h_ref[...] + m, g_ref[...], b_ref[...])


def _proj_residual_ln(v, w_out, b_out, h, ln_g, ln_b, alpha, bm):
    t, d = h.shape
    vec = lambda a: a.reshape(1, d)
    return pl.pallas_call(
        functools.partial(_proj_ln_kernel, alpha=alpha),
        grid=(t // bm,),
        in_specs=[
            pl.BlockSpec((bm, d), lambda i: (i, 0)),
            pl.BlockSpec((d, d), lambda i: (0, 0)),
            pl.BlockSpec((1, d), lambda i: (0, 0)),
            pl.BlockSpec((bm, d), lambda i: (i, 0)),
            pl.BlockSpec((1, d), lambda i: (0, 0)),
            pl.BlockSpec((1, d), lambda i: (0, 0)),
        ],
        out_specs=pl.BlockSpec((bm, d), lambda i: (i, 0)),
        out_shape=jax.ShapeDtypeStruct((t, d), F32),
        compiler_params=_params("parallel"),
        name="proj_residual_ln",
    )(v, w_out, vec(b_out), h, vec(ln_g), vec(ln_b))


def _pool_kernel(h_ref, halo_ref, w_ref, scale_ref, g_ref, b_ref, o_ref, buf_ref,
                 *, halo, ts, alpha):
    _fill_window(buf_ref, halo_ref, h_ref, halo)
    d = buf_ref.shape[1]
    gd = d // len(POOL_WINDOWS)
    pos = pl.program_id(1) * ts + lax.broadcasted_iota(jnp.int32, (ts, 1), 0) + 1
    ys = []
    for gi, win in enumerate(POOL_WINDOWS):
        cs = slice(gi * gd, (gi + 1) * gd)
        x = buf_ref[halo:halo + ts, cs]
        tot = x
        for j in range(1, win):
            tot = tot + buf_ref[halo - j:halo - j + ts, cs]
        cnt = jnp.minimum(pos, win).astype(F32)
        z = tot / cnt - x
        ys.append(_dot(z.astype(BF16), w_ref[gi]))
    y = jnp.concatenate(ys, axis=-1) * scale_ref[...]
    o_ref[0] = _ln(alpha * h_ref[0] + y, g_ref[...], b_ref[...])


def _pool_mixer_ln(h, w_grp, scale, ln_g, ln_b, alpha, ts):
    bsz, s, d = h.shape
    halo = max(POOL_WINDOWS)
    hb = ts // halo
    vec = lambda a: a.reshape(1, d)
    return pl.pallas_call(
        functools.partial(_pool_kernel, halo=halo, ts=ts, alpha=alpha),
        grid=(bsz, s // ts),
        in_specs=[
            pl.BlockSpec((1, ts, d), lambda b, i: (b, i, 0)),
            pl.BlockSpec((1, halo, d), lambda b, i: (b, jnp.maximum(i * hb - 1, 0), 0)),
            pl.BlockSpec(w_grp.shape, lambda b, i: (0, 0, 0)),
            pl.BlockSpec((1, d), lambda b, i: (0, 0)),
            pl.BlockSpec((1, d), lambda b, i: (0, 0)),
            pl.BlockSpec((1, d), lambda b, i: (0, 0)),
        ],
        out_specs=pl.BlockSpec((1, ts, d), lambda b, i: (b, i, 0)),
        out_shape=jax.ShapeDtypeStruct((bsz, s, d), F32),
        scratch_shapes=[pltpu.VMEM((halo + ts, d), F32)],
        compiler_params=_params("parallel", "arbitrary"),
        name="pool_mixer_ln",
    )(h, h, w_grp, vec(scale), vec(ln_g), vec(ln_b))


def _rank_desc(m):
    n = m.shape[0]
    row = lax.broadcasted_iota(jnp.int32, m.shape, 0)
    rank = jnp.zeros(m.shape, jnp.int32)
    for j in range(n):
        mj = m[j:j + 1, :]
        beats = (mj > m) | ((mj == m) & (row > j))
        rank = rank + beats.astype(jnp.int32)
    return rank


def _router_kernel(h_ref, wt_ref, b_ref, wsgu_ref, wsd_ref, lpos_ref, gate_ref, cnt_ref,
                   shared_ref, list_ref, *, tm):
    n_e = wt_ref.shape[0]
    epg = n_e // N_EXPERT_GROUPS
    x = h_ref[...].astype(BF16)
    f = wsd_ref.shape[0]
    su = _dot(x, wsgu_ref[...])
    shared_ref[...] = _dot((_silu(su[:, :f]) * su[:, f:]).astype(BF16), wsd_ref[...])
    logits = lax.dot_general(wt_ref[...], x, (((1,), (1,)), ((), ())),
                             preferred_element_type=F32)
    scores = jax.nn.sigmoid(logits)
    biased = scores + b_ref[...]

    sub = lax.broadcasted_iota(jnp.int32, (epg, tm), 0)
    gscore = []
    for g in range(N_EXPERT_GROUPS):
        blk = biased[g * epg:(g + 1) * epg, :]
        m1 = jnp.max(blk, axis=0, keepdims=True)
        first = jnp.min(jnp.where(blk == m1, sub, epg), axis=0, keepdims=True)
        m2 = jnp.max(jnp.where(sub == first, -jnp.inf, blk), axis=0, keepdims=True)
        gscore.append(m1 + m2)
    gsel = _rank_desc(jnp.concatenate(gscore, axis=0)) < TOPK_GROUPS
    emask = jnp.concatenate(
        [jnp.broadcast_to(gsel[g:g + 1, :], (epg, tm)) for g in range(N_EXPERT_GROUPS)], axis=0)
    masked = jnp.where(emask, biased, -jnp.inf)
    sel = _rank_desc(masked) < TOP_K
    self32 = sel.astype(F32)
    selb = self32.astype(BF16)

    gate = jnp.where(sel, scores, 0.0)
    gate = gate / jnp.sum(gate, axis=0, keepdims=True) * ROUTED_SCALE

    er = lax.broadcasted_iota(jnp.int32, (n_e, n_e), 0)
    ec = lax.broadcasted_iota(jnp.int32, (n_e, n_e), 1)
    lower = (ec < er).astype(BF16)
    slot = _dot(lower, selb)
    tr = lax.broadcasted_iota(jnp.int32, (tm, tm), 0)
    tc = lax.broadcasted_iota(jnp.int32, (tm, tm), 1)
    before = (tr < tc).astype(BF16)
    lrank = _dot(selb, before)
    count = jnp.sum(self32, axis=1, keepdims=True)
    count_l = jnp.broadcast_to(count, (n_e, LANES))
    lstart = _dot(lower, count_l.astype(BF16))[:, 0:1]
    lpos = lstart + lrank

    lpos_rows, gate_rows = [], []
    for k in range(TOP_K):
        pick = sel & (slot == float(k))
        lpos_rows.append(jnp.sum(jnp.where(pick, lpos, 0.0), axis=0, keepdims=True))
        gate_rows.append(jnp.sum(jnp.where(pick, gate, 0.0), axis=0, keepdims=True))
    lpos_ref[...] = jnp.concatenate(lpos_rows, axis=0).astype(jnp.int32)
    gate_ref[...] = jnp.concatenate(gate_rows, axis=0)
    cnt_ref[0] = count_l.astype(jnp.int32)

    n_bits = tm.bit_length()
    counts_i = count_l.astype(jnp.int32)
    lane = lax.broadcasted_iota(jnp.int32, (n_e, LANES), 1)
    erow_l = lax.broadcasted_iota(jnp.int32, (n_e, LANES), 0)
    lane_row = lax.broadcasted_iota(jnp.int32, (1, LANES), 1)
    rows, lengths = [], jnp.zeros((1, LANES), jnp.int32)
    for b in range(n_bits):
        bit = lax.shift_right_logical(counts_i, b) & 1
        below = _dot(lower, bit.astype(BF16)).astype(jnp.int32)
        hit = (bit == 1) & (below == lane)
        rows.append(jnp.sum(jnp.where(hit, erow_l, 0), axis=0, keepdims=True))
        total_b = jnp.sum(bit[:, 0:1], axis=0, keepdims=True)
        lengths = lengths + jnp.where(lane_row == b, total_b, 0)
    rows.append(lengths)
    rows.append(jnp.zeros((LIST_ROWS - n_bits - 1, LANES), jnp.int32))
    list_ref[0] = jnp.concatenate(rows, axis=0)


def _router(h1, w_router_t, b_router, ws_gu, ws_down, tm):
    t, d = h1.shape
    n_e = w_router_t.shape[0]
    f = ws_down.shape[0]
    assert tm <= 256, "tile counts must stay exactly representable in bf16"
    assert tm.bit_length() < LIST_ROWS and n_e <= LANES
    lpos, gate, cnt, shared, lists = pl.pallas_call(
        functools.partial(_router_kernel, tm=tm),
        grid=(t // tm,),
        in_specs=[
            pl.BlockSpec((tm, d), lambda i: (i, 0)),
            pl.BlockSpec((n_e, d), lambda i: (0, 0)),
            pl.BlockSpec((n_e, 1), lambda i: (0, 0)),
            pl.BlockSpec((d, 2 * f), lambda i: (0, 0)),
            pl.BlockSpec((f, d), lambda i: (0, 0)),
        ],
        out_specs=[
            pl.BlockSpec((TOP_K, tm), lambda i: (0, i)),
            pl.BlockSpec((TOP_K, tm), lambda i: (0, i)),
            pl.BlockSpec((1, n_e, LANES), lambda i: (i, 0, 0)),
            pl.BlockSpec((tm, d), lambda i: (i, 0)),
            pl.BlockSpec((1, LIST_ROWS, LANES), lambda i: (i, 0, 0)),
        ],
        out_shape=[
            jax.ShapeDtypeStruct((TOP_K, t), jnp.int32),
            jax.ShapeDtypeStruct((TOP_K, t), F32),
            jax.ShapeDtypeStruct((t // tm, n_e, LANES), jnp.int32),
            jax.ShapeDtypeStruct((t, d), F32),
            jax.ShapeDtypeStruct((t // tm, LIST_ROWS, LANES), jnp.int32),
        ],
        compiler_params=_params("parallel"),
        name="router",
    )(h1, w_router_t, b_router.reshape(n_e, 1), ws_gu, ws_down)
    return lpos, gate, cnt[:, :, 0], shared, lists.reshape(-1)


HIGH_HALF = -65536


def _pack_words(lo, hi):
    bits = lambda v: lax.bitcast_convert_type(v.astype(PACKED).astype(F32), jnp.int32)
    return (bits(hi) & HIGH_HALF) | lax.shift_right_logical(bits(lo), 16)


def _word_lo(w):
    return lax.bitcast_convert_type(lax.shift_left(w, 16), F32)


def _word_hi(w):
    return lax.bitcast_convert_type(w & HIGH_HALF, F32)


def _pack_rows(x, dst_ref, r0=0):
    n, half = x.shape[0], x.shape[1] // 2
    rc = half // LANES
    words = _pack_words(x[:, :half], x[:, half:])
    for c in range(rc):
        dst_ref[pl.ds(r0 * rc + c, n, stride=rc), :] = words[:, c * LANES:(c + 1) * LANES]


def _unpack_rows(src_ref, n, rc, r0=0):
    lo, hi = [], []
    for c in range(rc):
        w = src_ref[pl.ds(r0 * rc + c, n, stride=rc), :]
        lo.append(_word_lo(w))
        hi.append(_word_hi(w))
    return jnp.concatenate(lo + hi, axis=-1)


def _pack_planes(x, dst_ref, r0):
    n, half = x.shape[0], x.shape[1] // 2
    words = _pack_words(x[:, :half], x[:, half:])
    for c in range(half // LANES):
        dst_ref[c, pl.ds(r0, n), :] = words[:, c * LANES:(c + 1) * LANES]


def _unpack_planes(src_ref, n, r0):
    lo, hi = [], []
    for c in range(src_ref.shape[0]):
        w = src_ref[c, pl.ds(r0, n), :]
        lo.append(_word_lo(w))
        hi.append(_word_hi(w))
    return jnp.concatenate(lo + hi, axis=-1)


def _segment_copies(local_ref, hbm_ref, sem, tables, list_ref, tile, n_e, rc, tm, to_hbm):
    cnt_ref, lstart_ref, gstart_ref = tables
    n_bits = tm.bit_length()
    for b in range(n_bits):
        size = 1 << b

        def copy_one(j, _, b=b, size=size):
            idx = tile * n_e + list_ref[b * LANES + j]
            n, ls, gs = cnt_ref[idx], lstart_ref[idx], gstart_ref[idx]
            done = (n & ~(2 * size - 1)) * rc
            lo = pl.multiple_of(ls + done, rc)
            go = pl.multiple_of(gs + done, rc)
            loc = local_ref.at[pl.ds(lo, size * rc), :]
            glob = hbm_ref.at[pl.ds(go, size * rc), :]
            if to_hbm:
                pltpu.make_async_copy(loc, glob, sem).start()
            else:
                pltpu.make_async_copy(glob, loc, sem).start()
            return 0

        lax.fori_loop(0, list_ref[n_bits * LANES + b], copy_one, 0)


def _wait_segments(local_ref, hbm_ref, sem):
    rows = local_ref.shape[0]
    pltpu.make_async_copy(local_ref, hbm_ref.at[pl.ds(0, rows), :], sem).wait()


def _dispatch_kernel(cnt_ref, lstart_ref, gstart_ref, lpos_ref, list_ref, h_ref, xs_ref,
                     q_ref, s_ref, sem, *, tm, n_e, rc):
    i = pl.program_id(0)
    n = pl.num_programs(0)
    slot = i % 2

    @pl.when(i >= 2)
    def _():
        _wait_segments(s_ref.at[slot], xs_ref, sem.at[slot])

    _pack_rows(h_ref[...], q_ref)

    def place_into(sl):
        def place(j, _):
            for u in range(PLACE_UNROLL):
                t = j * PLACE_UNROLL + u
                row = q_ref[pl.ds(pl.multiple_of(t * rc, rc), rc), :]
                for k in range(TOP_K):
                    dst = pl.multiple_of(lpos_ref[t * TOP_K + k], rc)
                    s_ref[sl, pl.ds(dst, rc), :] = row
            return 0

        lax.fori_loop(0, tm // PLACE_UNROLL, place, 0)

    for sl in range(2):
        pl.when(slot == sl)(functools.partial(place_into, sl))
    _segment_copies(s_ref.at[slot], xs_ref, sem.at[slot], (cnt_ref, lstart_ref, gstart_ref),
                    list_ref, i, n_e, rc, tm, to_hbm=True)

    @pl.when(i == n - 1)
    def _():
        _wait_segments(s_ref.at[slot], xs_ref, sem.at[slot])

        @pl.when(n > 1)
        def _():
            _wait_segments(s_ref.at[1 - slot], xs_ref, sem.at[1 - slot])


def _dispatch(h1, lpos, tables, lists, n_rows, tm, n_e):
    t, d = h1.shape
    rc = d // 2 // LANES
    grid_spec = pltpu.PrefetchScalarGridSpec(
        num_scalar_prefetch=3,
        grid=(t // tm,),
        in_specs=[
            pl.BlockSpec((TOP_K * tm,), lambda i, *_: (i,), memory_space=pltpu.SMEM),
            pl.BlockSpec((LIST_ROWS * LANES,), lambda i, *_: (i,), memory_space=pltpu.SMEM),
            pl.BlockSpec((tm, d), lambda i, *_: (i, 0)),
        ],
        out_specs=pl.BlockSpec(memory_space=pl.ANY),
        scratch_shapes=[
            pltpu.VMEM((tm * rc, LANES), jnp.int32),
            pltpu.VMEM((2, TOP_K * tm * rc, LANES), jnp.int32),
            pltpu.SemaphoreType.DMA((2,)),
        ],
    )
    return pl.pallas_call(
        functools.partial(_dispatch_kernel, tm=tm, n_e=n_e, rc=rc),
        grid_spec=grid_spec,
        out_shape=jax.ShapeDtypeStruct((n_rows * rc, LANES), jnp.int32),
        compiler_params=_params("arbitrary"),
        name="dispatch",
    )(*tables, lpos, lists, h1)


def _expert_kernel(be_ref, nv_ref, nxt_ref, live_ref, xs_ref, wg_ref, wu_ref, wd_ref, ys_ref,
                   xbuf, ybuf, wg_f, wu_f, wd_f, wgu_s, wd_s, sem_x, sem_y, sem_w,
                   *, layer, blk, rc):
    n_live = live_ref[0]
    f = wd_s.shape[0]

    def x_copies(g, slot):
        r0 = pl.multiple_of(g * blk, blk)
        return [pltpu.make_async_copy(xs_ref.at[pl.ds(r0, blk), c, :], xbuf.at[slot, c],
                                      sem_x.at[slot]) for c in range(rc)]

    def y_copies(g, slot):
        r0 = pl.multiple_of(g * blk, blk)
        return [pltpu.make_async_copy(ybuf.at[slot, c], ys_ref.at[pl.ds(r0, blk), c, :],
                                      sem_y.at[slot]) for c in range(rc)]

    def w_copies(e, slot):
        return (pltpu.make_async_copy(wg_ref.at[layer, e], wg_f.at[slot], sem_w.at[slot]),
                pltpu.make_async_copy(wu_ref.at[layer, e], wu_f.at[slot], sem_w.at[slot]),
                pltpu.make_async_copy(wd_ref.at[layer, e], wd_f.at[slot], sem_w.at[slot]))

    def start(copies):
        for cp in copies:
            cp.start()

    def wait(copies):
        for cp in copies:
            cp.wait()

    start(x_copies(0, 0))
    start(w_copies(be_ref[0], 0))

    def block(g, wslot):
        slot = g % 2
        e = be_ref[g]
        fresh = (g == 0) | (e != be_ref[jnp.maximum(g - 1, 0)])

        @pl.when(fresh)
        def _():
            wait(w_copies(e, wslot))
            nxt = nxt_ref[g]

            @pl.when(nxt >= 0)
            def _():
                start(w_copies(nxt, 1 - wslot))

            wgu_s[:, :f] = wg_f[wslot].astype(BF16)
            wgu_s[:, f:] = wu_f[wslot].astype(BF16)
            wd_s[...] = wd_f[wslot].astype(BF16)

        wait(x_copies(g, slot))

        @pl.when(g + 1 < n_live)
        def _():
            start(x_copies(g + 1, 1 - slot))

        @pl.when(g >= 2)
        def _():
            wait(y_copies(g - 2, slot))

        nv = nv_ref[g]
        xin, yout = xbuf.at[slot], ybuf.at[slot]
        sub = blk // EXPERT_SUBBLOCKS
        for sb in range(EXPERT_SUBBLOCKS):
            r0 = sb * sub
            x = _unpack_planes(xin, sub, r0).astype(BF16)
            gu = _dot(x, wgu_s[...])
            mid = _silu(gu[:, :f]) * gu[:, f:]
            row = r0 + lax.broadcasted_iota(jnp.int32, (sub, 1), 0)
            mid = jnp.where(row < nv, mid, 0.0).astype(BF16)
            _pack_planes(_dot(mid, wd_s[...]), yout, r0)
        start(y_copies(g, slot))
        return jnp.where(fresh, 1 - wslot, wslot)

    lax.fori_loop(0, n_live, block, 0)
    last = n_live - 1
    wait(y_copies(last, last % 2))

    @pl.when(n_live >= 2)
    def _():
        wait(y_copies(last - 1, (last - 1) % 2))


def _experts(xs, blk_e, blk_nv, blk_nxt, n_live, w_gate, w_up, w_down, layer, d):
    rc = d // 2 // LANES
    f = w_gate.shape[-1]
    n_rows = xs.shape[0] // rc
    anyspace = pl.BlockSpec(memory_space=pl.ANY)
    grid_spec = pltpu.PrefetchScalarGridSpec(
        num_scalar_prefetch=4,
        grid=(1,),
        in_specs=[anyspace] * 4,
        out_specs=anyspace,
        scratch_shapes=[
            pltpu.VMEM((2, rc, EXPERT_BLOCK, LANES), jnp.int32),
            pltpu.VMEM((2, rc, EXPERT_BLOCK, LANES), jnp.int32),
            pltpu.VMEM((2, d, f), F32),
            pltpu.VMEM((2, d, f), F32),
            pltpu.VMEM((2, f, d), F32),
            pltpu.VMEM((d, 2 * f), BF16),
            pltpu.VMEM((f, d), BF16),
            pltpu.SemaphoreType.DMA((2,)),
            pltpu.SemaphoreType.DMA((2,)),
            pltpu.SemaphoreType.DMA((2,)),
        ],
    )
    ys = pl.pallas_call(
        functools.partial(_expert_kernel, layer=layer, blk=EXPERT_BLOCK, rc=rc),
        grid_spec=grid_spec,
        out_shape=jax.ShapeDtypeStruct((n_rows, rc, LANES), jnp.int32),
        compiler_params=_params("arbitrary"),
        name="experts",
    )(blk_e, blk_nv, blk_nxt, n_live, xs.reshape(n_rows, rc, LANES), w_gate, w_up, w_down)
    return ys.reshape(xs.shape)


def _combine_kernel(cnt_ref, lstart_ref, gstart_ref, lpos_ref, gate_ref, list0_ref, list1_ref,
                    h_ref, shared_ref, g_ref, b_ref, ys_ref, o_ref, l_ref, r_ref, sem,
                    *, tm, n_e, rc, alpha):
    i = pl.program_id(0)
    n = pl.num_programs(0)
    slot = i % 2
    tables = (cnt_ref, lstart_ref, gstart_ref)

    @pl.when(i == 0)
    def _():
        _segment_copies(l_ref.at[0], ys_ref, sem.at[0], tables, list0_ref, 0, n_e, rc, tm,
                        to_hbm=False)

    @pl.when(i + 1 < n)
    def _():
        _segment_copies(l_ref.at[1 - slot], ys_ref, sem.at[1 - slot], tables, list1_ref, i + 1,
                        n_e, rc, tm, to_hbm=False)

    _wait_segments(l_ref.at[slot], ys_ref, sem.at[slot])

    def gather_from(sl):
        def gather(j, _):
            for u in range(GATHER_UNROLL):
                t = j * GATHER_UNROLL + u
                lo = hi = None
                for k in range(TOP_K):
                    src = pl.multiple_of(lpos_ref[t * TOP_K + k], rc)
                    w = l_ref[sl, pl.ds(src, rc), :]
                    gk = gate_ref[t * TOP_K + k]
                    lo = gk * _word_lo(w) if lo is None else lo + gk * _word_lo(w)
                    hi = gk * _word_hi(w) if hi is None else hi + gk * _word_hi(w)
                base = pl.multiple_of(t * 2 * rc, 2 * rc)
                r_ref[pl.ds(base, rc), :] = lo
                r_ref[pl.ds(base + rc, rc), :] = hi
            return 0

        lax.fori_loop(0, tm // GATHER_UNROLL, gather, 0)

    for sl in range(2):
        pl.when(slot == sl)(functools.partial(gather_from, sl))
    routed = jnp.concatenate(
        [r_ref[pl.ds(c, tm, stride=2 * rc), :] for c in range(2 * rc)], axis=-1)
    o_ref[...] = _ln(alpha * h_ref[...] + (routed + shared_ref[...]), g_ref[...], b_ref[...])


def _combine_ln(ys, lpos, gate, tables, lists, h1, shared, ln_g, ln_b, alpha, tm, n_e):
    t, d = h1.shape
    rc = d // 2 // LANES
    last = t // tm - 1
    vec = lambda a: a.reshape(1, d)
    grid_spec = pltpu.PrefetchScalarGridSpec(
        num_scalar_prefetch=3,
        grid=(t // tm,),
        in_specs=[
            pl.BlockSpec((TOP_K * tm,), lambda i, *_: (i,), memory_space=pltpu.SMEM),
            pl.BlockSpec((TOP_K * tm,), lambda i, *_: (i,), memory_space=pltpu.SMEM),
            pl.BlockSpec((LIST_ROWS * LANES,), lambda i, *_: (0,), memory_space=pltpu.SMEM),
            pl.BlockSpec((LIST_ROWS * LANES,), lambda i, *_: (jnp.minimum(i + 1, last),),
                         memory_space=pltpu.SMEM),
            pl.BlockSpec((tm, d), lambda i, *_: (i, 0)),
            pl.BlockSpec((tm, d), lambda i, *_: (i, 0)),
            pl.BlockSpec((1, d), lambda i, *_: (0, 0)),
            pl.BlockSpec((1, d), lambda i, *_: (0, 0)),
            pl.BlockSpec(memory_space=pl.ANY),
        ],
        out_specs=pl.BlockSpec((tm, d), lambda i, *_: (i, 0)),
        scratch_shapes=[
            pltpu.VMEM((2, TOP_K * tm * rc, LANES), jnp.int32),
            pltpu.VMEM((tm * 2 * rc, LANES), F32),
            pltpu.SemaphoreType.DMA((2,)),
        ],
    )
    return pl.pallas_call(
        functools.partial(_combine_kernel, tm=tm, n_e=n_e, rc=rc, alpha=alpha),
        grid_spec=grid_spec,
        out_shape=jax.ShapeDtypeStruct((t, d), F32),
        compiler_params=_params("arbitrary"),
        name="combine_ln",
    )(*tables, lpos, gate, lists, lists, h1, shared, vec(ln_g), vec(ln_b), ys)


def _ple_kernel(h_ref, p_ref, wp_ref, wg_ref, g_ref, b_ref, o_ref):
    h = h_ref[...]
    e = _dot(p_ref[0].astype(BF16), wp_ref[...])
    gate = jax.nn.sigmoid(_dot(h.astype(BF16), wg_ref[...]))
    o_ref[...] = h + _ln(gate * e, g_ref[...], b_ref[...])


def _ple(h2, p, layer, w_proj, w_gate, ln_g, ln_b, bm):
    t, d = h2.shape
    pd = p.shape[-1]
    vec = lambda a: a.reshape(1, d)
    return pl.pallas_call(
        _ple_kernel,
        grid=(t // bm,),
        in_specs=[
            pl.BlockSpec((bm, d), lambda i: (i, 0)),
            pl.BlockSpec((1, bm, pd), lambda i: (layer, i, 0)),
            pl.BlockSpec((pd, d), lambda i: (0, 0)),
            pl.BlockSpec((d, d), lambda i: (0, 0)),
            pl.BlockSpec((1, d), lambda i: (0, 0)),
            pl.BlockSpec((1, d), lambda i: (0, 0)),
        ],
        out_specs=pl.BlockSpec((bm, d), lambda i: (i, 0)),
        out_shape=jax.ShapeDtypeStruct((t, d), F32),
        compiler_params=_params("parallel"),
        name="ple",
    )(h2, p, w_proj, w_gate, vec(ln_g), vec(ln_b))


def _moe_tables(cnt, n_blocks, rc):
    n_e = cnt.shape[1]
    total = jnp.sum(cnt, axis=0)
    padded = (total + EXPERT_BLOCK - 1) // EXPERT_BLOCK * EXPERT_BLOCK
    ends = jnp.cumsum(padded)
    starts = ends - padded
    gstart = starts[None, :] + jnp.cumsum(cnt, axis=0) - cnt
    lstart = jnp.cumsum(cnt, axis=1) - cnt
    blk_start = jnp.arange(n_blocks, dtype=jnp.int32) * EXPERT_BLOCK
    blk_e = jnp.minimum(jnp.sum(ends[None, :] <= blk_start[:, None], axis=1), n_e - 1)
    onehot = blk_e[:, None] == jnp.arange(n_e)[None, :]
    used = jnp.sum(jnp.where(onehot, (starts + total)[None, :], 0), axis=1)
    blk_nv = jnp.clip(used - blk_start, 0, EXPERT_BLOCK)
    ids = jnp.arange(n_e)
    later = (ids[None, :] > ids[:, None]) & (total[None, :] > 0)
    nxt_e = jnp.min(jnp.where(later, ids[None, :], n_e), axis=1)
    nxt_e = jnp.where(nxt_e == n_e, -1, nxt_e)
    blk_nxt = jnp.sum(jnp.where(onehot, nxt_e[None, :], 0), axis=1)
    n_live = ends[-1:] // EXPERT_BLOCK
    i32 = lambda a: a.astype(jnp.int32)
    tables = (i32(cnt).reshape(-1), i32(lstart * rc).reshape(-1), i32(gstart * rc).reshape(-1))
    return tables, i32(blk_e), i32(blk_nv), i32(blk_nxt), i32(n_live)


def _moe_ln(h1, layer, w_router, b_router, w_gate, w_up, w_down, ws_gate, ws_up, ws_down,
            ln_g, ln_b, alpha, tm):
    t, d = h1.shape
    n_e = w_router.shape[1]
    n_blocks = t * TOP_K // EXPERT_BLOCK + n_e
    rc = d // 2 // LANES
    ws_gu = jnp.concatenate([ws_gate, ws_up], axis=1).astype(BF16)
    lpos, gate, cnt, shared, lists = _router(h1, w_router.T.astype(BF16), b_router, ws_gu,
                                             ws_down.astype(BF16), tm)
    tables, blk_e, blk_nv, blk_nxt, n_live = _moe_tables(cnt, n_blocks, rc)
    lpos = (lpos * rc).T.reshape(-1)
    gate = gate.T.reshape(-1)
    xs = _dispatch(h1, lpos, tables, lists, n_blocks * EXPERT_BLOCK, tm, n_e)
    ys = _experts(xs, blk_e, blk_nv, blk_nxt, n_live, w_gate, w_up, w_down, layer, d)
    return _combine_ln(ys, lpos, gate, tables, lists, h1, shared, ln_g, ln_b, alpha, tm, n_e)


def _tiles(t, s):
    return {"mm_m": min(1024, t), "mm_n": 512, "glu_n": 1024, "proj_m": min(512, t),
            "seq": min(512, s),
            "moe": min(256, t)}


def kernel(x, p, a_w_in, a_b_in, a_w_dw, a_b_dw, a_ln_g, a_ln_b, a_w_out, a_b_out, b_w_in, b_w_conv, b_w_out, c_w_grp, c_scale, ln1_g, ln1_b, ln2_g, ln2_b, router_w, router_b, exp_w_gate, exp_w_up, exp_w_down, sh_w_gate, sh_w_up, sh_w_down, ple_w_proj, ple_w_gate, ple_ln_g, ple_ln_b):
    bsz, s, d = x.shape
    depth = ln1_g.shape[0]
    t = bsz * s
    assert (t * TOP_K) % EXPERT_BLOCK == 0 and d % (2 * LANES) == 0
    alpha = (2 * depth) ** 0.25
    tl = _tiles(t, s)
    bn = min(tl["mm_n"], d)
    h = x.reshape(t, d)
    p_rows = p.reshape(depth, t, p.shape[-1])
    for i in range(depth):
        kind, j = i % 3, i // 3
        if kind == 0:
            u = _glu_proj(h, a_w_in[j].astype(BF16), a_b_in[j], tl["mm_m"], min(tl["glu_n"], d))
            v = _conv_ln_silu(u.reshape(bsz, s, d), a_w_dw[j], a_b_dw[j], a_ln_g[j], a_ln_b[j],
                              tl["seq"])
            h1 = _proj_residual_ln(v.reshape(t, d), a_w_out[j].astype(BF16), a_b_out[j], h,
                                   ln1_g[i], ln1_b[i], alpha, tl["proj_m"])
        elif kind == 1:
            bg, cv = _bcv_proj(h, b_w_in[j].astype(BF16), tl["mm_m"], bn)
            v = _gated_short_conv(bg.reshape(bsz, s, d), cv.reshape(bsz, s, d), b_w_conv[j],
                                  tl["seq"])
            h1 = _proj_residual_ln(v.reshape(t, d), b_w_out[j].astype(BF16),
                                   jnp.zeros((d,), F32), h, ln1_g[i], ln1_b[i], alpha,
                                   tl["proj_m"])
        else:
            h1 = _pool_mixer_ln(h.reshape(bsz, s, d), c_w_grp[j].astype(BF16), c_scale[j],
                                ln1_g[i], ln1_b[i], alpha, tl["seq"]).reshape(t, d)
        h2 = _moe_ln(h1, i, router_w[i], router_b[i], exp_w_gate, exp_w_up, exp_w_down,
                     sh_w_gate[i], sh_w_up[i], sh_w_down[i], ln2_g[i], ln2_b[i], alpha,
                     tl["moe"])
        h = _ple(h2, p_rows, i, ple_w_proj[i].astype(BF16), ple_w_gate[i].astype(BF16),
                 ple_ln_g[i], ple_ln_b[i], tl["proj_m"])
    return h.reshape(bsz, s, d)
```

```python
import functools

import jax
import jax.numpy as jnp
from jax import lax
from jax.experimental import pallas as pl
from jax.experimental.pallas import tpu as pltpu

LN_EPS = 1e-5
TOP_K = 8
N_EXPERT_GROUPS = 8
TOPK_GROUPS = 4
ROUTED_SCALE = 2.5
POOL_WINDOWS = (2, 4, 8, 16)
EXPERT_BLOCK = 512
EXPERT_SUBBLOCKS = 4
PLACE_UNROLL = 4
GATHER_UNROLL = 8
LIST_ROWS = 16
LANES = 128
SUBLANES = 8
VMEM_LIMIT = 56 * 1024 * 1024

BF16 = jnp.bfloat16
PACKED = jnp.bfloat16
F32 = jnp.float32


def _params(*sem):
    return pltpu.CompilerParams(dimension_semantics=sem, vmem_limit_bytes=VMEM_LIMIT)


def _ln(x, g, b):
    mu = jnp.mean(x, axis=-1, keepdims=True)
    xc = x - mu
    var = jnp.mean(xc * xc, axis=-1, keepdims=True)
    return xc * lax.rsqrt(var + LN_EPS) * g + b


def _dot(a, b):
    return jnp.dot(a, b, preferred_element_type=F32)


def _silu(x):
    return x * jax.nn.sigmoid(x)


def _glu_kernel(x_ref, wa_ref, wg_ref, ba_ref, bg_ref, o_ref):
    x = x_ref[...].astype(BF16)
    a = _dot(x, wa_ref[...]) + ba_ref[...]
    g = _dot(x, wg_ref[...]) + bg_ref[...]
    o_ref[...] = a * jax.nn.sigmoid(g)


def _glu_proj(h, w_in, b_in, bm, bn):
    t, d = h.shape
    nb = d // bn
    b2 = b_in.reshape(1, 2 * d)
    return pl.pallas_call(
        _glu_kernel,
        grid=(t // bm, nb),
        in_specs=[
            pl.BlockSpec((bm, d), lambda i, j: (i, 0)),
            pl.BlockSpec((d, bn), lambda i, j: (0, j)),
            pl.BlockSpec((d, bn), lambda i, j: (0, j + nb)),
            pl.BlockSpec((1, bn), lambda i, j: (0, j)),
            pl.BlockSpec((1, bn), lambda i, j: (0, j + nb)),
        ],
        out_specs=pl.BlockSpec((bm, bn), lambda i, j: (i, j)),
        out_shape=jax.ShapeDtypeStruct((t, d), F32),
        compiler_params=_params("parallel", "arbitrary"),
        name="glu_proj",
    )(h, w_in, w_in, b2, b2)


def _bcv_kernel(x_ref, wb_ref, wc_ref, wv_ref, b_ref, cv_ref):
    x = x_ref[...].astype(BF16)
    b_ref[...] = _dot(x, wb_ref[...])
    cv_ref[...] = _dot(x, wc_ref[...]) * _dot(x, wv_ref[...])


def _bcv_proj(h, w_in, bm, bn):
    t, d = h.shape
    nb = d // bn
    return pl.pallas_call(
        _bcv_kernel,
        grid=(t // bm, nb),
        in_specs=[
            pl.BlockSpec((bm, d), lambda i, j: (i, 0)),
            pl.BlockSpec((d, bn), lambda i, j: (0, j)),
            pl.BlockSpec((d, bn), lambda i, j: (0, j + nb)),
            pl.BlockSpec((d, bn), lambda i, j: (0, j + 2 * nb)),
        ],
        out_specs=[pl.BlockSpec((bm, bn), lambda i, j: (i, j))] * 2,
        out_shape=[jax.ShapeDtypeStruct((t, d), F32)] * 2,
        compiler_params=_params("parallel", "arbitrary"),
        name="bcv_proj",
    )(h, w_in, w_in, w_in)


def _fill_window(buf_ref, halo_ref, main_ref, halo):
    first = pl.program_id(1) == 0
    buf_ref[0:halo, :] = jnp.where(first, 0.0, halo_ref[0])
    buf_ref[halo:, :] = main_ref[0]


def _conv_ln_kernel(u_ref, halo_ref, w_ref, bdw_ref, g_ref, b_ref, o_ref, buf_ref, acc_ref,
                    *, width, halo, ts, rows):
    _fill_window(buf_ref, halo_ref, u_ref, halo)
    d = buf_ref.shape[1]
    off = halo - (width - 1)

    def chunk(r, _):
        r0 = pl.multiple_of(r * rows, rows)
        for c in range(d // LANES):
            cs = slice(c * LANES, (c + 1) * LANES)
            win = buf_ref[pl.ds(r0, rows + halo), cs]
            acc = jnp.zeros((rows, LANES), F32)
            for s in range(SUBLANES):
                taps = [k for k in range(width) if (off + k) % SUBLANES == s]
                if not taps:
                    continue
                ws = pltpu.roll(win, rows + halo - s, axis=0) if s else win
                for k in taps:
                    q = off + k - s
                    acc = acc + w_ref[k:k + 1, cs] * ws[q:q + rows]
            acc_ref[pl.ds(r0, rows), cs] = acc
        return 0

    lax.fori_loop(0, ts // rows, chunk, 0)
    v = _ln(acc_ref[...] + bdw_ref[...], g_ref[...], b_ref[...])
    o_ref[0] = _silu(v).astype(o_ref.dtype)


def _conv_ln_silu(u, w_dw, b_dw, ln_g, ln_b, ts):
    bsz, s, d = u.shape
    width = w_dw.shape[0]
    halo = -(-(width - 1) // SUBLANES) * SUBLANES
    hb = ts // halo
    kern = functools.partial(_conv_ln_kernel, width=width, halo=halo, ts=ts, rows=32)
    vec = lambda a: a.reshape(1, d)
    return pl.pallas_call(
        kern,
        grid=(bsz, s // ts),
        in_specs=[
            pl.BlockSpec((1, ts, d), lambda b, i: (b, i, 0)),
            pl.BlockSpec((1, halo, d), lambda b, i: (b, jnp.maximum(i * hb - 1, 0), 0)),
            pl.BlockSpec((width, d), lambda b, i: (0, 0)),
            pl.BlockSpec((1, d), lambda b, i: (0, 0)),
            pl.BlockSpec((1, d), lambda b, i: (0, 0)),
            pl.BlockSpec((1, d), lambda b, i: (0, 0)),
        ],
        out_specs=pl.BlockSpec((1, ts, d), lambda b, i: (b, i, 0)),
        out_shape=jax.ShapeDtypeStruct((bsz, s, d), BF16),
        scratch_shapes=[pltpu.VMEM((halo + ts, d), F32), pltpu.VMEM((ts, d), F32)],
        compiler_params=_params("parallel", "arbitrary"),
        name="conv_ln_silu",
    )(u, u, w_dw, vec(b_dw), vec(ln_g), vec(ln_b))


def _gated_conv_kernel(cv_ref, halo_ref, bg_ref, w_ref, o_ref, buf_ref, *, width, halo, ts):
    _fill_window(buf_ref, halo_ref, cv_ref, halo)
    off = halo - (width - 1)
    acc = w_ref[0:1, :] * buf_ref[off:off + ts, :]
    for k in range(1, width):
        acc = acc + w_ref[k:k + 1, :] * buf_ref[off + k:off + k + ts, :]
    o_ref[0] = (bg_ref[0] * acc).astype(o_ref.dtype)


def _gated_short_conv(bg, cv, w_conv, ts):
    bsz, s, d = cv.shape
    width = w_conv.shape[0]
    halo = SUBLANES
    hb = ts // halo
    kern = functools.partial(_gated_conv_kernel, width=width, halo=halo, ts=ts)
    return pl.pallas_call(
        kern,
        grid=(bsz, s // ts),
        in_specs=[
            pl.BlockSpec((1, ts, d), lambda b, i: (b, i, 0)),
            pl.BlockSpec((1, halo, d), lambda b, i: (b, jnp.maximum(i * hb - 1, 0), 0)),
            pl.BlockSpec((1, ts, d), lambda b, i: (b, i, 0)),
            pl.BlockSpec((width, d), lambda b, i: (0, 0)),
        ],
        out_specs=pl.BlockSpec((1, ts, d), lambda b, i: (b, i, 0)),
        out_shape=jax.ShapeDtypeStruct((bsz, s, d), BF16),
        scratch_shapes=[pltpu.VMEM((halo + ts, d), F32)],
        compiler_params=_params("parallel", "arbitrary"),
        name="gated_short_conv",
    )(cv, cv, bg, w_conv)


def _proj_ln_kernel(v_ref, w_ref, bias_ref, h_ref, g_ref, b_ref, o_ref, *, alpha):
    m = _dot(v_ref[...], w_ref[...]) + bias_ref[...]
    o_ref[...] = _ln(alpha * h_ref[...] + m, g_ref[...], b_ref[...])


def _proj_residual_ln(v, w_out, b_out, h, ln_g, ln_b, alpha, bm):
    t, d = h.shape
    vec = lambda a: a.reshape(1, d)
    return pl.pallas_call(
        functools.partial(_proj_ln_kernel, alpha=alpha),
        grid=(t // bm,),
        in_specs=[
            pl.BlockSpec((bm, d), lambda i: (i, 0)),
            pl.BlockSpec((d, d), lambda i: (0, 0)),
            pl.BlockSpec((1, d), lambda i: (0, 0)),
            pl.BlockSpec((bm, d), lambda i: (i, 0)),
            pl.BlockSpec((1, d), lambda i: (0, 0)),
            pl.BlockSpec((1, d), lambda i: (0, 0)),
        ],
        out_specs=pl.BlockSpec((bm, d), lambda i: (i, 0)),
        out_shape=jax.ShapeDtypeStruct((t, d), F32),
        compiler_params=_params("parallel"),
        name="proj_residual_ln",
    )(v, w_out, vec(b_out), h, vec(ln_g), vec(ln_b))


def _pool_kernel(h_ref, halo_ref, w_ref, scale_ref, g_ref, b_ref, o_ref, buf_ref,
                 *, halo, ts, alpha):
    _fill_window(buf_ref, halo_ref, h_ref, halo)
    d = buf_ref.shape[1]
    gd = d // len(POOL_WINDOWS)
    pos = pl.program_id(1) * ts + lax.broadcasted_iota(jnp.int32, (ts, 1), 0) + 1
    ys = []
    for gi, win in enumerate(POOL_WINDOWS):
        cs = slice(gi * gd, (gi + 1) * gd)
        x = buf_ref[halo:halo + ts, cs]
        tot = x
        for j in range(1, win):
            tot = tot + buf_ref[halo - j:halo - j + ts, cs]
        cnt = jnp.minimum(pos, win).astype(F32)
        z = tot / cnt - x
        ys.append(_dot(z.astype(BF16), w_ref[gi]))
    y = jnp.concatenate(ys, axis=-1) * scale_ref[...]
    o_ref[0] = _ln(alpha * h_ref[0] + y, g_ref[...], b_ref[...])


def _pool_mixer_ln(h, w_grp, scale, ln_g, ln_b, alpha, ts):
    bsz, s, d = h.shape
    halo = max(POOL_WINDOWS)
    hb = ts // halo
    vec = lambda a: a.reshape(1, d)
    return pl.pallas_call(
        functools.partial(_pool_kernel, halo=halo, ts=ts, alpha=alpha),
        grid=(bsz, s // ts),
        in_specs=[
            pl.BlockSpec((1, ts, d), lambda b, i: (b, i, 0)),
            pl.BlockSpec((1, halo, d), lambda b, i: (b, jnp.maximum(i * hb - 1, 0), 0)),
            pl.BlockSpec(w_grp.shape, lambda b, i: (0, 0, 0)),
            pl.BlockSpec((1, d), lambda b, i: (0, 0)),
            pl.BlockSpec((1, d), lambda b, i: (0, 0)),
            pl.BlockSpec((1, d), lambda b, i: (0, 0)),
        ],
        out_specs=pl.BlockSpec((1, ts, d), lambda b, i: (b, i, 0)),
        out_shape=jax.ShapeDtypeStruct((bsz, s, d), F32),
        scratch_shapes=[pltpu.VMEM((halo + ts, d), F32)],
        compiler_params=_params("parallel", "arbitrary"),
        name="pool_mixer_ln",
    )(h, h, w_grp, vec(scale), vec(ln_g), vec(ln_b))


def _rank_desc(m, side_work=()):
    n = m.shape[0]
    row = lax.broadcasted_iota(jnp.int32, m.shape, 0)
    rank = jnp.zeros(m.shape, jnp.int32)
    every = n // len(side_work) if side_work else 0
    for j in range(n):
        if side_work and j % every == 0 and j // every < len(side_work):
            side_work[j // every]()
        mj = m[j:j + 1, :]
        beats = (mj > m) | ((mj == m) & (row > j))
        rank = rank + beats.astype(jnp.int32)
    return rank


def _router_kernel(h_ref, wt_ref, b_ref, wsgu_ref, wsd_ref, lpos_ref, gate_ref, cnt_ref,
                   shared_ref, list_ref, *, tm):
    n_e = wt_ref.shape[0]
    epg = n_e // N_EXPERT_GROUPS
    x = h_ref[...].astype(BF16)
    f = wsd_ref.shape[0]
    d = wsd_ref.shape[1]
    mid = []

    def shared_up():
        su = _dot(x, wsgu_ref[...])
        mid.append((_silu(su[:, :f]) * su[:, f:]).astype(BF16))

    def shared_down(c, parts=4):
        def run():
            cs = slice(c * d // parts, (c + 1) * d // parts)
            shared_ref[:, cs] = _dot(mid[0], wsd_ref[:, cs])
        return run

    side_work = [shared_up] + [shared_down(c) for c in range(4)]
    logits = lax.dot_general(wt_ref[...], x, (((1,), (1,)), ((), ())),
                             preferred_element_type=F32)
    scores = jax.nn.sigmoid(logits)
    biased = scores + b_ref[...]

    sub = lax.broadcasted_iota(jnp.int32, (epg, tm), 0)
    gscore = []
    for g in range(N_EXPERT_GROUPS):
        blk = biased[g * epg:(g + 1) * epg, :]
        m1 = jnp.max(blk, axis=0, keepdims=True)
        first = jnp.min(jnp.where(blk == m1, sub, epg), axis=0, keepdims=True)
        m2 = jnp.max(jnp.where(sub == first, -jnp.inf, blk), axis=0, keepdims=True)
        gscore.append(m1 + m2)
    gsel = _rank_desc(jnp.concatenate(gscore, axis=0)) < TOPK_GROUPS
    emask = jnp.concatenate(
        [jnp.broadcast_to(gsel[g:g + 1, :], (epg, tm)) for g in range(N_EXPERT_GROUPS)], axis=0)
    masked = jnp.where(emask, biased, -jnp.inf)
    sel = _rank_desc(masked, side_work) < TOP_K
    self32 = sel.astype(F32)
    selb = self32.astype(BF16)

    gate = jnp.where(sel, scores, 0.0)
    gate = gate / jnp.sum(gate, axis=0, keepdims=True) * ROUTED_SCALE

    er = lax.broadcasted_iota(jnp.int32, (n_e, n_e), 0)
    ec = lax.broadcasted_iota(jnp.int32, (n_e, n_e), 1)
    lower = (ec < er).astype(BF16)
    slot = _dot(lower, selb)
    tr = lax.broadcasted_iota(jnp.int32, (tm, tm), 0)
    tc = lax.broadcasted_iota(jnp.int32, (tm, tm), 1)
    before = (tr < tc).astype(BF16)
    lrank = _dot(selb, before)
    count = jnp.sum(self32, axis=1, keepdims=True)
    count_l = jnp.broadcast_to(count, (n_e, LANES))
    lstart = _dot(lower, count_l.astype(BF16))[:, 0:1]
    lpos = lstart + lrank

    lpos_rows, gate_rows = [], []
    for k in range(TOP_K):
        pick = sel & (slot == float(k))
        lpos_rows.append(jnp.sum(jnp.where(pick, lpos, 0.0), axis=0, keepdims=True))
        gate_rows.append(jnp.sum(jnp.where(pick, gate, 0.0), axis=0, keepdims=True))
    lpos_ref[...] = jnp.concatenate(lpos_rows, axis=0).astype(jnp.int32)
    gate_ref[...] = jnp.concatenate(gate_rows, axis=0)
    cnt_ref[0] = count_l.astype(jnp.int32)

    n_bits = tm.bit_length()
    counts_i = count_l.astype(jnp.int32)
    lane = lax.broadcasted_iota(jnp.int32, (n_e, LANES), 1)
    erow_l = lax.broadcasted_iota(jnp.int32, (n_e, LANES), 0)
    lane_row = lax.broadcasted_iota(jnp.int32, (1, LANES), 1)
    rows, lengths = [], jnp.zeros((1, LANES), jnp.int32)
    for b in range(n_bits):
        bit = lax.shift_right_logical(counts_i, b) & 1
        below = _dot(lower, bit.astype(BF16)).astype(jnp.int32)
        hit = (bit == 1) & (below == lane)
        rows.append(jnp.sum(jnp.where(hit, erow_l, 0), axis=0, keepdims=True))
        total_b = jnp.sum(bit[:, 0:1], axis=0, keepdims=True)
        lengths = lengths + jnp.where(lane_row == b, total_b, 0)
    rows.append(lengths)
    rows.append(jnp.zeros((LIST_ROWS - n_bits - 1, LANES), jnp.int32))
    list_ref[0] = jnp.concatenate(rows, axis=0)


def _router(h1, w_router_t, b_router, ws_gu, ws_down, tm):
    t, d = h1.shape
    n_e = w_router_t.shape[0]
    f = ws_down.shape[0]
    assert tm <= 256, "tile counts must stay exactly representable in bf16"
    assert tm.bit_length() < LIST_ROWS and n_e <= LANES
    lpos, gate, cnt, shared, lists = pl.pallas_call(
        functools.partial(_router_kernel, tm=tm),
        grid=(t // tm,),
        in_specs=[
            pl.BlockSpec((tm, d), lambda i: (i, 0)),
            pl.BlockSpec((n_e, d), lambda i: (0, 0)),
            pl.BlockSpec((n_e, 1), lambda i: (0, 0)),
            pl.BlockSpec((d, 2 * f), lambda i: (0, 0)),
            pl.BlockSpec((f, d), lambda i: (0, 0)),
        ],
        out_specs=[
            pl.BlockSpec((TOP_K, tm), lambda i: (0, i)),
            pl.BlockSpec((TOP_K, tm), lambda i: (0, i)),
            pl.BlockSpec((1, n_e, LANES), lambda i: (i, 0, 0)),
            pl.BlockSpec((tm, d), lambda i: (i, 0)),
            pl.BlockSpec((1, LIST_ROWS, LANES), lambda i: (i, 0, 0)),
        ],
        out_shape=[
            jax.ShapeDtypeStruct((TOP_K, t), jnp.int32),
            jax.ShapeDtypeStruct((TOP_K, t), F32),
            jax.ShapeDtypeStruct((t // tm, n_e, LANES), jnp.int32),
            jax.ShapeDtypeStruct((t, d), F32),
            jax.ShapeDtypeStruct((t // tm, LIST_ROWS, LANES), jnp.int32),
        ],
        compiler_params=_params("parallel"),
        name="router",
    )(h1, w_router_t, b_router.reshape(n_e, 1), ws_gu, ws_down)
    return lpos, gate, cnt[:, :, 0], shared, lists.reshape(-1)


HIGH_HALF = -65536


def _pack_words(lo, hi):
    bits = lambda v: lax.bitcast_convert_type(v.astype(PACKED).astype(F32), jnp.int32)
    return (bits(hi) & HIGH_HALF) | lax.shift_right_logical(bits(lo), 16)


def _word_lo(w):
    return lax.bitcast_convert_type(lax.shift_left(w, 16), F32)


def _word_hi(w):
    return lax.bitcast_convert_type(w & HIGH_HALF, F32)


def _pack_rows(x, dst_ref, r0=0):
    n, half = x.shape[0], x.shape[1] // 2
    rc = half // LANES
    words = _pack_words(x[:, :half], x[:, half:])
    for c in range(rc):
        dst_ref[pl.ds(r0 * rc + c, n, stride=rc), :] = words[:, c * LANES:(c + 1) * LANES]


def _unpack_rows(src_ref, n, rc, r0=0):
    lo, hi = [], []
    for c in range(rc):
        w = src_ref[pl.ds(r0 * rc + c, n, stride=rc), :]
        lo.append(_word_lo(w))
        hi.append(_word_hi(w))
    return jnp.concatenate(lo + hi, axis=-1)


def _pack_planes(x, dst_ref, r0):
    n, half = x.shape[0], x.shape[1] // 2
    words = _pack_words(x[:, :half], x[:, half:])
    for c in range(half // LANES):
        dst_ref[c, pl.ds(r0, n), :] = words[:, c * LANES:(c + 1) * LANES]


def _unpack_planes(src_ref, n, r0):
    lo, hi = [], []
    for c in range(src_ref.shape[0]):
        w = src_ref[c, pl.ds(r0, n), :]
        lo.append(_word_lo(w))
        hi.append(_word_hi(w))
    return jnp.concatenate(lo + hi, axis=-1)


def _segment_copies(local_ref, hbm_ref, sem, tables, list_ref, tile, n_e, rc, tm, to_hbm):
    cnt_ref, lstart_ref, gstart_ref = tables
    n_bits = tm.bit_length()
    for b in range(n_bits):
        size = 1 << b

        def copy_one(j, _, b=b, size=size):
            idx = tile * n_e + list_ref[b * LANES + j]
            n, ls, gs = cnt_ref[idx], lstart_ref[idx], gstart_ref[idx]
            done = (n & ~(2 * size - 1)) * rc
            lo = pl.multiple_of(ls + done, rc)
            go = pl.multiple_of(gs + done, rc)
            loc = local_ref.at[pl.ds(lo, size * rc), :]
            glob = hbm_ref.at[pl.ds(go, size * rc), :]
            if to_hbm:
                pltpu.make_async_copy(loc, glob, sem).start()
            else:
                pltpu.make_async_copy(glob, loc, sem).start()
            return 0

        lax.fori_loop(0, list_ref[n_bits * LANES + b], copy_one, 0)


def _wait_segments(local_ref, hbm_ref, sem):
    rows = local_ref.shape[0]
    pltpu.make_async_copy(local_ref, hbm_ref.at[pl.ds(0, rows), :], sem).wait()


def _dispatch_kernel(cnt_ref, lstart_ref, gstart_ref, lpos_ref, list_ref, h_ref, xs_ref,
                     q_ref, s_ref, sem, *, tm, n_e, rc):
    i = pl.program_id(0)
    n = pl.num_programs(0)
    slot = i % 2

    @pl.when(i >= 2)
    def _():
        _wait_segments(s_ref.at[slot], xs_ref, sem.at[slot])

    _pack_rows(h_ref[...], q_ref)

    def place_into(sl):
        def place(j, _):
            for u in range(PLACE_UNROLL):
                t = j * PLACE_UNROLL + u
                row = q_ref[pl.ds(pl.multiple_of(t * rc, rc), rc), :]
                for k in range(TOP_K):
                    dst = pl.multiple_of(lpos_ref[t * TOP_K + k], rc)
                    s_ref[sl, pl.ds(dst, rc), :] = row
            return 0

        lax.fori_loop(0, tm // PLACE_UNROLL, place, 0)

    for sl in range(2):
        pl.when(slot == sl)(functools.partial(place_into, sl))
    _segment_copies(s_ref.at[slot], xs_ref, sem.at[slot], (cnt_ref, lstart_ref, gstart_ref),
                    list_ref, i, n_e, rc, tm, to_hbm=True)

    @pl.when(i == n - 1)
    def _():
        _wait_segments(s_ref.at[slot], xs_ref, sem.at[slot])

        @pl.when(n > 1)
        def _():
            _wait_segments(s_ref.at[1 - slot], xs_ref, sem.at[1 - slot])


def _dispatch(h1, lpos, tables, lists, n_rows, tm, n_e):
    t, d = h1.shape
    rc = d // 2 // LANES
    grid_spec = pltpu.PrefetchScalarGridSpec(
        num_scalar_prefetch=3,
        grid=(t // tm,),
        in_specs=[
            pl.BlockSpec((TOP_K * tm,), lambda i, *_: (i,), memory_space=pltpu.SMEM),
            pl.BlockSpec((LIST_ROWS * LANES,), lambda i, *_: (i,), memory_space=pltpu.SMEM),
            pl.BlockSpec((tm, d), lambda i, *_: (i, 0)),
        ],
        out_specs=pl.BlockSpec(memory_space=pl.ANY),
        scratch_shapes=[
            pltpu.VMEM((tm * rc, LANES), jnp.int32),
            pltpu.VMEM((2, TOP_K * tm * rc, LANES), jnp.int32),
            pltpu.SemaphoreType.DMA((2,)),
        ],
    )
    return pl.pallas_call(
        functools.partial(_dispatch_kernel, tm=tm, n_e=n_e, rc=rc),
        grid_spec=grid_spec,
        out_shape=jax.ShapeDtypeStruct((n_rows * rc, LANES), jnp.int32),
        compiler_params=_params("arbitrary"),
        name="dispatch",
    )(*tables, lpos, lists, h1)


def _expert_kernel(be_ref, nv_ref, nxt_ref, live_ref, xs_ref, wg_ref, wu_ref, wd_ref, ys_ref,
                   xbuf, ybuf, wg_f, wu_f, wd_f, wgu_s, wd_s, sem_x, sem_y, sem_w,
                   *, layer, blk, rc):
    n_live = live_ref[0]
    f = wd_s.shape[0]

    def x_copies(g, slot):
        r0 = pl.multiple_of(g * blk, blk)
        return [pltpu.make_async_copy(xs_ref.at[pl.ds(r0, blk), c, :], xbuf.at[slot, c],
                                      sem_x.at[slot]) for c in range(rc)]

    def y_copies(g, slot):
        r0 = pl.multiple_of(g * blk, blk)
        return [pltpu.make_async_copy(ybuf.at[slot, c], ys_ref.at[pl.ds(r0, blk), c, :],
                                      sem_y.at[slot]) for c in range(rc)]

    def w_copies(e, slot):
        return (pltpu.make_async_copy(wg_ref.at[layer, e], wg_f.at[slot], sem_w.at[slot]),
                pltpu.make_async_copy(wu_ref.at[layer, e], wu_f.at[slot], sem_w.at[slot]),
                pltpu.make_async_copy(wd_ref.at[layer, e], wd_f.at[slot], sem_w.at[slot]))

    def start(copies):
        for cp in copies:
            cp.start()

    def wait(copies):
        for cp in copies:
            cp.wait()

    start(x_copies(0, 0))
    start(w_copies(be_ref[0], 0))

    def block(g, wslot):
        slot = g % 2
        e = be_ref[g]
        fresh = (g == 0) | (e != be_ref[jnp.maximum(g - 1, 0)])

        @pl.when(fresh)
        def _():
            wait(w_copies(e, wslot))
            nxt = nxt_ref[g]

            @pl.when(nxt >= 0)
            def _():
                start(w_copies(nxt, 1 - wslot))

            wgu_s[:, :f] = wg_f[wslot].astype(BF16)
            wgu_s[:, f:] = wu_f[wslot].astype(BF16)
            wd_s[...] = wd_f[wslot].astype(BF16)

        wait(x_copies(g, slot))

        @pl.when(g + 1 < n_live)
        def _():
            start(x_copies(g + 1, 1 - slot))

        @pl.when(g >= 2)
        def _():
            wait(y_copies(g - 2, slot))

        nv = nv_ref[g]
        xin, yout = xbuf.at[slot], ybuf.at[slot]
        sub = blk // EXPERT_SUBBLOCKS
        gu, y = {}, {}

        def up(sb):
            x = _unpack_planes(xin, sub, sb * sub).astype(BF16)
            gu[sb] = _dot(x, wgu_s[...])

        def down(sb):
            mid = _silu(gu[sb][:, :f]) * gu[sb][:, f:]
            row = sb * sub + lax.broadcasted_iota(jnp.int32, (sub, 1), 0)
            mid = jnp.where(row < nv, mid, 0.0).astype(BF16)
            y[sb] = _dot(mid, wd_s[...])

        def out(sb):
            _pack_planes(y[sb], yout, sb * sub)

        for step in range(EXPERT_SUBBLOCKS + 2):
            for stage, lag in ((up, 0), (down, 1), (out, 2)):
                if 0 <= step - lag < EXPERT_SUBBLOCKS:
                    stage(step - lag)
        start(y_copies(g, slot))
        return jnp.where(fresh, 1 - wslot, wslot)

    lax.fori_loop(0, n_live, block, 0)
    last = n_live - 1
    wait(y_copies(last, last % 2))

    @pl.when(n_live >= 2)
    def _():
        wait(y_copies(last - 1, (last - 1) % 2))


def _experts(xs, blk_e, blk_nv, blk_nxt, n_live, w_gate, w_up, w_down, layer, d):
    rc = d // 2 // LANES
    f = w_gate.shape[-1]
    n_rows = xs.shape[0] // rc
    anyspace = pl.BlockSpec(memory_space=pl.ANY)
    grid_spec = pltpu.PrefetchScalarGridSpec(
        num_scalar_prefetch=4,
        grid=(1,),
        in_specs=[anyspace] * 4,
        out_specs=anyspace,
        scratch_shapes=[
            pltpu.VMEM((2, rc, EXPERT_BLOCK, LANES), jnp.int32),
            pltpu.VMEM((2, rc, EXPERT_BLOCK, LANES), jnp.int32),
            pltpu.VMEM((2, d, f), F32),
            pltpu.VMEM((2, d, f), F32),
            pltpu.VMEM((2, f, d), F32),
            pltpu.VMEM((d, 2 * f), BF16),
            pltpu.VMEM((f, d), BF16),
            pltpu.SemaphoreType.DMA((2,)),
            pltpu.SemaphoreType.DMA((2,)),
            pltpu.SemaphoreType.DMA((2,)),
        ],
    )
    ys = pl.pallas_call(
        functools.partial(_expert_kernel, layer=layer, blk=EXPERT_BLOCK, rc=rc),
        grid_spec=grid_spec,
        out_shape=jax.ShapeDtypeStruct((n_rows, rc, LANES), jnp.int32),
        compiler_params=_params("arbitrary"),
        name="experts",
    )(blk_e, blk_nv, blk_nxt, n_live, xs.reshape(n_rows, rc, LANES), w_gate, w_up, w_down)
    return ys.reshape(xs.shape)


def _combine_kernel(cnt_ref, lstart_ref, gstart_ref, lpos_ref, gate_ref, list0_ref, list1_ref,
                    h_ref, shared_ref, g_ref, b_ref, ys_ref, o_ref, l_ref, r_ref, sem,
                    *, tm, n_e, rc, alpha):
    i = pl.program_id(0)
    n = pl.num_programs(0)
    slot = i % 2
    tables = (cnt_ref, lstart_ref, gstart_ref)

    @pl.when(i == 0)
    def _():
        _segment_copies(l_ref.at[0], ys_ref, sem.at[0], tables, list0_ref, 0, n_e, rc, tm,
                        to_hbm=False)

    @pl.when(i + 1 < n)
    def _():
        _segment_copies(l_ref.at[1 - slot], ys_ref, sem.at[1 - slot], tables, list1_ref, i + 1,
                        n_e, rc, tm, to_hbm=False)

    _wait_segments(l_ref.at[slot], ys_ref, sem.at[slot])

    def gather_from(sl):
        def gather(j, _):
            for u in range(GATHER_UNROLL):
                t = j * GATHER_UNROLL + u
                lo = hi = None
                for k in range(TOP_K):
                    src = pl.multiple_of(lpos_ref[t * TOP_K + k], rc)
                    w = l_ref[sl, pl.ds(src, rc), :]
                    gk = gate_ref[t * TOP_K + k]
                    lo = gk * _word_lo(w) if lo is None else lo + gk * _word_lo(w)
                    hi = gk * _word_hi(w) if hi is None else hi + gk * _word_hi(w)
                base = pl.multiple_of(t * 2 * rc, 2 * rc)
                r_ref[pl.ds(base, rc), :] = lo
                r_ref[pl.ds(base + rc, rc), :] = hi
            return 0

        lax.fori_loop(0, tm // GATHER_UNROLL, gather, 0)

    for sl in range(2):
        pl.when(slot == sl)(functools.partial(gather_from, sl))
    routed = jnp.concatenate(
        [r_ref[pl.ds(c, tm, stride=2 * rc), :] for c in range(2 * rc)], axis=-1)
    o_ref[...] = _ln(alpha * h_ref[...] + (routed + shared_ref[...]), g_ref[...], b_ref[...])


def _combine_ln(ys, lpos, gate, tables, lists, h1, shared, ln_g, ln_b, alpha, tm, n_e):
    t, d = h1.shape
    rc = d // 2 // LANES
    last = t // tm - 1
    vec = lambda a: a.reshape(1, d)
    grid_spec = pltpu.PrefetchScalarGridSpec(
        num_scalar_prefetch=3,
        grid=(t // tm,),
        in_specs=[
            pl.BlockSpec((TOP_K * tm,), lambda i, *_: (i,), memory_space=pltpu.SMEM),
            pl.BlockSpec((TOP_K * tm,), lambda i, *_: (i,), memory_space=pltpu.SMEM),
            pl.BlockSpec((LIST_ROWS * LANES,), lambda i, *_: (0,), memory_space=pltpu.SMEM),
            pl.BlockSpec((LIST_ROWS * LANES,), lambda i, *_: (jnp.minimum(i + 1, last),),
                         memory_space=pltpu.SMEM),
            pl.BlockSpec((tm, d), lambda i, *_: (i, 0)),
            pl.BlockSpec((tm, d), lambda i, *_: (i, 0)),
            pl.BlockSpec((1, d), lambda i, *_: (0, 0)),
            pl.BlockSpec((1, d), lambda i, *_: (0, 0)),
            pl.BlockSpec(memory_space=pl.ANY),
        ],
        out_specs=pl.BlockSpec((tm, d), lambda i, *_: (i, 0)),
        scratch_shapes=[
            pltpu.VMEM((2, TOP_K * tm * rc, LANES), jnp.int32),
            pltpu.VMEM((tm * 2 * rc, LANES), F32),
            pltpu.SemaphoreType.DMA((2,)),
        ],
    )
    return pl.pallas_call(
        functools.partial(_combine_kernel, tm=tm, n_e=n_e, rc=rc, alpha=alpha),
        grid_spec=grid_spec,
        out_shape=jax.ShapeDtypeStruct((t, d), F32),
        compiler_params=_params("arbitrary"),
        name="combine_ln",
    )(*tables, lpos, gate, lists, lists, h1, shared, vec(ln_g), vec(ln_b), ys)


def _ple_kernel(h_ref, p_ref, wp_ref, wg_ref, g_ref, b_ref, o_ref):
    h = h_ref[...]
    e = _dot(p_ref[0].astype(BF16), wp_ref[...])
    gate = jax.nn.sigmoid(_dot(h.astype(BF16), wg_ref[...]))
    o_ref[...] = h + _ln(gate * e, g_ref[...], b_ref[...])


def _ple(h2, p, layer, w_proj, w_gate, ln_g, ln_b, bm):
    t, d = h2.shape
    pd = p.shape[-1]
    vec = lambda a: a.reshape(1, d)
    return pl.pallas_call(
        _ple_kernel,
        grid=(t // bm,),
        in_specs=[
            pl.BlockSpec((bm, d), lambda i: (i, 0)),
            pl.BlockSpec((1, bm, pd), lambda i: (layer, i, 0)),
            pl.BlockSpec((pd, d), lambda i: (0, 0)),
            pl.BlockSpec((d, d), lambda i: (0, 0)),
            pl.BlockSpec((1, d), lambda i: (0, 0)),
            pl.BlockSpec((1, d), lambda i: (0, 0)),
        ],
        out_specs=pl.BlockSpec((bm, d), lambda i: (i, 0)),
        out_shape=jax.ShapeDtypeStruct((t, d), F32),
        compiler_params=_params("parallel"),
        name="ple",
    )(h2, p, w_proj, w_gate, vec(ln_g), vec(ln_b))


def _moe_tables(cnt, n_blocks, rc):
    n_e = cnt.shape[1]
    total = jnp.sum(cnt, axis=0)
    padded = (total + EXPERT_BLOCK - 1) // EXPERT_BLOCK * EXPERT_BLOCK
    ends = jnp.cumsum(padded)
    starts = ends - padded
    gstart = starts[None, :] + jnp.cumsum(cnt, axis=0) - cnt
    lstart = jnp.cumsum(cnt, axis=1) - cnt
    blk_start = jnp.arange(n_blocks, dtype=jnp.int32) * EXPERT_BLOCK
    blk_e = jnp.minimum(jnp.sum(ends[None, :] <= blk_start[:, None], axis=1), n_e - 1)
    onehot = blk_e[:, None] == jnp.arange(n_e)[None, :]
    used = jnp.sum(jnp.where(onehot, (starts + total)[None, :], 0), axis=1)
    blk_nv = jnp.clip(used - blk_start, 0, EXPERT_BLOCK)
    ids = jnp.arange(n_e)
    later = (ids[None, :] > ids[:, None]) & (total[None, :] > 0)
    nxt_e = jnp.min(jnp.where(later, ids[None, :], n_e), axis=1)
    nxt_e = jnp.where(nxt_e == n_e, -1, nxt_e)
    blk_nxt = jnp.sum(jnp.where(onehot, nxt_e[None, :], 0), axis=1)
    n_live = ends[-1:] // EXPERT_BLOCK
    i32 = lambda a: a.astype(jnp.int32)
    tables = (i32(cnt).reshape(-1), i32(lstart * rc).reshape(-1), i32(gstart * rc).reshape(-1))
    return tables, i32(blk_e), i32(blk_nv), i32(blk_nxt), i32(n_live)


def _moe_ln(h1, layer, w_router, b_router, w_gate, w_up, w_down, ws_gate, ws_up, ws_down,
            ln_g, ln_b, alpha, tm):
    t, d = h1.shape
    n_e = w_router.shape[1]
    n_blocks = t * TOP_K // EXPERT_BLOCK + n_e
    rc = d // 2 // LANES
    ws_gu = jnp.concatenate([ws_gate, ws_up], axis=1).astype(BF16)
    lpos, gate, cnt, shared, lists = _router(h1, w_router.T.astype(BF16), b_router, ws_gu,
                                             ws_down.astype(BF16), tm)
    tables, blk_e, blk_nv, blk_nxt, n_live = _moe_tables(cnt, n_blocks, rc)
    lpos = (lpos * rc).T.reshape(-1)
    gate = gate.T.reshape(-1)
    xs = _dispatch(h1, lpos, tables, lists, n_blocks * EXPERT_BLOCK, tm, n_e)
    ys = _experts(xs, blk_e, blk_nv, blk_nxt, n_live, w_gate, w_up, w_down, layer, d)
    return _combine_ln(ys, lpos, gate, tables, lists, h1, shared, ln_g, ln_b, alpha, tm, n_e)


def _tiles(t, s):
    return {"mm_m": min(1024, t), "mm_n": 512, "glu_n": 1024, "proj_m": min(512, t),
            "seq": min(512, s),
            "moe": min(256, t)}


def kernel(x, p, a_w_in, a_b_in, a_w_dw, a_b_dw, a_ln_g, a_ln_b, a_w_out, a_b_out, b_w_in, b_w_conv, b_w_out, c_w_grp, c_scale, ln1_g, ln1_b, ln2_g, ln2_b, router_w, router_b, exp_w_gate, exp_w_up, exp_w_down, sh_w_gate, sh_w_up, sh_w_down, ple_w_proj, ple_w_gate, ple_ln_g, ple_ln_b):
    bsz, s, d = x.shape
    depth = ln1_g.shape[0]
    t = bsz * s
    assert (t * TOP_K) % EXPERT_BLOCK == 0 and d % (2 * LANES) == 0
    alpha = (2 * depth) ** 0.25
    tl = _tiles(t, s)
    bn = min(tl["mm_n"], d)
    h = x.reshape(t, d)
    p_rows = p.reshape(depth, t, p.shape[-1])
    for i in range(depth):
        kind, j = i % 3, i // 3
        if kind == 0:
            u = _glu_proj(h, a_w_in[j].astype(BF16), a_b_in[j], tl["mm_m"], min(tl["glu_n"], d))
            v = _conv_ln_silu(u.reshape(bsz, s, d), a_w_dw[j], a_b_dw[j], a_ln_g[j], a_ln_b[j],
                              tl["seq"])
            h1 = _proj_residual_ln(v.reshape(t, d), a_w_out[j].astype(BF16), a_b_out[j], h,
                                   ln1_g[i], ln1_b[i], alpha, tl["proj_m"])
        elif kind == 1:
            bg, cv = _bcv_proj(h, b_w_in[j].astype(BF16), tl["mm_m"], bn)
            v = _gated_short_conv(bg.reshape(bsz, s, d), cv.reshape(bsz, s, d), b_w_conv[j],
                                  tl["seq"])
            h1 = _proj_residual_ln(v.reshape(t, d), b_w_out[j].astype(BF16),
                                   jnp.zeros((d,), F32), h, ln1_g[i], ln1_b[i], alpha,
                                   tl["proj_m"])
        else:
            h1 = _pool_mixer_ln(h.reshape(bsz, s, d), c_w_grp[j].astype(BF16), c_scale[j],
                                ln1_g[i], ln1_b[i], alpha, tl["seq"]).reshape(t, d)
        h2 = _moe_ln(h1, i, router_w[i], router_b[i], exp_w_gate, exp_w_up, exp_w_down,
                     sh_w_gate[i], sh_w_up[i], sh_w_down[i], ln2_g[i], ln2_b[i], alpha,
                     tl["moe"])
        h = _ple(h2, p_rows, i, ple_w_proj[i].astype(BF16), ple_w_gate[i].astype(BF16),
                 ple_ln_g[i], ple_ln_b[i], tl["proj_m"])
    return h.reshape(bsz, s, d)
```

```python
import functools

import jax
import jax.numpy as jnp
from jax import lax
from jax.experimental import pallas as pl
from jax.experimental.pallas import tpu as pltpu

LN_EPS = 1e-5
TOP_K = 8
N_EXPERT_GROUPS = 8
TOPK_GROUPS = 4
ROUTED_SCALE = 2.5
POOL_WINDOWS = (2, 4, 8, 16)
EXPERT_BLOCK = 512
EXPERT_SUBBLOCKS = 4
ROW_SUBBLOCKS = 4
PLACE_UNROLL = 4
GATHER_UNROLL = 8
LIST_ROWS = 16
LANES = 128
SUBLANES = 8
VMEM_LIMIT = 56 * 1024 * 1024

BF16 = jnp.bfloat16
PACKED = jnp.bfloat16
F32 = jnp.float32


def _params(*sem):
    return pltpu.CompilerParams(dimension_semantics=sem, vmem_limit_bytes=VMEM_LIMIT)


def _ln(x, g, b):
    mu = jnp.mean(x, axis=-1, keepdims=True)
    xc = x - mu
    var = jnp.mean(xc * xc, axis=-1, keepdims=True)
    return xc * lax.rsqrt(var + LN_EPS) * g + b


def _dot(a, b):
    return jnp.dot(a, b, preferred_element_type=F32)


def _silu(x):
    return x * jax.nn.sigmoid(x)


def _glu_kernel(x_ref, wa_ref, wg_ref, ba_ref, bg_ref, o_ref):
    x = x_ref[...].astype(BF16)
    a = _dot(x, wa_ref[...]) + ba_ref[...]
    g = _dot(x, wg_ref[...]) + bg_ref[...]
    o_ref[...] = a * jax.nn.sigmoid(g)


def _glu_proj(h, w_in, b_in, bm, bn):
    t, d = h.shape
    nb = d // bn
    b2 = b_in.reshape(1, 2 * d)
    return pl.pallas_call(
        _glu_kernel,
        grid=(t // bm, nb),
        in_specs=[
            pl.BlockSpec((bm, d), lambda i, j: (i, 0)),
            pl.BlockSpec((d, bn), lambda i, j: (0, j)),
            pl.BlockSpec((d, bn), lambda i, j: (0, j + nb)),
            pl.BlockSpec((1, bn), lambda i, j: (0, j)),
            pl.BlockSpec((1, bn), lambda i, j: (0, j + nb)),
        ],
        out_specs=pl.BlockSpec((bm, bn), lambda i, j: (i, j)),
        out_shape=jax.ShapeDtypeStruct((t, d), F32),
        compiler_params=_params("parallel", "arbitrary"),
        name="glu_proj",
    )(h, w_in, w_in, b2, b2)


def _bcv_kernel(x_ref, wb_ref, wc_ref, wv_ref, b_ref, cv_ref):
    x = x_ref[...].astype(BF16)
    b_ref[...] = _dot(x, wb_ref[...])
    cv_ref[...] = _dot(x, wc_ref[...]) * _dot(x, wv_ref[...])


def _bcv_proj(h, w_in, bm, bn):
    t, d = h.shape
    nb = d // bn
    return pl.pallas_call(
        _bcv_kernel,
        grid=(t // bm, nb),
        in_specs=[
            pl.BlockSpec((bm, d), lambda i, j: (i, 0)),
            pl.BlockSpec((d, bn), lambda i, j: (0, j)),
            pl.BlockSpec((d, bn), lambda i, j: (0, j + nb)),
            pl.BlockSpec((d, bn), lambda i, j: (0, j + 2 * nb)),
        ],
        out_specs=[pl.BlockSpec((bm, bn), lambda i, j: (i, j))] * 2,
        out_shape=[jax.ShapeDtypeStruct((t, d), F32)] * 2,
        compiler_params=_params("parallel", "arbitrary"),
        name="bcv_proj",
    )(h, w_in, w_in, w_in)


def _fill_window(buf_ref, halo_ref, main_ref, halo):
    first = pl.program_id(1) == 0
    buf_ref[0:halo, :] = jnp.where(first, 0.0, halo_ref[0])
    buf_ref[halo:, :] = main_ref[0]


def _conv_ln_kernel(u_ref, halo_ref, w_ref, bdw_ref, g_ref, b_ref, o_ref, buf_ref, acc_ref,
                    *, width, halo, ts, rows):
    _fill_window(buf_ref, halo_ref, u_ref, halo)
    d = buf_ref.shape[1]
    off = halo - (width - 1)

    def chunk(r, _):
        r0 = pl.multiple_of(r * rows, rows)
        for c in range(d // LANES):
            cs = slice(c * LANES, (c + 1) * LANES)
            win = buf_ref[pl.ds(r0, rows + halo), cs]
            acc = jnp.zeros((rows, LANES), F32)
            for s in range(SUBLANES):
                taps = [k for k in range(width) if (off + k) % SUBLANES == s]
                if not taps:
                    continue
                ws = pltpu.roll(win, rows + halo - s, axis=0) if s else win
                for k in taps:
                    q = off + k - s
                    acc = acc + w_ref[k:k + 1, cs] * ws[q:q + rows]
            acc_ref[pl.ds(r0, rows), cs] = acc
        return 0

    lax.fori_loop(0, ts // rows, chunk, 0)
    v = _ln(acc_ref[...] + bdw_ref[...], g_ref[...], b_ref[...])
    o_ref[0] = _silu(v).astype(o_ref.dtype)


def _conv_ln_silu(u, w_dw, b_dw, ln_g, ln_b, ts):
    bsz, s, d = u.shape
    width = w_dw.shape[0]
    halo = -(-(width - 1) // SUBLANES) * SUBLANES
    hb = ts // halo
    kern = functools.partial(_conv_ln_kernel, width=width, halo=halo, ts=ts, rows=32)
    vec = lambda a: a.reshape(1, d)
    return pl.pallas_call(
        kern,
        grid=(bsz, s // ts),
        in_specs=[
            pl.BlockSpec((1, ts, d), lambda b, i: (b, i, 0)),
            pl.BlockSpec((1, halo, d), lambda b, i: (b, jnp.maximum(i * hb - 1, 0), 0)),
            pl.BlockSpec((width, d), lambda b, i: (0, 0)),
            pl.BlockSpec((1, d), lambda b, i: (0, 0)),
            pl.BlockSpec((1, d), lambda b, i: (0, 0)),
            pl.BlockSpec((1, d), lambda b, i: (0, 0)),
        ],
        out_specs=pl.BlockSpec((1, ts, d), lambda b, i: (b, i, 0)),
        out_shape=jax.ShapeDtypeStruct((bsz, s, d), BF16),
        scratch_shapes=[pltpu.VMEM((halo + ts, d), F32), pltpu.VMEM((ts, d), F32)],
        compiler_params=_params("parallel", "arbitrary"),
        name="conv_ln_silu",
    )(u, u, w_dw, vec(b_dw), vec(ln_g), vec(ln_b))


def _gated_conv_kernel(cv_ref, halo_ref, bg_ref, w_ref, o_ref, buf_ref, *, width, halo, ts):
    _fill_window(buf_ref, halo_ref, cv_ref, halo)
    off = halo - (width - 1)
    acc = w_ref[0:1, :] * buf_ref[off:off + ts, :]
    for k in range(1, width):
        acc = acc + w_ref[k:k + 1, :] * buf_ref[off + k:off + k + ts, :]
    o_ref[0] = (bg_ref[0] * acc).astype(o_ref.dtype)


def _gated_short_conv(bg, cv, w_conv, ts):
    bsz, s, d = cv.shape
    width = w_conv.shape[0]
    halo = SUBLANES
    hb = ts // halo
    kern = functools.partial(_gated_conv_kernel, width=width, halo=halo, ts=ts)
    return pl.pallas_call(
        kern,
        grid=(bsz, s // ts),
        in_specs=[
            pl.BlockSpec((1, ts, d), lambda b, i: (b, i, 0)),
            pl.BlockSpec((1, halo, d), lambda b, i: (b, jnp.maximum(i * hb - 1, 0), 0)),
            pl.BlockSpec((1, ts, d), lambda b, i: (b, i, 0)),
            pl.BlockSpec((width, d), lambda b, i: (0, 0)),
        ],
        out_specs=pl.BlockSpec((1, ts, d), lambda b, i: (b, i, 0)),
        out_shape=jax.ShapeDtypeStruct((bsz, s, d), BF16),
        scratch_shapes=[pltpu.VMEM((halo + ts, d), F32)],
        compiler_params=_params("parallel", "arbitrary"),
        name="gated_short_conv",
    )(cv, cv, bg, w_conv)


def _skewed(n_sub, *stages):
    for step in range(n_sub + len(stages) - 1):
        for lag, stage in enumerate(stages):
            if 0 <= step - lag < n_sub:
                stage(step - lag)


def _proj_ln_kernel(v_ref, w_ref, bias_ref, h_ref, g_ref, b_ref, o_ref, *, alpha):
    sub = v_ref.shape[0] // ROW_SUBBLOCKS
    acc = {}

    def matmul(i):
        acc[i] = _dot(v_ref[pl.ds(i * sub, sub), :], w_ref[...])

    def epilogue(i):
        rows = pl.ds(i * sub, sub)
        m = acc[i] + bias_ref[...]
        o_ref[rows, :] = _ln(alpha * h_ref[rows, :] + m, g_ref[...], b_ref[...])

    _skewed(ROW_SUBBLOCKS, matmul, epilogue)


def _proj_residual_ln(v, w_out, b_out, h, ln_g, ln_b, alpha, bm):
    t, d = h.shape
    vec = lambda a: a.reshape(1, d)
    return pl.pallas_call(
        functools.partial(_proj_ln_kernel, alpha=alpha),
        grid=(t // bm,),
        in_specs=[
            pl.BlockSpec((bm, d), lambda i: (i, 0)),
            pl.BlockSpec((d, d), lambda i: (0, 0)),
            pl.BlockSpec((1, d), lambda i: (0, 0)),
            pl.BlockSpec((bm, d), lambda i: (i, 0)),
            pl.BlockSpec((1, d), lambda i: (0, 0)),
            pl.BlockSpec((1, d), lambda i: (0, 0)),
        ],
        out_specs=pl.BlockSpec((bm, d), lambda i: (i, 0)),
        out_shape=jax.ShapeDtypeStruct((t, d), F32),
        compiler_params=_params("parallel"),
        name="proj_residual_ln",
    )(v, w_out, vec(b_out), h, vec(ln_g), vec(ln_b))


def _pool_kernel(h_ref, halo_ref, w_ref, scale_ref, g_ref, b_ref, o_ref, buf_ref,
                 *, halo, ts, alpha):
    _fill_window(buf_ref, halo_ref, h_ref, halo)
    d = buf_ref.shape[1]
    gd = d // len(POOL_WINDOWS)
    pos = pl.program_id(1) * ts + lax.broadcasted_iota(jnp.int32, (ts, 1), 0) + 1
    ys = []
    for gi, win in enumerate(POOL_WINDOWS):
        cs = slice(gi * gd, (gi + 1) * gd)
        x = buf_ref[halo:halo + ts, cs]
        tot = x
        for j in range(1, win):
            tot = tot + buf_ref[halo - j:halo - j + ts, cs]
        cnt = jnp.minimum(pos, win).astype(F32)
        z = tot / cnt - x
        ys.append(_dot(z.astype(BF16), w_ref[gi]))
    y = jnp.concatenate(ys, axis=-1) * scale_ref[...]
    o_ref[0] = _ln(alpha * h_ref[0] + y, g_ref[...], b_ref[...])


def _pool_mixer_ln(h, w_grp, scale, ln_g, ln_b, alpha, ts):
    bsz, s, d = h.shape
    halo = max(POOL_WINDOWS)
    hb = ts // halo
    vec = lambda a: a.reshape(1, d)
    return pl.pallas_call(
        functools.partial(_pool_kernel, halo=halo, ts=ts, alpha=alpha),
        grid=(bsz, s // ts),
        in_specs=[
            pl.BlockSpec((1, ts, d), lambda b, i: (b, i, 0)),
            pl.BlockSpec((1, halo, d), lambda b, i: (b, jnp.maximum(i * hb - 1, 0), 0)),
            pl.BlockSpec(w_grp.shape, lambda b, i: (0, 0, 0)),
            pl.BlockSpec((1, d), lambda b, i: (0, 0)),
            pl.BlockSpec((1, d), lambda b, i: (0, 0)),
            pl.BlockSpec((1, d), lambda b, i: (0, 0)),
        ],
        out_specs=pl.BlockSpec((1, ts, d), lambda b, i: (b, i, 0)),
        out_shape=jax.ShapeDtypeStruct((bsz, s, d), F32),
        scratch_shapes=[pltpu.VMEM((halo + ts, d), F32)],
        compiler_params=_params("parallel", "arbitrary"),
        name="pool_mixer_ln",
    )(h, h, w_grp, vec(scale), vec(ln_g), vec(ln_b))


def _rank_desc(m, side_work=()):
    n = m.shape[0]
    row = lax.broadcasted_iota(jnp.int32, m.shape, 0)
    rank = jnp.zeros(m.shape, jnp.int32)
    every = n // len(side_work) if side_work else 0
    for j in range(n):
        if side_work and j % every == 0 and j // every < len(side_work):
            side_work[j // every]()
        mj = m[j:j + 1, :]
        beats = (mj > m) | ((mj == m) & (row > j))
        rank = rank + beats.astype(jnp.int32)
    return rank


def _router_kernel(h_ref, wt_ref, b_ref, wsgu_ref, wsd_ref, lpos_ref, gate_ref, cnt_ref,
                   shared_ref, list_ref, *, tm):
    n_e = wt_ref.shape[0]
    epg = n_e // N_EXPERT_GROUPS
    x = h_ref[...].astype(BF16)
    f = wsd_ref.shape[0]
    d = wsd_ref.shape[1]
    mid = []

    def shared_up():
        su = _dot(x, wsgu_ref[...])
        mid.append((_silu(su[:, :f]) * su[:, f:]).astype(BF16))

    def shared_down(c, parts=4):
        def run():
            cs = slice(c * d // parts, (c + 1) * d // parts)
            shared_ref[:, cs] = _dot(mid[0], wsd_ref[:, cs])
        return run

    side_work = [shared_up] + [shared_down(c) for c in range(4)]
    logits = lax.dot_general(wt_ref[...], x, (((1,), (1,)), ((), ())),
                             preferred_element_type=F32)
    scores = jax.nn.sigmoid(logits)
    biased = scores + b_ref[...]

    sub = lax.broadcasted_iota(jnp.int32, (epg, tm), 0)
    gscore = []
    for g in range(N_EXPERT_GROUPS):
        blk = biased[g * epg:(g + 1) * epg, :]
        m1 = jnp.max(blk, axis=0, keepdims=True)
        first = jnp.min(jnp.where(blk == m1, sub, epg), axis=0, keepdims=True)
        m2 = jnp.max(jnp.where(sub == first, -jnp.inf, blk), axis=0, keepdims=True)
        gscore.append(m1 + m2)
    gsel = _rank_desc(jnp.concatenate(gscore, axis=0)) < TOPK_GROUPS
    emask = jnp.concatenate(
        [jnp.broadcast_to(gsel[g:g + 1, :], (epg, tm)) for g in range(N_EXPERT_GROUPS)], axis=0)
    masked = jnp.where(emask, biased, -jnp.inf)
    sel = _rank_desc(masked, side_work) < TOP_K
    self32 = sel.astype(F32)
    selb = self32.astype(BF16)

    gate = jnp.where(sel, scores, 0.0)
    gate = gate / jnp.sum(gate, axis=0, keepdims=True) * ROUTED_SCALE

    er = lax.broadcasted_iota(jnp.int32, (n_e, n_e), 0)
    ec = lax.broadcasted_iota(jnp.int32, (n_e, n_e), 1)
    lower = (ec < er).astype(BF16)
    slot = _dot(lower, selb)
    tr = lax.broadcasted_iota(jnp.int32, (tm, tm), 0)
    tc = lax.broadcasted_iota(jnp.int32, (tm, tm), 1)
    before = (tr < tc).astype(BF16)
    lrank = _dot(selb, before)
    count = jnp.sum(self32, axis=1, keepdims=True)
    count_l = jnp.broadcast_to(count, (n_e, LANES))
    lstart = _dot(lower, count_l.astype(BF16))[:, 0:1]
    lpos = lstart + lrank

    lpos_rows, gate_rows = [], []
    for k in range(TOP_K):
        pick = sel & (slot == float(k))
        lpos_rows.append(jnp.sum(jnp.where(pick, lpos, 0.0), axis=0, keepdims=True))
        gate_rows.append(jnp.sum(jnp.where(pick, gate, 0.0), axis=0, keepdims=True))
    lpos_ref[...] = jnp.concatenate(lpos_rows, axis=0).astype(jnp.int32)
    gate_ref[...] = jnp.concatenate(gate_rows, axis=0)
    cnt_ref[0] = count_l.astype(jnp.int32)

    n_bits = tm.bit_length()
    counts_i = count_l.astype(jnp.int32)
    lane = lax.broadcasted_iota(jnp.int32, (n_e, LANES), 1)
    erow_l = lax.broadcasted_iota(jnp.int32, (n_e, LANES), 0)
    lane_row = lax.broadcasted_iota(jnp.int32, (1, LANES), 1)
    rows, lengths = [], jnp.zeros((1, LANES), jnp.int32)
    for b in range(n_bits):
        bit = lax.shift_right_logical(counts_i, b) & 1
        below = _dot(lower, bit.astype(BF16)).astype(jnp.int32)
        hit = (bit == 1) & (below == lane)
        rows.append(jnp.sum(jnp.where(hit, erow_l, 0), axis=0, keepdims=True))
        total_b = jnp.sum(bit[:, 0:1], axis=0, keepdims=True)
        lengths = lengths + jnp.where(lane_row == b, total_b, 0)
    rows.append(lengths)
    rows.append(jnp.zeros((LIST_ROWS - n_bits - 1, LANES), jnp.int32))
    list_ref[0] = jnp.concatenate(rows, axis=0)


def _router(h1, w_router_t, b_router, ws_gu, ws_down, tm):
    t, d = h1.shape
    n_e = w_router_t.shape[0]
    f = ws_down.shape[0]
    assert tm <= 256, "tile counts must stay exactly representable in bf16"
    assert tm.bit_length() < LIST_ROWS and n_e <= LANES
    lpos, gate, cnt, shared, lists = pl.pallas_call(
        functools.partial(_router_kernel, tm=tm),
        grid=(t // tm,),
        in_specs=[
            pl.BlockSpec((tm, d), lambda i: (i, 0)),
            pl.BlockSpec((n_e, d), lambda i: (0, 0)),
            pl.BlockSpec((n_e, 1), lambda i: (0, 0)),
            pl.BlockSpec((d, 2 * f), lambda i: (0, 0)),
            pl.BlockSpec((f, d), lambda i: (0, 0)),
        ],
        out_specs=[
            pl.BlockSpec((TOP_K, tm), lambda i: (0, i)),
            pl.BlockSpec((TOP_K, tm), lambda i: (0, i)),
            pl.BlockSpec((1, n_e, LANES), lambda i: (i, 0, 0)),
            pl.BlockSpec((tm, d), lambda i: (i, 0)),
            pl.BlockSpec((1, LIST_ROWS, LANES), lambda i: (i, 0, 0)),
        ],
        out_shape=[
            jax.ShapeDtypeStruct((TOP_K, t), jnp.int32),
            jax.ShapeDtypeStruct((TOP_K, t), F32),
            jax.ShapeDtypeStruct((t // tm, n_e, LANES), jnp.int32),
            jax.ShapeDtypeStruct((t, d), F32),
            jax.ShapeDtypeStruct((t // tm, LIST_ROWS, LANES), jnp.int32),
        ],
        compiler_params=_params("parallel"),
        name="router",
    )(h1, w_router_t, b_router.reshape(n_e, 1), ws_gu, ws_down)
    return lpos, gate, cnt[:, :, 0], shared, lists.reshape(-1)


HIGH_HALF = -65536


def _pack_words(lo, hi):
    bits = lambda v: lax.bitcast_convert_type(v.astype(PACKED).astype(F32), jnp.int32)
    return (bits(hi) & HIGH_HALF) | lax.shift_right_logical(bits(lo), 16)


def _word_lo(w):
    return lax.bitcast_convert_type(lax.shift_left(w, 16), F32)


def _word_hi(w):
    return lax.bitcast_convert_type(w & HIGH_HALF, F32)


def _pack_rows(x, dst_ref, r0=0):
    n, half = x.shape[0], x.shape[1] // 2
    rc = half // LANES
    words = _pack_words(x[:, :half], x[:, half:])
    for c in range(rc):
        dst_ref[pl.ds(r0 * rc + c, n, stride=rc), :] = words[:, c * LANES:(c + 1) * LANES]


def _unpack_rows(src_ref, n, rc, r0=0):
    lo, hi = [], []
    for c in range(rc):
        w = src_ref[pl.ds(r0 * rc + c, n, stride=rc), :]
        lo.append(_word_lo(w))
        hi.append(_word_hi(w))
    return jnp.concatenate(lo + hi, axis=-1)


def _pack_planes(x, dst_ref, r0):
    n, half = x.shape[0], x.shape[1] // 2
    words = _pack_words(x[:, :half], x[:, half:])
    for c in range(half // LANES):
        dst_ref[c, pl.ds(r0, n), :] = words[:, c * LANES:(c + 1) * LANES]


def _unpack_planes(src_ref, n, r0):
    lo, hi = [], []
    for c in range(src_ref.shape[0]):
        w = src_ref[c, pl.ds(r0, n), :]
        lo.append(_word_lo(w))
        hi.append(_word_hi(w))
    return jnp.concatenate(lo + hi, axis=-1)


def _segment_copies(local_ref, hbm_ref, sem, tables, list_ref, tile, n_e, rc, tm, to_hbm):
    cnt_ref, lstart_ref, gstart_ref = tables
    n_bits = tm.bit_length()
    for b in range(n_bits):
        size = 1 << b

        def copy_one(j, _, b=b, size=size):
            idx = tile * n_e + list_ref[b * LANES + j]
            n, ls, gs = cnt_ref[idx], lstart_ref[idx], gstart_ref[idx]
            done = (n & ~(2 * size - 1)) * rc
            lo = pl.multiple_of(ls + done, rc)
            go = pl.multiple_of(gs + done, rc)
            loc = local_ref.at[pl.ds(lo, size * rc), :]
            glob = hbm_ref.at[pl.ds(go, size * rc), :]
            if to_hbm:
                pltpu.make_async_copy(loc, glob, sem).start()
            else:
                pltpu.make_async_copy(glob, loc, sem).start()
            return 0

        lax.fori_loop(0, list_ref[n_bits * LANES + b], copy_one, 0)


def _wait_segments(local_ref, hbm_ref, sem):
    rows = local_ref.shape[0]
    pltpu.make_async_copy(local_ref, hbm_ref.at[pl.ds(0, rows), :], sem).wait()


def _dispatch_kernel(cnt_ref, lstart_ref, gstart_ref, lpos_ref, list_ref, h_ref, xs_ref,
                     q_ref, s_ref, sem, *, tm, n_e, rc):
    i = pl.program_id(0)
    n = pl.num_programs(0)
    slot = i % 2

    @pl.when(i >= 2)
    def _():
        _wait_segments(s_ref.at[slot], xs_ref, sem.at[slot])

    _pack_rows(h_ref[...], q_ref)

    def place_into(sl):
        def place(j, _):
            for u in range(PLACE_UNROLL):
                t = j * PLACE_UNROLL + u
                row = q_ref[pl.ds(pl.multiple_of(t * rc, rc), rc), :]
                for k in range(TOP_K):
                    dst = pl.multiple_of(lpos_ref[t * TOP_K + k], rc)
                    s_ref[sl, pl.ds(dst, rc), :] = row
            return 0

        lax.fori_loop(0, tm // PLACE_UNROLL, place, 0)

    for sl in range(2):
        pl.when(slot == sl)(functools.partial(place_into, sl))
    _segment_copies(s_ref.at[slot], xs_ref, sem.at[slot], (cnt_ref, lstart_ref, gstart_ref),
                    list_ref, i, n_e, rc, tm, to_hbm=True)

    @pl.when(i == n - 1)
    def _():
        _wait_segments(s_ref.at[slot], xs_ref, sem.at[slot])

        @pl.when(n > 1)
        def _():
            _wait_segments(s_ref.at[1 - slot], xs_ref, sem.at[1 - slot])


def _dispatch(h1, lpos, tables, lists, n_rows, tm, n_e):
    t, d = h1.shape
    rc = d // 2 // LANES
    grid_spec = pltpu.PrefetchScalarGridSpec(
        num_scalar_prefetch=3,
        grid=(t // tm,),
        in_specs=[
            pl.BlockSpec((TOP_K * tm,), lambda i, *_: (i,), memory_space=pltpu.SMEM),
            pl.BlockSpec((LIST_ROWS * LANES,), lambda i, *_: (i,), memory_space=pltpu.SMEM),
            pl.BlockSpec((tm, d), lambda i, *_: (i, 0)),
        ],
        out_specs=pl.BlockSpec(memory_space=pl.ANY),
        scratch_shapes=[
            pltpu.VMEM((tm * rc, LANES), jnp.int32),
            pltpu.VMEM((2, TOP_K * tm * rc, LANES), jnp.int32),
            pltpu.SemaphoreType.DMA((2,)),
        ],
    )
    return pl.pallas_call(
        functools.partial(_dispatch_kernel, tm=tm, n_e=n_e, rc=rc),
        grid_spec=grid_spec,
        out_shape=jax.ShapeDtypeStruct((n_rows * rc, LANES), jnp.int32),
        compiler_params=_params("arbitrary"),
        name="dispatch",
    )(*tables, lpos, lists, h1)


def _expert_kernel(be_ref, nv_ref, nxt_ref, live_ref, xs_ref, wg_ref, wu_ref, wd_ref, ys_ref,
                   xbuf, ybuf, wg_f, wu_f, wd_f, wgu_s, wd_s, sem_x, sem_y, sem_w,
                   *, layer, blk, rc):
    n_live = live_ref[0]
    f = wd_s.shape[0]

    def x_copies(g, slot):
        r0 = pl.multiple_of(g * blk, blk)
        return [pltpu.make_async_copy(xs_ref.at[pl.ds(r0, blk), c, :], xbuf.at[slot, c],
                                      sem_x.at[slot]) for c in range(rc)]

    def y_copies(g, slot):
        r0 = pl.multiple_of(g * blk, blk)
        return [pltpu.make_async_copy(ybuf.at[slot, c], ys_ref.at[pl.ds(r0, blk), c, :],
                                      sem_y.at[slot]) for c in range(rc)]

    def w_copies(e, slot):
        return (pltpu.make_async_copy(wg_ref.at[layer, e], wg_f.at[slot], sem_w.at[slot]),
                pltpu.make_async_copy(wu_ref.at[layer, e], wu_f.at[slot], sem_w.at[slot]),
                pltpu.make_async_copy(wd_ref.at[layer, e], wd_f.at[slot], sem_w.at[slot]))

    def start(copies):
        for cp in copies:
            cp.start()

    def wait(copies):
        for cp in copies:
            cp.wait()

    start(x_copies(0, 0))
    start(w_copies(be_ref[0], 0))

    def block(g, wslot):
        slot = g % 2
        e = be_ref[g]
        fresh = (g == 0) | (e != be_ref[jnp.maximum(g - 1, 0)])

        @pl.when(fresh)
        def _():
            wait(w_copies(e, wslot))
            nxt = nxt_ref[g]

            @pl.when(nxt >= 0)
            def _():
                start(w_copies(nxt, 1 - wslot))

            wgu_s[:, :f] = wg_f[wslot].astype(BF16)
            wgu_s[:, f:] = wu_f[wslot].astype(BF16)
            wd_s[...] = wd_f[wslot].astype(BF16)

        wait(x_copies(g, slot))

        @pl.when(g + 1 < n_live)
        def _():
            start(x_copies(g + 1, 1 - slot))

        @pl.when(g >= 2)
        def _():
            wait(y_copies(g - 2, slot))

        nv = nv_ref[g]
        xin, yout = xbuf.at[slot], ybuf.at[slot]
        sub = blk // EXPERT_SUBBLOCKS
        gu, y = {}, {}

        def up(sb):
            x = _unpack_planes(xin, sub, sb * sub).astype(BF16)
            gu[sb] = _dot(x, wgu_s[...])

        def act(sb):
            mid = _silu(gu[sb][:, :f]) * gu[sb][:, f:]
            row = sb * sub + lax.broadcasted_iota(jnp.int32, (sub, 1), 0)
            gu[sb] = jnp.where(row < nv, mid, 0.0).astype(BF16)

        def down(sb):
            y[sb] = _dot(gu[sb], wd_s[...])

        def out(sb):
            _pack_planes(y[sb], yout, sb * sub)

        _skewed(EXPERT_SUBBLOCKS, up, act, down, out)
        start(y_copies(g, slot))
        return jnp.where(fresh, 1 - wslot, wslot)

    lax.fori_loop(0, n_live, block, 0)
    last = n_live - 1
    wait(y_copies(last, last % 2))

    @pl.when(n_live >= 2)
    def _():
        wait(y_copies(last - 1, (last - 1) % 2))


def _experts(xs, blk_e, blk_nv, blk_nxt, n_live, w_gate, w_up, w_down, layer, d):
    rc = d // 2 // LANES
    f = w_gate.shape[-1]
    n_rows = xs.shape[0] // rc
    anyspace = pl.BlockSpec(memory_space=pl.ANY)
    grid_spec = pltpu.PrefetchScalarGridSpec(
        num_scalar_prefetch=4,
        grid=(1,),
        in_specs=[anyspace] * 4,
        out_specs=anyspace,
        scratch_shapes=[
            pltpu.VMEM((2, rc, EXPERT_BLOCK, LANES), jnp.int32),
            pltpu.VMEM((2, rc, EXPERT_BLOCK, LANES), jnp.int32),
            pltpu.VMEM((2, d, f), F32),
            pltpu.VMEM((2, d, f), F32),
            pltpu.VMEM((2, f, d), F32),
            pltpu.VMEM((d, 2 * f), BF16),
            pltpu.VMEM((f, d), BF16),
            pltpu.SemaphoreType.DMA((2,)),
            pltpu.SemaphoreType.DMA((2,)),
            pltpu.SemaphoreType.DMA((2,)),
        ],
    )
    ys = pl.pallas_call(
        functools.partial(_expert_kernel, layer=layer, blk=EXPERT_BLOCK, rc=rc),
        grid_spec=grid_spec,
        out_shape=jax.ShapeDtypeStruct((n_rows, rc, LANES), jnp.int32),
        compiler_params=_params("arbitrary"),
        name="experts",
    )(blk_e, blk_nv, blk_nxt, n_live, xs.reshape(n_rows, rc, LANES), w_gate, w_up, w_down)
    return ys.reshape(xs.shape)


def _combine_kernel(cnt_ref, lstart_ref, gstart_ref, lpos_ref, gate_ref, list0_ref, list1_ref,
                    h_ref, shared_ref, g_ref, b_ref, ys_ref, o_ref, l_ref, r_ref, sem,
                    *, tm, n_e, rc, alpha):
    i = pl.program_id(0)
    n = pl.num_programs(0)
    slot = i % 2
    tables = (cnt_ref, lstart_ref, gstart_ref)

    @pl.when(i == 0)
    def _():
        _segment_copies(l_ref.at[0], ys_ref, sem.at[0], tables, list0_ref, 0, n_e, rc, tm,
                        to_hbm=False)

    @pl.when(i + 1 < n)
    def _():
        _segment_copies(l_ref.at[1 - slot], ys_ref, sem.at[1 - slot], tables, list1_ref, i + 1,
                        n_e, rc, tm, to_hbm=False)

    _wait_segments(l_ref.at[slot], ys_ref, sem.at[slot])

    def gather_from(sl):
        def gather(j, _):
            for u in range(GATHER_UNROLL):
                t = j * GATHER_UNROLL + u
                lo = hi = None
                for k in range(TOP_K):
                    src = pl.multiple_of(lpos_ref[t * TOP_K + k], rc)
                    w = l_ref[sl, pl.ds(src, rc), :]
                    gk = gate_ref[t * TOP_K + k]
                    lo = gk * _word_lo(w) if lo is None else lo + gk * _word_lo(w)
                    hi = gk * _word_hi(w) if hi is None else hi + gk * _word_hi(w)
                base = pl.multiple_of(t * 2 * rc, 2 * rc)
                r_ref[pl.ds(base, rc), :] = lo
                r_ref[pl.ds(base + rc, rc), :] = hi
            return 0

        lax.fori_loop(0, tm // GATHER_UNROLL, gather, 0)

    for sl in range(2):
        pl.when(slot == sl)(functools.partial(gather_from, sl))
    routed = jnp.concatenate(
        [r_ref[pl.ds(c, tm, stride=2 * rc), :] for c in range(2 * rc)], axis=-1)
    o_ref[...] = _ln(alpha * h_ref[...] + (routed + shared_ref[...]), g_ref[...], b_ref[...])


def _combine_ln(ys, lpos, gate, tables, lists, h1, shared, ln_g, ln_b, alpha, tm, n_e):
    t, d = h1.shape
    rc = d // 2 // LANES
    last = t // tm - 1
    vec = lambda a: a.reshape(1, d)
    grid_spec = pltpu.PrefetchScalarGridSpec(
        num_scalar_prefetch=3,
        grid=(t // tm,),
        in_specs=[
            pl.BlockSpec((TOP_K * tm,), lambda i, *_: (i,), memory_space=pltpu.SMEM),
            pl.BlockSpec((TOP_K * tm,), lambda i, *_: (i,), memory_space=pltpu.SMEM),
            pl.BlockSpec((LIST_ROWS * LANES,), lambda i, *_: (0,), memory_space=pltpu.SMEM),
            pl.BlockSpec((LIST_ROWS * LANES,), lambda i, *_: (jnp.minimum(i + 1, last),),
                         memory_space=pltpu.SMEM),
            pl.BlockSpec((tm, d), lambda i, *_: (i, 0)),
            pl.BlockSpec((tm, d), lambda i, *_: (i, 0)),
            pl.BlockSpec((1, d), lambda i, *_: (0, 0)),
            pl.BlockSpec((1, d), lambda i, *_: (0, 0)),
            pl.BlockSpec(memory_space=pl.ANY),
        ],
        out_specs=pl.BlockSpec((tm, d), lambda i, *_: (i, 0)),
        scratch_shapes=[
            pltpu.VMEM((2, TOP_K * tm * rc, LANES), jnp.int32),
            pltpu.VMEM((tm * 2 * rc, LANES), F32),
            pltpu.SemaphoreType.DMA((2,)),
        ],
    )
    return pl.pallas_call(
        functools.partial(_combine_kernel, tm=tm, n_e=n_e, rc=rc, alpha=alpha),
        grid_spec=grid_spec,
        out_shape=jax.ShapeDtypeStruct((t, d), F32),
        compiler_params=_params("arbitrary"),
        name="combine_ln",
    )(*tables, lpos, gate, lists, lists, h1, shared, vec(ln_g), vec(ln_b), ys)


def _ple_kernel(h_ref, p_ref, wp_ref, wg_ref, g_ref, b_ref, o_ref):
    sub = h_ref.shape[0] // ROW_SUBBLOCKS
    acc = {}

    def matmul(i):
        rows = pl.ds(i * sub, sub)
        e = _dot(p_ref[0, rows, :].astype(BF16), wp_ref[...])
        acc[i] = (e, _dot(h_ref[rows, :].astype(BF16), wg_ref[...]))

    def epilogue(i):
        rows = pl.ds(i * sub, sub)
        e, logit = acc[i]
        o_ref[rows, :] = h_ref[rows, :] + _ln(jax.nn.sigmoid(logit) * e, g_ref[...], b_ref[...])

    _skewed(ROW_SUBBLOCKS, matmul, epilogue)


def _ple(h2, p, layer, w_proj, w_gate, ln_g, ln_b, bm):
    t, d = h2.shape
    pd = p.shape[-1]
    vec = lambda a: a.reshape(1, d)
    return pl.pallas_call(
        _ple_kernel,
        grid=(t // bm,),
        in_specs=[
            pl.BlockSpec((bm, d), lambda i: (i, 0)),
            pl.BlockSpec((1, bm, pd), lambda i: (layer, i, 0)),
            pl.BlockSpec((pd, d), lambda i: (0, 0)),
            pl.BlockSpec((d, d), lambda i: (0, 0)),
            pl.BlockSpec((1, d), lambda i: (0, 0)),
            pl.BlockSpec((1, d), lambda i: (0, 0)),
        ],
        out_specs=pl.BlockSpec((bm, d), lambda i: (i, 0)),
        out_shape=jax.ShapeDtypeStruct((t, d), F32),
        compiler_params=_params("parallel"),
        name="ple",
    )(h2, p, w_proj, w_gate, vec(ln_g), vec(ln_b))


def _moe_tables(cnt, n_blocks, rc):
    n_e = cnt.shape[1]
    total = jnp.sum(cnt, axis=0)
    padded = (total + EXPERT_BLOCK - 1) // EXPERT_BLOCK * EXPERT_BLOCK
    ends = jnp.cumsum(padded)
    starts = ends - padded
    gstart = starts[None, :] + jnp.cumsum(cnt, axis=0) - cnt
    lstart = jnp.cumsum(cnt, axis=1) - cnt
    blk_start = jnp.arange(n_blocks, dtype=jnp.int32) * EXPERT_BLOCK
    blk_e = jnp.minimum(jnp.sum(ends[None, :] <= blk_start[:, None], axis=1), n_e - 1)
    onehot = blk_e[:, None] == jnp.arange(n_e)[None, :]
    used = jnp.sum(jnp.where(onehot, (starts + total)[None, :], 0), axis=1)
    blk_nv = jnp.clip(used - blk_start, 0, EXPERT_BLOCK)
    ids = jnp.arange(n_e)
    later = (ids[None, :] > ids[:, None]) & (total[None, :] > 0)
    nxt_e = jnp.min(jnp.where(later, ids[None, :], n_e), axis=1)
    nxt_e = jnp.where(nxt_e == n_e, -1, nxt_e)
    blk_nxt = jnp.sum(jnp.where(onehot, nxt_e[None, :], 0), axis=1)
    n_live = ends[-1:] // EXPERT_BLOCK
    i32 = lambda a: a.astype(jnp.int32)
    tables = (i32(cnt).reshape(-1), i32(lstart * rc).reshape(-1), i32(gstart * rc).reshape(-1))
    return tables, i32(blk_e), i32(blk_nv), i32(blk_nxt), i32(n_live)


def _moe_ln(h1, layer, w_router, b_router, w_gate, w_up, w_down, ws_gate, ws_up, ws_down,
            ln_g, ln_b, alpha, tm):
    t, d = h1.shape
    n_e = w_router.shape[1]
    n_blocks = t * TOP_K // EXPERT_BLOCK + n_e
    rc = d // 2 // LANES
    ws_gu = jnp.concatenate([ws_gate, ws_up], axis=1).astype(BF16)
    lpos, gate, cnt, shared, lists = _router(h1, w_router.T.astype(BF16), b_router, ws_gu,
                                             ws_down.astype(BF16), tm)
    tables, blk_e, blk_nv, blk_nxt, n_live = _moe_tables(cnt, n_blocks, rc)
    lpos = (lpos * rc).T.reshape(-1)
    gate = gate.T.reshape(-1)
    xs = _dispatch(h1, lpos, tables, lists, n_blocks * EXPERT_BLOCK, tm, n_e)
    ys = _experts(xs, blk_e, blk_nv, blk_nxt, n_live, w_gate, w_up, w_down, layer, d)
    return _combine_ln(ys, lpos, gate, tables, lists, h1, shared, ln_g, ln_b, alpha, tm, n_e)


def _tiles(t, s):
    return {"mm_m": min(1024, t), "mm_n": 512, "glu_n": 1024, "proj_m": min(512, t),
            "seq": min(512, s),
            "moe": min(256, t)}


def kernel(x, p, a_w_in, a_b_in, a_w_dw, a_b_dw, a_ln_g, a_ln_b, a_w_out, a_b_out, b_w_in, b_w_conv, b_w_out, c_w_grp, c_scale, ln1_g, ln1_b, ln2_g, ln2_b, router_w, router_b, exp_w_gate, exp_w_up, exp_w_down, sh_w_gate, sh_w_up, sh_w_down, ple_w_proj, ple_w_gate, ple_ln_g, ple_ln_b):
    bsz, s, d = x.shape
    depth = ln1_g.shape[0]
    t = bsz * s
    assert (t * TOP_K) % EXPERT_BLOCK == 0 and d % (2 * LANES) == 0
    alpha = (2 * depth) ** 0.25
    tl = _tiles(t, s)
    bn = min(tl["mm_n"], d)
    h = x.reshape(t, d)
    p_rows = p.reshape(depth, t, p.shape[-1])
    for i in range(depth):
        kind, j = i % 3, i // 3
        if kind == 0:
            u = _glu_proj(h, a_w_in[j].astype(BF16), a_b_in[j], tl["mm_m"], min(tl["glu_n"], d))
            v = _conv_ln_silu(u.reshape(bsz, s, d), a_w_dw[j], a_b_dw[j], a_ln_g[j], a_ln_b[j],
                              tl["seq"])
            h1 = _proj_residual_ln(v.reshape(t, d), a_w_out[j].astype(BF16), a_b_out[j], h,
                                   ln1_g[i], ln1_b[i], alpha, tl["proj_m"])
        elif kind == 1:
            bg, cv = _bcv_proj(h, b_w_in[j].astype(BF16), tl["mm_m"], bn)
            v = _gated_short_conv(bg.reshape(bsz, s, d), cv.reshape(bsz, s, d), b_w_conv[j],
                                  tl["seq"])
            h1 = _proj_residual_ln(v.reshape(t, d), b_w_out[j].astype(BF16),
                                   jnp.zeros((d,), F32), h, ln1_g[i], ln1_b[i], alpha,
                                   tl["proj_m"])
        else:
            h1 = _pool_mixer_ln(h.reshape(bsz, s, d), c_w_grp[j].astype(BF16), c_scale[j],
                                ln1_g[i], ln1_b[i], alpha, tl["seq"]).reshape(t, d)
        h2 = _moe_ln(h1, i, router_w[i], router_b[i], exp_w_gate, exp_w_up, exp_w_down,
                     sh_w_gate[i], sh_w_up[i], sh_w_down[i], ln2_g[i], ln2_b[i], alpha,
                     tl["moe"])
        h = _ple(h2, p_rows, i, ple_w_proj[i].astype(BF16), ple_w_gate[i].astype(BF16),
                 ple_ln_g[i], ple_ln_b[i], tl["proj_m"])
    return h.reshape(bsz, s, d)
```

```python
import functools

import jax
import jax.numpy as jnp
from jax import lax
from jax.experimental import pallas as pl
from jax.experimental.pallas import tpu as pltpu

LN_EPS = 1e-5
TOP_K = 8
N_EXPERT_GROUPS = 8
TOPK_GROUPS = 4
ROUTED_SCALE = 2.5
POOL_WINDOWS = (2, 4, 8, 16)
EXPERT_BLOCK = 512
EXPERT_SUBBLOCKS = 4
ROW_SUBBLOCKS = 4
PLACE_UNROLL = 4
GATHER_UNROLL = 8
LIST_ROWS = 16
LANES = 128
SUBLANES = 8
VMEM_LIMIT = 56 * 1024 * 1024

BF16 = jnp.bfloat16
PACKED = jnp.bfloat16
F32 = jnp.float32


def _params(*sem):
    return pltpu.CompilerParams(dimension_semantics=sem, vmem_limit_bytes=VMEM_LIMIT)


def _ln(x, g, b):
    mu = jnp.mean(x, axis=-1, keepdims=True)
    xc = x - mu
    var = jnp.mean(xc * xc, axis=-1, keepdims=True)
    return xc * lax.rsqrt(var + LN_EPS) * g + b


def _dot(a, b):
    return jnp.dot(a, b, preferred_element_type=F32)


def _silu(x):
    return x * jax.nn.sigmoid(x)


def _glu_kernel(x_ref, wa_ref, wg_ref, ba_ref, bg_ref, o_ref):
    x = x_ref[...].astype(BF16)
    a = _dot(x, wa_ref[...]) + ba_ref[...]
    g = _dot(x, wg_ref[...]) + bg_ref[...]
    o_ref[...] = a * jax.nn.sigmoid(g)


def _glu_proj(h, w_in, b_in, bm, bn):
    t, d = h.shape
    nb = d // bn
    b2 = b_in.reshape(1, 2 * d)
    return pl.pallas_call(
        _glu_kernel,
        grid=(t // bm, nb),
        in_specs=[
            pl.BlockSpec((bm, d), lambda i, j: (i, 0)),
            pl.BlockSpec((d, bn), lambda i, j: (0, j)),
            pl.BlockSpec((d, bn), lambda i, j: (0, j + nb)),
            pl.BlockSpec((1, bn), lambda i, j: (0, j)),
            pl.BlockSpec((1, bn), lambda i, j: (0, j + nb)),
        ],
        out_specs=pl.BlockSpec((bm, bn), lambda i, j: (i, j)),
        out_shape=jax.ShapeDtypeStruct((t, d), F32),
        compiler_params=_params("parallel", "arbitrary"),
        name="glu_proj",
    )(h, w_in, w_in, b2, b2)


def _bcv_kernel(x_ref, wb_ref, wc_ref, wv_ref, b_ref, cv_ref):
    x = x_ref[...].astype(BF16)
    b_ref[...] = _dot(x, wb_ref[...])
    cv_ref[...] = _dot(x, wc_ref[...]) * _dot(x, wv_ref[...])


def _bcv_proj(h, w_in, bm, bn):
    t, d = h.shape
    nb = d // bn
    return pl.pallas_call(
        _bcv_kernel,
        grid=(t // bm, nb),
        in_specs=[
            pl.BlockSpec((bm, d), lambda i, j: (i, 0)),
            pl.BlockSpec((d, bn), lambda i, j: (0, j)),
            pl.BlockSpec((d, bn), lambda i, j: (0, j + nb)),
            pl.BlockSpec((d, bn), lambda i, j: (0, j + 2 * nb)),
        ],
        out_specs=[pl.BlockSpec((bm, bn), lambda i, j: (i, j))] * 2,
        out_shape=[jax.ShapeDtypeStruct((t, d), F32)] * 2,
        compiler_params=_params("parallel", "arbitrary"),
        name="bcv_proj",
    )(h, w_in, w_in, w_in)


def _fill_window(buf_ref, halo_ref, main_ref, halo):
    first = pl.program_id(1) == 0
    buf_ref[0:halo, :] = jnp.where(first, 0.0, halo_ref[0])
    buf_ref[halo:, :] = main_ref[0]


def _conv_ln_kernel(u_ref, halo_ref, w_ref, bdw_ref, g_ref, b_ref, o_ref, buf_ref, acc_ref,
                    *, width, halo, ts, rows):
    _fill_window(buf_ref, halo_ref, u_ref, halo)
    d = buf_ref.shape[1]
    off = halo - (width - 1)

    def chunk(r, _):
        r0 = pl.multiple_of(r * rows, rows)
        for c in range(d // LANES):
            cs = slice(c * LANES, (c + 1) * LANES)
            win = buf_ref[pl.ds(r0, rows + halo), cs]
            acc = jnp.zeros((rows, LANES), F32)
            for s in range(SUBLANES):
                taps = [k for k in range(width) if (off + k) % SUBLANES == s]
                if not taps:
                    continue
                ws = pltpu.roll(win, rows + halo - s, axis=0) if s else win
                for k in taps:
                    q = off + k - s
                    acc = acc + w_ref[k:k + 1, cs] * ws[q:q + rows]
            acc_ref[pl.ds(r0, rows), cs] = acc
        return 0

    lax.fori_loop(0, ts // rows, chunk, 0)
    v = _ln(acc_ref[...] + bdw_ref[...], g_ref[...], b_ref[...])
    o_ref[0] = _silu(v).astype(o_ref.dtype)


def _conv_ln_silu(u, w_dw, b_dw, ln_g, ln_b, ts):
    bsz, s, d = u.shape
    width = w_dw.shape[0]
    halo = -(-(width - 1) // SUBLANES) * SUBLANES
    hb = ts // halo
    kern = functools.partial(_conv_ln_kernel, width=width, halo=halo, ts=ts, rows=32)
    vec = lambda a: a.reshape(1, d)
    return pl.pallas_call(
        kern,
        grid=(bsz, s // ts),
        in_specs=[
            pl.BlockSpec((1, ts, d), lambda b, i: (b, i, 0)),
            pl.BlockSpec((1, halo, d), lambda b, i: (b, jnp.maximum(i * hb - 1, 0), 0)),
            pl.BlockSpec((width, d), lambda b, i: (0, 0)),
            pl.BlockSpec((1, d), lambda b, i: (0, 0)),
            pl.BlockSpec((1, d), lambda b, i: (0, 0)),
            pl.BlockSpec((1, d), lambda b, i: (0, 0)),
        ],
        out_specs=pl.BlockSpec((1, ts, d), lambda b, i: (b, i, 0)),
        out_shape=jax.ShapeDtypeStruct((bsz, s, d), BF16),
        scratch_shapes=[pltpu.VMEM((halo + ts, d), F32), pltpu.VMEM((ts, d), F32)],
        compiler_params=_params("parallel", "arbitrary"),
        name="conv_ln_silu",
    )(u, u, w_dw, vec(b_dw), vec(ln_g), vec(ln_b))


def _gated_conv_kernel(cv_ref, halo_ref, bg_ref, w_ref, o_ref, buf_ref, *, width, halo, ts):
    _fill_window(buf_ref, halo_ref, cv_ref, halo)
    full = buf_ref[...]
    acc = None
    for k in range(width):
        back = width - 1 - k
        rows = (pltpu.roll(full, back, axis=0) if back else full)[halo:, :]
        term = w_ref[k:k + 1, :] * rows
        acc = term if acc is None else acc + term
    o_ref[0] = (bg_ref[0] * acc).astype(o_ref.dtype)


def _gated_short_conv(bg, cv, w_conv, ts):
    bsz, s, d = cv.shape
    width = w_conv.shape[0]
    halo = SUBLANES
    hb = ts // halo
    kern = functools.partial(_gated_conv_kernel, width=width, halo=halo, ts=ts)
    return pl.pallas_call(
        kern,
        grid=(bsz, s // ts),
        in_specs=[
            pl.BlockSpec((1, ts, d), lambda b, i: (b, i, 0)),
            pl.BlockSpec((1, halo, d), lambda b, i: (b, jnp.maximum(i * hb - 1, 0), 0)),
            pl.BlockSpec((1, ts, d), lambda b, i: (b, i, 0)),
            pl.BlockSpec((width, d), lambda b, i: (0, 0)),
        ],
        out_specs=pl.BlockSpec((1, ts, d), lambda b, i: (b, i, 0)),
        out_shape=jax.ShapeDtypeStruct((bsz, s, d), BF16),
        scratch_shapes=[pltpu.VMEM((halo + ts, d), F32)],
        compiler_params=_params("parallel", "arbitrary"),
        name="gated_short_conv",
    )(cv, cv, bg, w_conv)


def _skewed(n_sub, *stages):
    for step in range(n_sub + len(stages) - 1):
        for lag, stage in enumerate(stages):
            if 0 <= step - lag < n_sub:
                stage(step - lag)


def _proj_ln_kernel(v_ref, w_ref, bias_ref, h_ref, g_ref, b_ref, o_ref, *, alpha):
    sub = v_ref.shape[0] // ROW_SUBBLOCKS
    acc = {}

    def matmul(i):
        acc[i] = _dot(v_ref[pl.ds(i * sub, sub), :], w_ref[...])

    def epilogue(i):
        rows = pl.ds(i * sub, sub)
        m = acc[i] + bias_ref[...]
        o_ref[rows, :] = _ln(alpha * h_ref[rows, :] + m, g_ref[...], b_ref[...])

    _skewed(ROW_SUBBLOCKS, matmul, epilogue)


def _proj_residual_ln(v, w_out, b_out, h, ln_g, ln_b, alpha, bm):
    t, d = h.shape
    vec = lambda a: a.reshape(1, d)
    return pl.pallas_call(
        functools.partial(_proj_ln_kernel, alpha=alpha),
        grid=(t // bm,),
        in_specs=[
            pl.BlockSpec((bm, d), lambda i: (i, 0)),
            pl.BlockSpec((d, d), lambda i: (0, 0)),
            pl.BlockSpec((1, d), lambda i: (0, 0)),
            pl.BlockSpec((bm, d), lambda i: (i, 0)),
            pl.BlockSpec((1, d), lambda i: (0, 0)),
            pl.BlockSpec((1, d), lambda i: (0, 0)),
        ],
        out_specs=pl.BlockSpec((bm, d), lambda i: (i, 0)),
        out_shape=jax.ShapeDtypeStruct((t, d), F32),
        compiler_params=_params("parallel"),
        name="proj_residual_ln",
    )(v, w_out, vec(b_out), h, vec(ln_g), vec(ln_b))


def _pool_kernel(h_ref, halo_ref, w_ref, scale_ref, g_ref, b_ref, o_ref, buf_ref,
                 *, halo, ts, alpha):
    _fill_window(buf_ref, halo_ref, h_ref, halo)
    d = buf_ref.shape[1]
    gd = d // len(POOL_WINDOWS)
    pos = pl.program_id(1) * ts + lax.broadcasted_iota(jnp.int32, (ts, 1), 0) + 1
    ys = []
    for gi, win in enumerate(POOL_WINDOWS):
        cs = slice(gi * gd, (gi + 1) * gd)
        full = buf_ref[:, cs]
        tot, span = full, 1
        while span < win:
            tot = tot + pltpu.roll(tot, span, axis=0)
            span *= 2
        x, tot = full[halo:, :], tot[halo:, :]
        cnt = jnp.minimum(pos, win).astype(F32)
        z = tot / cnt - x
        ys.append(_dot(z.astype(BF16), w_ref[gi]))
    y = jnp.concatenate(ys, axis=-1) * scale_ref[...]
    o_ref[0] = _ln(alpha * h_ref[0] + y, g_ref[...], b_ref[...])


def _pool_mixer_ln(h, w_grp, scale, ln_g, ln_b, alpha, ts):
    bsz, s, d = h.shape
    halo = max(POOL_WINDOWS)
    hb = ts // halo
    vec = lambda a: a.reshape(1, d)
    return pl.pallas_call(
        functools.partial(_pool_kernel, halo=halo, ts=ts, alpha=alpha),
        grid=(bsz, s // ts),
        in_specs=[
            pl.BlockSpec((1, ts, d), lambda b, i: (b, i, 0)),
            pl.BlockSpec((1, halo, d), lambda b, i: (b, jnp.maximum(i * hb - 1, 0), 0)),
            pl.BlockSpec(w_grp.shape, lambda b, i: (0, 0, 0)),
            pl.BlockSpec((1, d), lambda b, i: (0, 0)),
            pl.BlockSpec((1, d), lambda b, i: (0, 0)),
            pl.BlockSpec((1, d), lambda b, i: (0, 0)),
        ],
        out_specs=pl.BlockSpec((1, ts, d), lambda b, i: (b, i, 0)),
        out_shape=jax.ShapeDtypeStruct((bsz, s, d), F32),
        scratch_shapes=[pltpu.VMEM((halo + ts, d), F32)],
        compiler_params=_params("parallel", "arbitrary"),
        name="pool_mixer_ln",
    )(h, h, w_grp, vec(scale), vec(ln_g), vec(ln_b))


def _rank_desc(m, side_work=()):
    n = m.shape[0]
    row = lax.broadcasted_iota(jnp.int32, m.shape, 0)
    rank = jnp.zeros(m.shape, jnp.int32)
    every = n // len(side_work) if side_work else 0
    for j in range(n):
        if side_work and j % every == 0 and j // every < len(side_work):
            side_work[j // every]()
        mj = m[j:j + 1, :]
        beats = (mj > m) | ((mj == m) & (row > j))
        rank = rank + beats.astype(jnp.int32)
    return rank


def _router_kernel(h_ref, wt_ref, b_ref, wsgu_ref, wsd_ref, lpos_ref, gate_ref, cnt_ref,
                   shared_ref, list_ref, *, tm):
    n_e = wt_ref.shape[0]
    epg = n_e // N_EXPERT_GROUPS
    x = h_ref[...].astype(BF16)
    f = wsd_ref.shape[0]
    d = wsd_ref.shape[1]
    mid = []

    def shared_up():
        su = _dot(x, wsgu_ref[...])
        mid.append((_silu(su[:, :f]) * su[:, f:]).astype(BF16))

    def shared_down(c, parts=4):
        def run():
            cs = slice(c * d // parts, (c + 1) * d // parts)
            shared_ref[:, cs] = _dot(mid[0], wsd_ref[:, cs])
        return run

    side_work = [shared_up] + [shared_down(c) for c in range(4)]
    logits = lax.dot_general(wt_ref[...], x, (((1,), (1,)), ((), ())),
                             preferred_element_type=F32)
    scores = jax.nn.sigmoid(logits)
    biased = scores + b_ref[...]

    sub = lax.broadcasted_iota(jnp.int32, (epg, tm), 0)
    gscore = []
    for g in range(N_EXPERT_GROUPS):
        blk = biased[g * epg:(g + 1) * epg, :]
        m1 = jnp.max(blk, axis=0, keepdims=True)
        first = jnp.min(jnp.where(blk == m1, sub, epg), axis=0, keepdims=True)
        m2 = jnp.max(jnp.where(sub == first, -jnp.inf, blk), axis=0, keepdims=True)
        gscore.append(m1 + m2)
    gsel = _rank_desc(jnp.concatenate(gscore, axis=0)) < TOPK_GROUPS
    emask = jnp.concatenate(
        [jnp.broadcast_to(gsel[g:g + 1, :], (epg, tm)) for g in range(N_EXPERT_GROUPS)], axis=0)
    masked = jnp.where(emask, biased, -jnp.inf)
    sel = _rank_desc(masked, side_work) < TOP_K
    self32 = sel.astype(F32)
    selb = self32.astype(BF16)

    gate = jnp.where(sel, scores, 0.0)
    gate = gate / jnp.sum(gate, axis=0, keepdims=True) * ROUTED_SCALE

    er = lax.broadcasted_iota(jnp.int32, (n_e, n_e), 0)
    ec = lax.broadcasted_iota(jnp.int32, (n_e, n_e), 1)
    lower = (ec < er).astype(BF16)
    slot = _dot(lower, selb)
    tr = lax.broadcasted_iota(jnp.int32, (tm, tm), 0)
    tc = lax.broadcasted_iota(jnp.int32, (tm, tm), 1)
    before = (tr < tc).astype(BF16)
    lrank = _dot(selb, before)
    count = jnp.sum(self32, axis=1, keepdims=True)
    count_l = jnp.broadcast_to(count, (n_e, LANES))
    lstart = _dot(lower, count_l.astype(BF16))[:, 0:1]
    lpos = lstart + lrank

    lpos_rows, gate_rows = [], []
    for k in range(TOP_K):
        pick = sel & (slot == float(k))
        lpos_rows.append(jnp.sum(jnp.where(pick, lpos, 0.0), axis=0, keepdims=True))
        gate_rows.append(jnp.sum(jnp.where(pick, gate, 0.0), axis=0, keepdims=True))
    lpos_ref[...] = jnp.concatenate(lpos_rows, axis=0).astype(jnp.int32)
    gate_ref[...] = jnp.concatenate(gate_rows, axis=0)
    cnt_ref[0] = count_l.astype(jnp.int32)

    n_bits = tm.bit_length()
    counts_i = count_l.astype(jnp.int32)
    lane = lax.broadcasted_iota(jnp.int32, (n_e, LANES), 1)
    erow_l = lax.broadcasted_iota(jnp.int32, (n_e, LANES), 0)
    lane_row = lax.broadcasted_iota(jnp.int32, (1, LANES), 1)
    rows, lengths = [], jnp.zeros((1, LANES), jnp.int32)
    for b in range(n_bits):
        bit = lax.shift_right_logical(counts_i, b) & 1
        below = _dot(lower, bit.astype(BF16)).astype(jnp.int32)
        hit = (bit == 1) & (below == lane)
        rows.append(jnp.sum(jnp.where(hit, erow_l, 0), axis=0, keepdims=True))
        total_b = jnp.sum(bit[:, 0:1], axis=0, keepdims=True)
        lengths = lengths + jnp.where(lane_row == b, total_b, 0)
    rows.append(lengths)
    rows.append(jnp.zeros((LIST_ROWS - n_bits - 1, LANES), jnp.int32))
    list_ref[0] = jnp.concatenate(rows, axis=0)


def _router(h1, w_router_t, b_router, ws_gu, ws_down, tm):
    t, d = h1.shape
    n_e = w_router_t.shape[0]
    f = ws_down.shape[0]
    assert tm <= 256, "tile counts must stay exactly representable in bf16"
    assert tm.bit_length() < LIST_ROWS and n_e <= LANES
    lpos, gate, cnt, shared, lists = pl.pallas_call(
        functools.partial(_router_kernel, tm=tm),
        grid=(t // tm,),
        in_specs=[
            pl.BlockSpec((tm, d), lambda i: (i, 0)),
            pl.BlockSpec((n_e, d), lambda i: (0, 0)),
            pl.BlockSpec((n_e, 1), lambda i: (0, 0)),
            pl.BlockSpec((d, 2 * f), lambda i: (0, 0)),
            pl.BlockSpec((f, d), lambda i: (0, 0)),
        ],
        out_specs=[
            pl.BlockSpec((TOP_K, tm), lambda i: (0, i)),
            pl.BlockSpec((TOP_K, tm), lambda i: (0, i)),
            pl.BlockSpec((1, n_e, LANES), lambda i: (i, 0, 0)),
            pl.BlockSpec((tm, d), lambda i: (i, 0)),
            pl.BlockSpec((1, LIST_ROWS, LANES), lambda i: (i, 0, 0)),
        ],
        out_shape=[
            jax.ShapeDtypeStruct((TOP_K, t), jnp.int32),
            jax.ShapeDtypeStruct((TOP_K, t), F32),
            jax.ShapeDtypeStruct((t // tm, n_e, LANES), jnp.int32),
            jax.ShapeDtypeStruct((t, d), F32),
            jax.ShapeDtypeStruct((t // tm, LIST_ROWS, LANES), jnp.int32),
        ],
        compiler_params=_params("parallel"),
        name="router",
    )(h1, w_router_t, b_router.reshape(n_e, 1), ws_gu, ws_down)
    return lpos, gate, cnt[:, :, 0], shared, lists.reshape(-1)


HIGH_HALF = -65536


def _pack_words(lo, hi):
    bits = lambda v: lax.bitcast_convert_type(v.astype(PACKED).astype(F32), jnp.int32)
    return (bits(hi) & HIGH_HALF) | lax.shift_right_logical(bits(lo), 16)


def _word_lo(w):
    return lax.bitcast_convert_type(lax.shift_left(w, 16), F32)


def _word_hi(w):
    return lax.bitcast_convert_type(w & HIGH_HALF, F32)


def _pack_rows(x, dst_ref, r0=0):
    n, half = x.shape[0], x.shape[1] // 2
    rc = half // LANES
    words = _pack_words(x[:, :half], x[:, half:])
    for c in range(rc):
        dst_ref[pl.ds(r0 * rc + c, n, stride=rc), :] = words[:, c * LANES:(c + 1) * LANES]


def _unpack_rows(src_ref, n, rc, r0=0):
    lo, hi = [], []
    for c in range(rc):
        w = src_ref[pl.ds(r0 * rc + c, n, stride=rc), :]
        lo.append(_word_lo(w))
        hi.append(_word_hi(w))
    return jnp.concatenate(lo + hi, axis=-1)


def _pack_planes(x, dst_ref, r0):
    n, half = x.shape[0], x.shape[1] // 2
    words = _pack_words(x[:, :half], x[:, half:])
    for c in range(half // LANES):
        dst_ref[c, pl.ds(r0, n), :] = words[:, c * LANES:(c + 1) * LANES]


def _unpack_planes(src_ref, n, r0):
    lo, hi = [], []
    for c in range(src_ref.shape[0]):
        w = src_ref[c, pl.ds(r0, n), :]
        lo.append(_word_lo(w))
        hi.append(_word_hi(w))
    return jnp.concatenate(lo + hi, axis=-1)


def _segment_copies(local_ref, hbm_ref, sem, tables, list_ref, tile, n_e, rc, tm, to_hbm):
    cnt_ref, lstart_ref, gstart_ref = tables
    n_bits = tm.bit_length()
    for b in range(n_bits):
        size = 1 << b

        def copy_one(j, _, b=b, size=size):
            idx = tile * n_e + list_ref[b * LANES + j]
            n, ls, gs = cnt_ref[idx], lstart_ref[idx], gstart_ref[idx]
            done = (n & ~(2 * size - 1)) * rc
            lo = pl.multiple_of(ls + done, rc)
            go = pl.multiple_of(gs + done, rc)
            loc = local_ref.at[pl.ds(lo, size * rc), :]
            glob = hbm_ref.at[pl.ds(go, size * rc), :]
            if to_hbm:
                pltpu.make_async_copy(loc, glob, sem).start()
            else:
                pltpu.make_async_copy(glob, loc, sem).start()
            return 0

        lax.fori_loop(0, list_ref[n_bits * LANES + b], copy_one, 0)


def _wait_segments(local_ref, hbm_ref, sem):
    rows = local_ref.shape[0]
    pltpu.make_async_copy(local_ref, hbm_ref.at[pl.ds(0, rows), :], sem).wait()


def _dispatch_kernel(cnt_ref, lstart_ref, gstart_ref, lpos_ref, list_ref, h_ref, xs_ref,
                     q_ref, s_ref, sem, *, tm, n_e, rc):
    i = pl.program_id(0)
    n = pl.num_programs(0)
    slot = i % 2

    @pl.when(i >= 2)
    def _():
        _wait_segments(s_ref.at[slot], xs_ref, sem.at[slot])

    _pack_rows(h_ref[...], q_ref)

    def place_into(sl):
        def place(j, _):
            for u in range(PLACE_UNROLL):
                t = j * PLACE_UNROLL + u
                row = q_ref[pl.ds(pl.multiple_of(t * rc, rc), rc), :]
                for k in range(TOP_K):
                    dst = pl.multiple_of(lpos_ref[t * TOP_K + k], rc)
                    s_ref[sl, pl.ds(dst, rc), :] = row
            return 0

        lax.fori_loop(0, tm // PLACE_UNROLL, place, 0)

    for sl in range(2):
        pl.when(slot == sl)(functools.partial(place_into, sl))
    _segment_copies(s_ref.at[slot], xs_ref, sem.at[slot], (cnt_ref, lstart_ref, gstart_ref),
                    list_ref, i, n_e, rc, tm, to_hbm=True)

    @pl.when(i == n - 1)
    def _():
        _wait_segments(s_ref.at[slot], xs_ref, sem.at[slot])

        @pl.when(n > 1)
        def _():
            _wait_segments(s_ref.at[1 - slot], xs_ref, sem.at[1 - slot])


def _dispatch(h1, lpos, tables, lists, n_rows, tm, n_e):
    t, d = h1.shape
    rc = d // 2 // LANES
    grid_spec = pltpu.PrefetchScalarGridSpec(
        num_scalar_prefetch=3,
        grid=(t // tm,),
        in_specs=[
            pl.BlockSpec((TOP_K * tm,), lambda i, *_: (i,), memory_space=pltpu.SMEM),
            pl.BlockSpec((LIST_ROWS * LANES,), lambda i, *_: (i,), memory_space=pltpu.SMEM),
            pl.BlockSpec((tm, d), lambda i, *_: (i, 0)),
        ],
        out_specs=pl.BlockSpec(memory_space=pl.ANY),
        scratch_shapes=[
            pltpu.VMEM((tm * rc, LANES), jnp.int32),
            pltpu.VMEM((2, TOP_K * tm * rc, LANES), jnp.int32),
            pltpu.SemaphoreType.DMA((2,)),
        ],
    )
    return pl.pallas_call(
        functools.partial(_dispatch_kernel, tm=tm, n_e=n_e, rc=rc),
        grid_spec=grid_spec,
        out_shape=jax.ShapeDtypeStruct((n_rows * rc, LANES), jnp.int32),
        compiler_params=_params("arbitrary"),
        name="dispatch",
    )(*tables, lpos, lists, h1)


def _expert_kernel(be_ref, nv_ref, nxt_ref, live_ref, xs_ref, wg_ref, wu_ref, wd_ref, ys_ref,
                   xbuf, ybuf, wg_f, wu_f, wd_f, wgu_s, wd_s, sem_x, sem_y, sem_w,
                   *, layer, blk, rc):
    n_live = live_ref[0]
    f = wd_s.shape[0]

    def x_copies(g, slot):
        r0 = pl.multiple_of(g * blk, blk)
        return [pltpu.make_async_copy(xs_ref.at[pl.ds(r0, blk), c, :], xbuf.at[slot, c],
                                      sem_x.at[slot]) for c in range(rc)]

    def y_copies(g, slot):
        r0 = pl.multiple_of(g * blk, blk)
        return [pltpu.make_async_copy(ybuf.at[slot, c], ys_ref.at[pl.ds(r0, blk), c, :],
                                      sem_y.at[slot]) for c in range(rc)]

    def w_copies(e, slot):
        return (pltpu.make_async_copy(wg_ref.at[layer, e], wg_f.at[slot], sem_w.at[slot]),
                pltpu.make_async_copy(wu_ref.at[layer, e], wu_f.at[slot], sem_w.at[slot]),
                pltpu.make_async_copy(wd_ref.at[layer, e], wd_f.at[slot], sem_w.at[slot]))

    def start(copies):
        for cp in copies:
            cp.start()

    def wait(copies):
        for cp in copies:
            cp.wait()

    start(x_copies(0, 0))
    start(w_copies(be_ref[0], 0))

    def block(g, wslot):
        slot = g % 2
        e = be_ref[g]
        fresh = (g == 0) | (e != be_ref[jnp.maximum(g - 1, 0)])

        @pl.when(fresh)
        def _():
            wait(w_copies(e, wslot))
            nxt = nxt_ref[g]

            @pl.when(nxt >= 0)
            def _():
                start(w_copies(nxt, 1 - wslot))

            wgu_s[:, :f] = wg_f[wslot].astype(BF16)
            wgu_s[:, f:] = wu_f[wslot].astype(BF16)
            wd_s[...] = wd_f[wslot].astype(BF16)

        wait(x_copies(g, slot))

        @pl.when(g + 1 < n_live)
        def _():
            start(x_copies(g + 1, 1 - slot))

        @pl.when(g >= 2)
        def _():
            wait(y_copies(g - 2, slot))

        nv = nv_ref[g]
        xin, yout = xbuf.at[slot], ybuf.at[slot]
        sub = blk // EXPERT_SUBBLOCKS
        gu, y = {}, {}

        def up(sb):
            x = _unpack_planes(xin, sub, sb * sub).astype(BF16)
            gu[sb] = _dot(x, wgu_s[...])

        def act(sb):
            mid = _silu(gu[sb][:, :f]) * gu[sb][:, f:]
            row = sb * sub + lax.broadcasted_iota(jnp.int32, (sub, 1), 0)
            gu[sb] = jnp.where(row < nv, mid, 0.0).astype(BF16)

        def down(sb):
            y[sb] = _dot(gu[sb], wd_s[...])

        def out(sb):
            _pack_planes(y[sb], yout, sb * sub)

        _skewed(EXPERT_SUBBLOCKS, up, act, down, out)
        start(y_copies(g, slot))
        return jnp.where(fresh, 1 - wslot, wslot)

    lax.fori_loop(0, n_live, block, 0)
    last = n_live - 1
    wait(y_copies(last, last % 2))

    @pl.when(n_live >= 2)
    def _():
        wait(y_copies(last - 1, (last - 1) % 2))


def _experts(xs, blk_e, blk_nv, blk_nxt, n_live, w_gate, w_up, w_down, layer, d):
    rc = d // 2 // LANES
    f = w_gate.shape[-1]
    n_rows = xs.shape[0] // rc
    anyspace = pl.BlockSpec(memory_space=pl.ANY)
    grid_spec = pltpu.PrefetchScalarGridSpec(
        num_scalar_prefetch=4,
        grid=(1,),
        in_specs=[anyspace] * 4,
        out_specs=anyspace,
        scratch_shapes=[
            pltpu.VMEM((2, rc, EXPERT_BLOCK, LANES), jnp.int32),
            pltpu.VMEM((2, rc, EXPERT_BLOCK, LANES), jnp.int32),
            pltpu.VMEM((2, d, f), F32),
            pltpu.VMEM((2, d, f), F32),
            pltpu.VMEM((2, f, d), F32),
            pltpu.VMEM((d, 2 * f), BF16),
            pltpu.VMEM((f, d), BF16),
            pltpu.SemaphoreType.DMA((2,)),
            pltpu.SemaphoreType.DMA((2,)),
            pltpu.SemaphoreType.DMA((2,)),
        ],
    )
    ys = pl.pallas_call(
        functools.partial(_expert_kernel, layer=layer, blk=EXPERT_BLOCK, rc=rc),
        grid_spec=grid_spec,
        out_shape=jax.ShapeDtypeStruct((n_rows, rc, LANES), jnp.int32),
        compiler_params=_params("arbitrary"),
        name="experts",
    )(blk_e, blk_nv, blk_nxt, n_live, xs.reshape(n_rows, rc, LANES), w_gate, w_up, w_down)
    return ys.reshape(xs.shape)


def _combine_kernel(cnt_ref, lstart_ref, gstart_ref, lpos_ref, gate_ref, list0_ref, list1_ref,
                    h_ref, shared_ref, g_ref, b_ref, ys_ref, o_ref, l_ref, r_ref, sem,
                    *, tm, n_e, rc, alpha):
    i = pl.program_id(0)
    n = pl.num_programs(0)
    slot = i % 2
    tables = (cnt_ref, lstart_ref, gstart_ref)

    @pl.when(i == 0)
    def _():
        _segment_copies(l_ref.at[0], ys_ref, sem.at[0], tables, list0_ref, 0, n_e, rc, tm,
                        to_hbm=False)

    @pl.when(i + 1 < n)
    def _():
        _segment_copies(l_ref.at[1 - slot], ys_ref, sem.at[1 - slot], tables, list1_ref, i + 1,
                        n_e, rc, tm, to_hbm=False)

    _wait_segments(l_ref.at[slot], ys_ref, sem.at[slot])

    def gather_from(sl):
        def gather(j, _):
            for u in range(GATHER_UNROLL):
                t = j * GATHER_UNROLL + u
                lo = hi = None
                for k in range(TOP_K):
                    src = pl.multiple_of(lpos_ref[t * TOP_K + k], rc)
                    w = l_ref[sl, pl.ds(src, rc), :]
                    gk = gate_ref[t * TOP_K + k]
                    lo = gk * _word_lo(w) if lo is None else lo + gk * _word_lo(w)
                    hi = gk * _word_hi(w) if hi is None else hi + gk * _word_hi(w)
                base = pl.multiple_of(t * 2 * rc, 2 * rc)
                r_ref[pl.ds(base, rc), :] = lo
                r_ref[pl.ds(base + rc, rc), :] = hi
            return 0

        lax.fori_loop(0, tm // GATHER_UNROLL, gather, 0)

    for sl in range(2):
        pl.when(slot == sl)(functools.partial(gather_from, sl))
    routed = jnp.concatenate(
        [r_ref[pl.ds(c, tm, stride=2 * rc), :] for c in range(2 * rc)], axis=-1)
    o_ref[...] = _ln(alpha * h_ref[...] + (routed + shared_ref[...]), g_ref[...], b_ref[...])


def _combine_ln(ys, lpos, gate, tables, lists, h1, shared, ln_g, ln_b, alpha, tm, n_e):
    t, d = h1.shape
    rc = d // 2 // LANES
    last = t // tm - 1
    vec = lambda a: a.reshape(1, d)
    grid_spec = pltpu.PrefetchScalarGridSpec(
        num_scalar_prefetch=3,
        grid=(t // tm,),
        in_specs=[
            pl.BlockSpec((TOP_K * tm,), lambda i, *_: (i,), memory_space=pltpu.SMEM),
            pl.BlockSpec((TOP_K * tm,), lambda i, *_: (i,), memory_space=pltpu.SMEM),
            pl.BlockSpec((LIST_ROWS * LANES,), lambda i, *_: (0,), memory_space=pltpu.SMEM),
            pl.BlockSpec((LIST_ROWS * LANES,), lambda i, *_: (jnp.minimum(i + 1, last),),
                         memory_space=pltpu.SMEM),
            pl.BlockSpec((tm, d), lambda i, *_: (i, 0)),
            pl.BlockSpec((tm, d), lambda i, *_: (i, 0)),
            pl.BlockSpec((1, d), lambda i, *_: (0, 0)),
            pl.BlockSpec((1, d), lambda i, *_: (0, 0)),
            pl.BlockSpec(memory_space=pl.ANY),
        ],
        out_specs=pl.BlockSpec((tm, d), lambda i, *_: (i, 0)),
        scratch_shapes=[
            pltpu.VMEM((2, TOP_K * tm * rc, LANES), jnp.int32),
            pltpu.VMEM((tm * 2 * rc, LANES), F32),
            pltpu.SemaphoreType.DMA((2,)),
        ],
    )
    return pl.pallas_call(
        functools.partial(_combine_kernel, tm=tm, n_e=n_e, rc=rc, alpha=alpha),
        grid_spec=grid_spec,
        out_shape=jax.ShapeDtypeStruct((t, d), F32),
        compiler_params=_params("arbitrary"),
        name="combine_ln",
    )(*tables, lpos, gate, lists, lists, h1, shared, vec(ln_g), vec(ln_b), ys)


def _ple_kernel(h_ref, p_ref, wp_ref, wg_ref, g_ref, b_ref, o_ref):
    sub = h_ref.shape[0] // ROW_SUBBLOCKS
    acc = {}

    def matmul(i):
        rows = pl.ds(i * sub, sub)
        e = _dot(p_ref[0, rows, :].astype(BF16), wp_ref[...])
        acc[i] = (e, _dot(h_ref[rows, :].astype(BF16), wg_ref[...]))

    def epilogue(i):
        rows = pl.ds(i * sub, sub)
        e, logit = acc[i]
        o_ref[rows, :] = h_ref[rows, :] + _ln(jax.nn.sigmoid(logit) * e, g_ref[...], b_ref[...])

    _skewed(ROW_SUBBLOCKS, matmul, epilogue)


def _ple(h2, p, layer, w_proj, w_gate, ln_g, ln_b, bm):
    t, d = h2.shape
    pd = p.shape[-1]
    vec = lambda a: a.reshape(1, d)
    return pl.pallas_call(
        _ple_kernel,
        grid=(t // bm,),
        in_specs=[
            pl.BlockSpec((bm, d), lambda i: (i, 0)),
            pl.BlockSpec((1, bm, pd), lambda i: (layer, i, 0)),
            pl.BlockSpec((pd, d), lambda i: (0, 0)),
            pl.BlockSpec((d, d), lambda i: (0, 0)),
            pl.BlockSpec((1, d), lambda i: (0, 0)),
            pl.BlockSpec((1, d), lambda i: (0, 0)),
        ],
        out_specs=pl.BlockSpec((bm, d), lambda i: (i, 0)),
        out_shape=jax.ShapeDtypeStruct((t, d), F32),
        compiler_params=_params("parallel"),
        name="ple",
    )(h2, p, w_proj, w_gate, vec(ln_g), vec(ln_b))


def _moe_tables(cnt, n_blocks, rc):
    n_e = cnt.shape[1]
    total = jnp.sum(cnt, axis=0)
    padded = (total + EXPERT_BLOCK - 1) // EXPERT_BLOCK * EXPERT_BLOCK
    ends = jnp.cumsum(padded)
    starts = ends - padded
    gstart = starts[None, :] + jnp.cumsum(cnt, axis=0) - cnt
    lstart = jnp.cumsum(cnt, axis=1) - cnt
    blk_start = jnp.arange(n_blocks, dtype=jnp.int32) * EXPERT_BLOCK
    blk_e = jnp.minimum(jnp.sum(ends[None, :] <= blk_start[:, None], axis=1), n_e - 1)
    onehot = blk_e[:, None] == jnp.arange(n_e)[None, :]
    used = jnp.sum(jnp.where(onehot, (starts + total)[None, :], 0), axis=1)
    blk_nv = jnp.clip(used - blk_start, 0, EXPERT_BLOCK)
    ids = jnp.arange(n_e)
    later = (ids[None, :] > ids[:, None]) & (total[None, :] > 0)
    nxt_e = jnp.min(jnp.where(later, ids[None, :], n_e), axis=1)
    nxt_e = jnp.where(nxt_e == n_e, -1, nxt_e)
    blk_nxt = jnp.sum(jnp.where(onehot, nxt_e[None, :], 0), axis=1)
    n_live = ends[-1:] // EXPERT_BLOCK
    i32 = lambda a: a.astype(jnp.int32)
    tables = (i32(cnt).reshape(-1), i32(lstart * rc).reshape(-1), i32(gstart * rc).reshape(-1))
    return tables, i32(blk_e), i32(blk_nv), i32(blk_nxt), i32(n_live)


def _moe_ln(h1, layer, w_router, b_router, w_gate, w_up, w_down, ws_gate, ws_up, ws_down,
            ln_g, ln_b, alpha, tm):
    t, d = h1.shape
    n_e = w_router.shape[1]
    n_blocks = t * TOP_K // EXPERT_BLOCK + n_e
    rc = d // 2 // LANES
    ws_gu = jnp.concatenate([ws_gate, ws_up], axis=1).astype(BF16)
    lpos, gate, cnt, shared, lists = _router(h1, w_router.T.astype(BF16), b_router, ws_gu,
                                             ws_down.astype(BF16), tm)
    tables, blk_e, blk_nv, blk_nxt, n_live = _moe_tables(cnt, n_blocks, rc)
    lpos = (lpos * rc).T.reshape(-1)
    gate = gate.T.reshape(-1)
    xs = _dispatch(h1, lpos, tables, lists, n_blocks * EXPERT_BLOCK, tm, n_e)
    ys = _experts(xs, blk_e, blk_nv, blk_nxt, n_live, w_gate, w_up, w_down, layer, d)
    return _combine_ln(ys, lpos, gate, tables, lists, h1, shared, ln_g, ln_b, alpha, tm, n_e)


def _tiles(t, s):
    return {"mm_m": min(1024, t), "mm_n": 512, "glu_n": 1024, "proj_m": min(512, t),
            "seq": min(512, s),
            "moe": min(256, t)}


def kernel(x, p, a_w_in, a_b_in, a_w_dw, a_b_dw, a_ln_g, a_ln_b, a_w_out, a_b_out, b_w_in, b_w_conv, b_w_out, c_w_grp, c_scale, ln1_g, ln1_b, ln2_g, ln2_b, router_w, router_b, exp_w_gate, exp_w_up, exp_w_down, sh_w_gate, sh_w_up, sh_w_down, ple_w_proj, ple_w_gate, ple_ln_g, ple_ln_b):
    bsz, s, d = x.shape
    depth = ln1_g.shape[0]
    t = bsz * s
    assert (t * TOP_K) % EXPERT_BLOCK == 0 and d % (2 * LANES) == 0
    alpha = (2 * depth) ** 0.25
    tl = _tiles(t, s)
    bn = min(tl["mm_n"], d)
    h = x.reshape(t, d)
    p_rows = p.reshape(depth, t, p.shape[-1])
    for i in range(depth):
        kind, j = i % 3, i // 3
        if kind == 0:
            u = _glu_proj(h, a_w_in[j].astype(BF16), a_b_in[j], tl["mm_m"], min(tl["glu_n"], d))
            v = _conv_ln_silu(u.reshape(bsz, s, d), a_w_dw[j], a_b_dw[j], a_ln_g[j], a_ln_b[j],
                              tl["seq"])
            h1 = _proj_residual_ln(v.reshape(t, d), a_w_out[j].astype(BF16), a_b_out[j], h,
                                   ln1_g[i], ln1_b[i], alpha, tl["proj_m"])
        elif kind == 1:
            bg, cv = _bcv_proj(h, b_w_in[j].astype(BF16), tl["mm_m"], bn)
            v = _gated_short_conv(bg.reshape(bsz, s, d), cv.reshape(bsz, s, d), b_w_conv[j],
                                  tl["seq"])
            h1 = _proj_residual_ln(v.reshape(t, d), b_w_out[j].astype(BF16),
                                   jnp.zeros((d,), F32), h, ln1_g[i], ln1_b[i], alpha,
                                   tl["proj_m"])
        else:
            h1 = _pool_mixer_ln(h.reshape(bsz, s, d), c_w_grp[j].astype(BF16), c_scale[j],
                                ln1_g[i], ln1_b[i], alpha, tl["seq"]).reshape(t, d)
        h2 = _moe_ln(h1, i, router_w[i], router_b[i], exp_w_gate, exp_w_up, exp_w_down,
                     sh_w_gate[i], sh_w_up[i], sh_w_down[i], ln2_g[i], ln2_b[i], alpha,
                     tl["moe"])
        h = _ple(h2, p_rows, i, ple_w_proj[i].astype(BF16), ple_w_gate[i].astype(BF16),
                 ple_ln_g[i], ple_ln_b[i], tl["proj_m"])
    return h.reshape(bsz, s, d)
```

```python
import functools

import jax
import jax.numpy as jnp
from jax import lax
from jax.experimental import pallas as pl
from jax.experimental.pallas import tpu as pltpu

LN_EPS = 1e-5
TOP_K = 8
N_EXPERT_GROUPS = 8
TOPK_GROUPS = 4
ROUTED_SCALE = 2.5
POOL_WINDOWS = (2, 4, 8, 16)
EXPERT_BLOCK = 512
EXPERT_SUBBLOCKS = 4
ROW_SUBBLOCKS = 4
PLACE_UNROLL = 4
GATHER_UNROLL = 8
LIST_ROWS = 16
LANES = 128
SUBLANES = 8
VMEM_LIMIT = 56 * 1024 * 1024

BF16 = jnp.bfloat16
PACKED = jnp.bfloat16
F32 = jnp.float32


def _params(*sem):
    return pltpu.CompilerParams(dimension_semantics=sem, vmem_limit_bytes=VMEM_LIMIT)


def _ln(x, g, b):
    mu = jnp.mean(x, axis=-1, keepdims=True)
    xc = x - mu
    var = jnp.mean(xc * xc, axis=-1, keepdims=True)
    return xc * lax.rsqrt(var + LN_EPS) * g + b


def _dot(a, b):
    return jnp.dot(a, b, preferred_element_type=F32)


def _silu(x):
    return x * jax.nn.sigmoid(x)


def _glu_kernel(x_ref, wa_ref, wg_ref, ba_ref, bg_ref, o_ref):
    x = x_ref[...].astype(BF16)
    a = _dot(x, wa_ref[...]) + ba_ref[...]
    g = _dot(x, wg_ref[...]) + bg_ref[...]
    o_ref[...] = a * jax.nn.sigmoid(g)


def _glu_proj(h, w_in, b_in, bm, bn):
    t, d = h.shape
    nb = d // bn
    b2 = b_in.reshape(1, 2 * d)
    return pl.pallas_call(
        _glu_kernel,
        grid=(t // bm, nb),
        in_specs=[
            pl.BlockSpec((bm, d), lambda i, j: (i, 0)),
            pl.BlockSpec((d, bn), lambda i, j: (0, j)),
            pl.BlockSpec((d, bn), lambda i, j: (0, j + nb)),
            pl.BlockSpec((1, bn), lambda i, j: (0, j)),
            pl.BlockSpec((1, bn), lambda i, j: (0, j + nb)),
        ],
        out_specs=pl.BlockSpec((bm, bn), lambda i, j: (i, j)),
        out_shape=jax.ShapeDtypeStruct((t, d), F32),
        compiler_params=_params("parallel", "arbitrary"),
        name="glu_proj",
    )(h, w_in, w_in, b2, b2)


def _bcv_kernel(x_ref, wb_ref, wc_ref, wv_ref, b_ref, cv_ref):
    x = x_ref[...].astype(BF16)
    b_ref[...] = _dot(x, wb_ref[...])
    cv_ref[...] = _dot(x, wc_ref[...]) * _dot(x, wv_ref[...])


def _bcv_proj(h, w_in, bm, bn):
    t, d = h.shape
    nb = d // bn
    return pl.pallas_call(
        _bcv_kernel,
        grid=(t // bm, nb),
        in_specs=[
            pl.BlockSpec((bm, d), lambda i, j: (i, 0)),
            pl.BlockSpec((d, bn), lambda i, j: (0, j)),
            pl.BlockSpec((d, bn), lambda i, j: (0, j + nb)),
            pl.BlockSpec((d, bn), lambda i, j: (0, j + 2 * nb)),
        ],
        out_specs=[pl.BlockSpec((bm, bn), lambda i, j: (i, j))] * 2,
        out_shape=[jax.ShapeDtypeStruct((t, d), F32)] * 2,
        compiler_params=_params("parallel", "arbitrary"),
        name="bcv_proj",
    )(h, w_in, w_in, w_in)


def _fill_window(buf_ref, halo_ref, main_ref, halo):
    first = pl.program_id(1) == 0
    buf_ref[0:halo, :] = jnp.where(first, 0.0, halo_ref[0])
    buf_ref[halo:, :] = main_ref[0]


def _conv_ln_kernel(u_ref, halo_ref, w_ref, bdw_ref, g_ref, b_ref, o_ref, buf_ref, acc_ref,
                    *, width, halo, ts, rows):
    _fill_window(buf_ref, halo_ref, u_ref, halo)
    d = buf_ref.shape[1]
    off = halo - (width - 1)

    def chunk(r, _):
        r0 = pl.multiple_of(r * rows, rows)
        for c in range(d // LANES):
            cs = slice(c * LANES, (c + 1) * LANES)
            win = buf_ref[pl.ds(r0, rows + halo), cs]
            acc = jnp.zeros((rows, LANES), F32)
            for s in range(SUBLANES):
                taps = [k for k in range(width) if (off + k) % SUBLANES == s]
                if not taps:
                    continue
                ws = pltpu.roll(win, rows + halo - s, axis=0) if s else win
                for k in taps:
                    q = off + k - s
                    acc = acc + w_ref[k:k + 1, cs] * ws[q:q + rows]
            acc_ref[pl.ds(r0, rows), cs] = acc
        return 0

    lax.fori_loop(0, ts // rows, chunk, 0)
    v = _ln(acc_ref[...] + bdw_ref[...], g_ref[...], b_ref[...])
    o_ref[0] = _silu(v).astype(o_ref.dtype)


def _conv_ln_silu(u, w_dw, b_dw, ln_g, ln_b, ts):
    bsz, s, d = u.shape
    width = w_dw.shape[0]
    halo = -(-(width - 1) // SUBLANES) * SUBLANES
    hb = ts // halo
    kern = functools.partial(_conv_ln_kernel, width=width, halo=halo, ts=ts, rows=32)
    vec = lambda a: a.reshape(1, d)
    return pl.pallas_call(
        kern,
        grid=(bsz, s // ts),
        in_specs=[
            pl.BlockSpec((1, ts, d), lambda b, i: (b, i, 0)),
            pl.BlockSpec((1, halo, d), lambda b, i: (b, jnp.maximum(i * hb - 1, 0), 0)),
            pl.BlockSpec((width, d), lambda b, i: (0, 0)),
            pl.BlockSpec((1, d), lambda b, i: (0, 0)),
            pl.BlockSpec((1, d), lambda b, i: (0, 0)),
            pl.BlockSpec((1, d), lambda b, i: (0, 0)),
        ],
        out_specs=pl.BlockSpec((1, ts, d), lambda b, i: (b, i, 0)),
        out_shape=jax.ShapeDtypeStruct((bsz, s, d), BF16),
        scratch_shapes=[pltpu.VMEM((halo + ts, d), F32), pltpu.VMEM((ts, d), F32)],
        compiler_params=_params("parallel", "arbitrary"),
        name="conv_ln_silu",
    )(u, u, w_dw, vec(b_dw), vec(ln_g), vec(ln_b))


def _gated_conv_kernel(cv_ref, halo_ref, bg_ref, w_ref, o_ref, buf_ref, *, width, halo, ts):
    _fill_window(buf_ref, halo_ref, cv_ref, halo)
    full = buf_ref[...]
    acc = None
    for k in range(width):
        back = width - 1 - k
        rows = (pltpu.roll(full, back, axis=0) if back else full)[halo:, :]
        term = w_ref[k:k + 1, :] * rows
        acc = term if acc is None else acc + term
    o_ref[0] = (bg_ref[0] * acc).astype(o_ref.dtype)


def _gated_short_conv(bg, cv, w_conv, ts):
    bsz, s, d = cv.shape
    width = w_conv.shape[0]
    halo = SUBLANES
    hb = ts // halo
    kern = functools.partial(_gated_conv_kernel, width=width, halo=halo, ts=ts)
    return pl.pallas_call(
        kern,
        grid=(bsz, s // ts),
        in_specs=[
            pl.BlockSpec((1, ts, d), lambda b, i: (b, i, 0)),
            pl.BlockSpec((1, halo, d), lambda b, i: (b, jnp.maximum(i * hb - 1, 0), 0)),
            pl.BlockSpec((1, ts, d), lambda b, i: (b, i, 0)),
            pl.BlockSpec((width, d), lambda b, i: (0, 0)),
        ],
        out_specs=pl.BlockSpec((1, ts, d), lambda b, i: (b, i, 0)),
        out_shape=jax.ShapeDtypeStruct((bsz, s, d), BF16),
        scratch_shapes=[pltpu.VMEM((halo + ts, d), F32)],
        compiler_params=_params("parallel", "arbitrary"),
        name="gated_short_conv",
    )(cv, cv, bg, w_conv)


def _skewed(n_sub, *stages):
    for step in range(n_sub + len(stages) - 1):
        for lag, stage in enumerate(stages):
            if 0 <= step - lag < n_sub:
                stage(step - lag)


def _proj_ln_kernel(v_ref, w_ref, bias_ref, h_ref, g_ref, b_ref, o_ref, *, alpha):
    sub = v_ref.shape[0] // ROW_SUBBLOCKS
    acc = {}

    def matmul(i):
        acc[i] = _dot(v_ref[pl.ds(i * sub, sub), :], w_ref[...])

    def epilogue(i):
        rows = pl.ds(i * sub, sub)
        m = acc[i] + bias_ref[...]
        o_ref[rows, :] = _ln(alpha * h_ref[rows, :] + m, g_ref[...], b_ref[...])

    _skewed(ROW_SUBBLOCKS, matmul, epilogue)


def _proj_residual_ln(v, w_out, b_out, h, ln_g, ln_b, alpha, bm):
    t, d = h.shape
    vec = lambda a: a.reshape(1, d)
    return pl.pallas_call(
        functools.partial(_proj_ln_kernel, alpha=alpha),
        grid=(t // bm,),
        in_specs=[
            pl.BlockSpec((bm, d), lambda i: (i, 0)),
            pl.BlockSpec((d, d), lambda i: (0, 0)),
            pl.BlockSpec((1, d), lambda i: (0, 0)),
            pl.BlockSpec((bm, d), lambda i: (i, 0)),
            pl.BlockSpec((1, d), lambda i: (0, 0)),
            pl.BlockSpec((1, d), lambda i: (0, 0)),
        ],
        out_specs=pl.BlockSpec((bm, d), lambda i: (i, 0)),
        out_shape=jax.ShapeDtypeStruct((t, d), F32),
        compiler_params=_params("parallel"),
        name="proj_residual_ln",
    )(v, w_out, vec(b_out), h, vec(ln_g), vec(ln_b))


def _pool_kernel(h_ref, halo_ref, w_ref, scale_ref, g_ref, b_ref, o_ref, buf_ref,
                 *, halo, ts, alpha):
    _fill_window(buf_ref, halo_ref, h_ref, halo)
    d = buf_ref.shape[1]
    gd = d // len(POOL_WINDOWS)
    pos = pl.program_id(1) * ts + lax.broadcasted_iota(jnp.int32, (ts, 1), 0) + 1
    ys = []
    for gi, win in enumerate(POOL_WINDOWS):
        cs = slice(gi * gd, (gi + 1) * gd)
        full = buf_ref[:, cs]
        tot, span = full, 1
        while span < win:
            tot = tot + pltpu.roll(tot, span, axis=0)
            span *= 2
        x, tot = full[halo:, :], tot[halo:, :]
        cnt = jnp.minimum(pos, win).astype(F32)
        z = tot / cnt - x
        ys.append(_dot(z.astype(BF16), w_ref[gi]))
    y = jnp.concatenate(ys, axis=-1) * scale_ref[...]
    o_ref[0] = _ln(alpha * h_ref[0] + y, g_ref[...], b_ref[...])


def _pool_mixer_ln(h, w_grp, scale, ln_g, ln_b, alpha, ts):
    bsz, s, d = h.shape
    halo = max(POOL_WINDOWS)
    hb = ts // halo
    vec = lambda a: a.reshape(1, d)
    return pl.pallas_call(
        functools.partial(_pool_kernel, halo=halo, ts=ts, alpha=alpha),
        grid=(bsz, s // ts),
        in_specs=[
            pl.BlockSpec((1, ts, d), lambda b, i: (b, i, 0)),
            pl.BlockSpec((1, halo, d), lambda b, i: (b, jnp.maximum(i * hb - 1, 0), 0)),
            pl.BlockSpec(w_grp.shape, lambda b, i: (0, 0, 0)),
            pl.BlockSpec((1, d), lambda b, i: (0, 0)),
            pl.BlockSpec((1, d), lambda b, i: (0, 0)),
            pl.BlockSpec((1, d), lambda b, i: (0, 0)),
        ],
        out_specs=pl.BlockSpec((1, ts, d), lambda b, i: (b, i, 0)),
        out_shape=jax.ShapeDtypeStruct((bsz, s, d), F32),
        scratch_shapes=[pltpu.VMEM((halo + ts, d), F32)],
        compiler_params=_params("parallel", "arbitrary"),
        name="pool_mixer_ln",
    )(h, h, w_grp, vec(scale), vec(ln_g), vec(ln_b))


def _rank_desc(m, side_work=()):
    n = m.shape[0]
    row = lax.broadcasted_iota(jnp.int32, m.shape, 0)
    rank = jnp.zeros(m.shape, jnp.int32)
    every = n // len(side_work) if side_work else 0
    for j in range(n):
        if side_work and j % every == 0 and j // every < len(side_work):
            side_work[j // every]()
        mj = m[j:j + 1, :]
        beats = (mj > m) | ((mj == m) & (row > j))
        rank = rank + beats.astype(jnp.int32)
    return rank


def _router_kernel(h_ref, wt_ref, b_ref, wsgu_ref, wsd_ref, lpos_ref, gate_ref, cnt_ref,
                   shared_ref, list_ref, *, tm):
    n_e = wt_ref.shape[0]
    epg = n_e // N_EXPERT_GROUPS
    x = h_ref[...].astype(BF16)
    f = wsd_ref.shape[0]
    d = wsd_ref.shape[1]
    mid = []

    def shared_up():
        su = _dot(x, wsgu_ref[...])
        mid.append((_silu(su[:, :f]) * su[:, f:]).astype(BF16))

    def shared_down(c, parts=4):
        def run():
            cs = slice(c * d // parts, (c + 1) * d // parts)
            shared_ref[:, cs] = _dot(mid[0], wsd_ref[:, cs])
        return run

    side_work = [shared_up] + [shared_down(c) for c in range(4)]
    logits = lax.dot_general(wt_ref[...], x, (((1,), (1,)), ((), ())),
                             preferred_element_type=F32)
    scores = jax.nn.sigmoid(logits)
    biased = scores + b_ref[...]

    sub = lax.broadcasted_iota(jnp.int32, (epg, tm), 0)
    gscore = []
    for g in range(N_EXPERT_GROUPS):
        blk = biased[g * epg:(g + 1) * epg, :]
        m1 = jnp.max(blk, axis=0, keepdims=True)
        first = jnp.min(jnp.where(blk == m1, sub, epg), axis=0, keepdims=True)
        m2 = jnp.max(jnp.where(sub == first, -jnp.inf, blk), axis=0, keepdims=True)
        gscore.append(m1 + m2)
    gsel = _rank_desc(jnp.concatenate(gscore, axis=0)) < TOPK_GROUPS
    emask = jnp.concatenate(
        [jnp.broadcast_to(gsel[g:g + 1, :], (epg, tm)) for g in range(N_EXPERT_GROUPS)], axis=0)
    masked = jnp.where(emask, biased, -jnp.inf)
    sel = _rank_desc(masked, side_work) < TOP_K
    self32 = sel.astype(F32)
    selb = self32.astype(BF16)

    gate = jnp.where(sel, scores, 0.0)
    gate = gate / jnp.sum(gate, axis=0, keepdims=True) * ROUTED_SCALE

    er = lax.broadcasted_iota(jnp.int32, (n_e, n_e), 0)
    ec = lax.broadcasted_iota(jnp.int32, (n_e, n_e), 1)
    lower = (ec < er).astype(BF16)
    slot = _dot(lower, selb)
    tr = lax.broadcasted_iota(jnp.int32, (tm, tm), 0)
    tc = lax.broadcasted_iota(jnp.int32, (tm, tm), 1)
    before = (tr < tc).astype(BF16)
    lrank = _dot(selb, before)
    count = jnp.sum(self32, axis=1, keepdims=True)
    count_l = jnp.broadcast_to(count, (n_e, LANES))
    lstart = _dot(lower, count_l.astype(BF16))[:, 0:1]
    lpos = lstart + lrank

    lpos_rows, gate_rows = [], []
    for k in range(TOP_K):
        pick = sel & (slot == float(k))
        lpos_rows.append(jnp.sum(jnp.where(pick, lpos, 0.0), axis=0, keepdims=True))
        gate_rows.append(jnp.sum(jnp.where(pick, gate, 0.0), axis=0, keepdims=True))
    lpos_ref[...] = jnp.concatenate(lpos_rows, axis=0).astype(jnp.int32)
    gate_ref[...] = jnp.concatenate(gate_rows, axis=0)
    cnt_ref[0] = count_l.astype(jnp.int32)

    n_bits = tm.bit_length()
    counts_i = count_l.astype(jnp.int32)
    lane = lax.broadcasted_iota(jnp.int32, (n_e, LANES), 1)
    erow_l = lax.broadcasted_iota(jnp.int32, (n_e, LANES), 0)
    lane_row = lax.broadcasted_iota(jnp.int32, (1, LANES), 1)
    rows, lengths = [], jnp.zeros((1, LANES), jnp.int32)
    for b in range(n_bits):
        bit = lax.shift_right_logical(counts_i, b) & 1
        below = _dot(lower, bit.astype(BF16)).astype(jnp.int32)
        hit = (bit == 1) & (below == lane)
        rows.append(jnp.sum(jnp.where(hit, erow_l, 0), axis=0, keepdims=True))
        total_b = jnp.sum(bit[:, 0:1], axis=0, keepdims=True)
        lengths = lengths + jnp.where(lane_row == b, total_b, 0)
    rows.append(lengths)
    rows.append(jnp.zeros((LIST_ROWS - n_bits - 1, LANES), jnp.int32))
    list_ref[0] = jnp.concatenate(rows, axis=0)


def _router(h1, w_router_t, b_router, ws_gu, ws_down, tm):
    t, d = h1.shape
    n_e = w_router_t.shape[0]
    f = ws_down.shape[0]
    assert tm <= 256, "tile counts must stay exactly representable in bf16"
    assert tm.bit_length() < LIST_ROWS and n_e <= LANES
    lpos, gate, cnt, shared, lists = pl.pallas_call(
        functools.partial(_router_kernel, tm=tm),
        grid=(t // tm,),
        in_specs=[
            pl.BlockSpec((tm, d), lambda i: (i, 0)),
            pl.BlockSpec((n_e, d), lambda i: (0, 0)),
            pl.BlockSpec((n_e, 1), lambda i: (0, 0)),
            pl.BlockSpec((d, 2 * f), lambda i: (0, 0)),
            pl.BlockSpec((f, d), lambda i: (0, 0)),
        ],
        out_specs=[
            pl.BlockSpec((TOP_K, tm), lambda i: (0, i)),
            pl.BlockSpec((TOP_K, tm), lambda i: (0, i)),
            pl.BlockSpec((1, n_e, LANES), lambda i: (i, 0, 0)),
            pl.BlockSpec((tm, d), lambda i: (i, 0)),
            pl.BlockSpec((1, LIST_ROWS, LANES), lambda i: (i, 0, 0)),
        ],
        out_shape=[
            jax.ShapeDtypeStruct((TOP_K, t), jnp.int32),
            jax.ShapeDtypeStruct((TOP_K, t), F32),
            jax.ShapeDtypeStruct((t // tm, n_e, LANES), jnp.int32),
            jax.ShapeDtypeStruct((t, d), F32),
            jax.ShapeDtypeStruct((t // tm, LIST_ROWS, LANES), jnp.int32),
        ],
        compiler_params=_params("parallel"),
        name="router",
    )(h1, w_router_t, b_router.reshape(n_e, 1), ws_gu, ws_down)
    return lpos, gate, cnt[:, :, 0], shared, lists.reshape(-1)


HIGH_HALF = -65536


def _pack_words(lo, hi):
    bits = lambda v: lax.bitcast_convert_type(v.astype(PACKED).astype(F32), jnp.int32)
    return (bits(hi) & HIGH_HALF) | lax.shift_right_logical(bits(lo), 16)


def _word_lo(w):
    return lax.bitcast_convert_type(lax.shift_left(w, 16), F32)


def _word_hi(w):
    return lax.bitcast_convert_type(w & HIGH_HALF, F32)


def _pack_rows(x, dst_ref, r0=0):
    n, half = x.shape[0], x.shape[1] // 2
    rc = half // LANES
    words = _pack_words(x[:, :half], x[:, half:])
    for c in range(rc):
        dst_ref[pl.ds(r0 * rc + c, n, stride=rc), :] = words[:, c * LANES:(c + 1) * LANES]


def _unpack_rows(src_ref, n, rc, r0=0):
    lo, hi = [], []
    for c in range(rc):
        w = src_ref[pl.ds(r0 * rc + c, n, stride=rc), :]
        lo.append(_word_lo(w))
        hi.append(_word_hi(w))
    return jnp.concatenate(lo + hi, axis=-1)


def _pack_planes(x, dst_ref, r0):
    n, half = x.shape[0], x.shape[1] // 2
    words = _pack_words(x[:, :half], x[:, half:])
    for c in range(half // LANES):
        dst_ref[c, pl.ds(r0, n), :] = words[:, c * LANES:(c + 1) * LANES]


def _unpack_planes(src_ref, n, r0):
    lo, hi = [], []
    for c in range(src_ref.shape[0]):
        w = src_ref[c, pl.ds(r0, n), :]
        lo.append(_word_lo(w))
        hi.append(_word_hi(w))
    return jnp.concatenate(lo + hi, axis=-1)


def _segment_copies(local_ref, hbm_ref, sem, tables, list_ref, tile, n_e, rc, tm, to_hbm):
    cnt_ref, lstart_ref, gstart_ref = tables
    n_bits = tm.bit_length()
    for b in range(n_bits):
        size = 1 << b

        def copy_one(j, _, b=b, size=size):
            idx = tile * n_e + list_ref[b * LANES + j]
            n, ls, gs = cnt_ref[idx], lstart_ref[idx], gstart_ref[idx]
            done = (n & ~(2 * size - 1)) * rc
            lo = pl.multiple_of(ls + done, rc)
            go = pl.multiple_of(gs + done, rc)
            loc = local_ref.at[pl.ds(lo, size * rc), :]
            glob = hbm_ref.at[pl.ds(go, size * rc), :]
            if to_hbm:
                pltpu.make_async_copy(loc, glob, sem).start(priority=b % 2)
            else:
                pltpu.make_async_copy(glob, loc, sem).start(priority=b % 2)
            return 0

        lax.fori_loop(0, list_ref[n_bits * LANES + b], copy_one, 0)


def _wait_segments(local_ref, hbm_ref, sem):
    rows = local_ref.shape[0]
    pltpu.make_async_copy(local_ref, hbm_ref.at[pl.ds(0, rows), :], sem).wait()


def _dispatch_kernel(cnt_ref, lstart_ref, gstart_ref, lpos_ref, list_ref, h_ref, xs_ref,
                     q_ref, s_ref, sem, *, tm, n_e, rc):
    i = pl.program_id(0)
    n = pl.num_programs(0)
    slot = i % 2

    @pl.when(i >= 2)
    def _():
        _wait_segments(s_ref.at[slot], xs_ref, sem.at[slot])

    _pack_rows(h_ref[...], q_ref)

    def place_into(sl):
        def place(j, _):
            for u in range(PLACE_UNROLL):
                t = j * PLACE_UNROLL + u
                row = q_ref[pl.ds(pl.multiple_of(t * rc, rc), rc), :]
                for k in range(TOP_K):
                    dst = pl.multiple_of(lpos_ref[t * TOP_K + k], rc)
                    s_ref[sl, pl.ds(dst, rc), :] = row
            return 0

        lax.fori_loop(0, tm // PLACE_UNROLL, place, 0)

    for sl in range(2):
        pl.when(slot == sl)(functools.partial(place_into, sl))
    _segment_copies(s_ref.at[slot], xs_ref, sem.at[slot], (cnt_ref, lstart_ref, gstart_ref),
                    list_ref, i, n_e, rc, tm, to_hbm=True)

    @pl.when(i == n - 1)
    def _():
        _wait_segments(s_ref.at[slot], xs_ref, sem.at[slot])

        @pl.when(n > 1)
        def _():
            _wait_segments(s_ref.at[1 - slot], xs_ref, sem.at[1 - slot])


def _dispatch(h1, lpos, tables, lists, n_rows, tm, n_e):
    t, d = h1.shape
    rc = d // 2 // LANES
    grid_spec = pltpu.PrefetchScalarGridSpec(
        num_scalar_prefetch=3,
        grid=(t // tm,),
        in_specs=[
            pl.BlockSpec((TOP_K * tm,), lambda i, *_: (i,), memory_space=pltpu.SMEM),
            pl.BlockSpec((LIST_ROWS * LANES,), lambda i, *_: (i,), memory_space=pltpu.SMEM),
            pl.BlockSpec((tm, d), lambda i, *_: (i, 0)),
        ],
        out_specs=pl.BlockSpec(memory_space=pl.ANY),
        scratch_shapes=[
            pltpu.VMEM((tm * rc, LANES), jnp.int32),
            pltpu.VMEM((2, TOP_K * tm * rc, LANES), jnp.int32),
            pltpu.SemaphoreType.DMA((2,)),
        ],
    )
    return pl.pallas_call(
        functools.partial(_dispatch_kernel, tm=tm, n_e=n_e, rc=rc),
        grid_spec=grid_spec,
        out_shape=jax.ShapeDtypeStruct((n_rows * rc, LANES), jnp.int32),
        compiler_params=_params("arbitrary"),
        name="dispatch",
    )(*tables, lpos, lists, h1)


def _expert_kernel(be_ref, nv_ref, nxt_ref, live_ref, xs_ref, wg_ref, wu_ref, wd_ref, ys_ref,
                   xbuf, ybuf, wg_f, wu_f, wd_f, wgu_s, wd_s, sem_x, sem_y, sem_w,
                   *, layer, blk, rc):
    n_live = live_ref[0]
    f = wd_s.shape[0]

    def x_copies(g, slot):
        r0 = pl.multiple_of(g * blk, blk)
        return [pltpu.make_async_copy(xs_ref.at[pl.ds(r0, blk), c, :], xbuf.at[slot, c],
                                      sem_x.at[slot]) for c in range(rc)]

    def y_copies(g, slot):
        r0 = pl.multiple_of(g * blk, blk)
        return [pltpu.make_async_copy(ybuf.at[slot, c], ys_ref.at[pl.ds(r0, blk), c, :],
                                      sem_y.at[slot]) for c in range(rc)]

    def w_copies(e, slot):
        return (pltpu.make_async_copy(wg_ref.at[layer, e], wg_f.at[slot], sem_w.at[slot]),
                pltpu.make_async_copy(wu_ref.at[layer, e], wu_f.at[slot], sem_w.at[slot]),
                pltpu.make_async_copy(wd_ref.at[layer, e], wd_f.at[slot], sem_w.at[slot]))

    def start(copies):
        for cp in copies:
            cp.start()

    def wait(copies):
        for cp in copies:
            cp.wait()

    start(x_copies(0, 0))
    start(w_copies(be_ref[0], 0))

    def block(g, wslot):
        slot = g % 2
        e = be_ref[g]
        fresh = (g == 0) | (e != be_ref[jnp.maximum(g - 1, 0)])

        @pl.when(fresh)
        def _():
            wait(w_copies(e, wslot))
            nxt = nxt_ref[g]

            @pl.when(nxt >= 0)
            def _():
                start(w_copies(nxt, 1 - wslot))

            wgu_s[:, :f] = wg_f[wslot].astype(BF16)
            wgu_s[:, f:] = wu_f[wslot].astype(BF16)
            wd_s[...] = wd_f[wslot].astype(BF16)

        wait(x_copies(g, slot))

        @pl.when(g + 1 < n_live)
        def _():
            start(x_copies(g + 1, 1 - slot))

        @pl.when(g >= 2)
        def _():
            wait(y_copies(g - 2, slot))

        nv = nv_ref[g]
        xin, yout = xbuf.at[slot], ybuf.at[slot]
        sub = blk // EXPERT_SUBBLOCKS
        gu, y = {}, {}

        def up(sb):
            x = _unpack_planes(xin, sub, sb * sub).astype(BF16)
            gu[sb] = _dot(x, wgu_s[...])

        def act(sb):
            mid = _silu(gu[sb][:, :f]) * gu[sb][:, f:]
            row = sb * sub + lax.broadcasted_iota(jnp.int32, (sub, 1), 0)
            gu[sb] = jnp.where(row < nv, mid, 0.0).astype(BF16)

        def down(sb):
            y[sb] = _dot(gu[sb], wd_s[...])

        def out(sb):
            _pack_planes(y[sb], yout, sb * sub)

        _skewed(EXPERT_SUBBLOCKS, up, act, down, out)
        start(y_copies(g, slot))
        return jnp.where(fresh, 1 - wslot, wslot)

    lax.fori_loop(0, n_live, block, 0)
    last = n_live - 1
    wait(y_copies(last, last % 2))

    @pl.when(n_live >= 2)
    def _():
        wait(y_copies(last - 1, (last - 1) % 2))


def _experts(xs, blk_e, blk_nv, blk_nxt, n_live, w_gate, w_up, w_down, layer, d):
    rc = d // 2 // LANES
    f = w_gate.shape[-1]
    n_rows = xs.shape[0] // rc
    anyspace = pl.BlockSpec(memory_space=pl.ANY)
    grid_spec = pltpu.PrefetchScalarGridSpec(
        num_scalar_prefetch=4,
        grid=(1,),
        in_specs=[anyspace] * 4,
        out_specs=anyspace,
        scratch_shapes=[
            pltpu.VMEM((2, rc, EXPERT_BLOCK, LANES), jnp.int32),
            pltpu.VMEM((2, rc, EXPERT_BLOCK, LANES), jnp.int32),
            pltpu.VMEM((2, d, f), F32),
            pltpu.VMEM((2, d, f), F32),
            pltpu.VMEM((2, f, d), F32),
            pltpu.VMEM((d, 2 * f), BF16),
            pltpu.VMEM((f, d), BF16),
            pltpu.SemaphoreType.DMA((2,)),
            pltpu.SemaphoreType.DMA((2,)),
            pltpu.SemaphoreType.DMA((2,)),
        ],
    )
    ys = pl.pallas_call(
        functools.partial(_expert_kernel, layer=layer, blk=EXPERT_BLOCK, rc=rc),
        grid_spec=grid_spec,
        out_shape=jax.ShapeDtypeStruct((n_rows, rc, LANES), jnp.int32),
        compiler_params=_params("arbitrary"),
        name="experts",
    )(blk_e, blk_nv, blk_nxt, n_live, xs.reshape(n_rows, rc, LANES), w_gate, w_up, w_down)
    return ys.reshape(xs.shape)


def _combine_kernel(cnt_ref, lstart_ref, gstart_ref, lpos_ref, gate_ref, list0_ref, list1_ref,
                    h_ref, shared_ref, g_ref, b_ref, ys_ref, o_ref, l_ref, r_ref, sem,
                    *, tm, n_e, rc, alpha):
    i = pl.program_id(0)
    n = pl.num_programs(0)
    slot = i % 2
    tables = (cnt_ref, lstart_ref, gstart_ref)

    @pl.when(i == 0)
    def _():
        _segment_copies(l_ref.at[0], ys_ref, sem.at[0], tables, list0_ref, 0, n_e, rc, tm,
                        to_hbm=False)

    @pl.when(i + 1 < n)
    def _():
        _segment_copies(l_ref.at[1 - slot], ys_ref, sem.at[1 - slot], tables, list1_ref, i + 1,
                        n_e, rc, tm, to_hbm=False)

    _wait_segments(l_ref.at[slot], ys_ref, sem.at[slot])

    def gather_from(sl):
        def gather(j, _):
            for u in range(GATHER_UNROLL):
                t = j * GATHER_UNROLL + u
                lo = hi = None
                for k in range(TOP_K):
                    src = pl.multiple_of(lpos_ref[t * TOP_K + k], rc)
                    w = l_ref[sl, pl.ds(src, rc), :]
                    gk = gate_ref[t * TOP_K + k]
                    lo = gk * _word_lo(w) if lo is None else lo + gk * _word_lo(w)
                    hi = gk * _word_hi(w) if hi is None else hi + gk * _word_hi(w)
                base = pl.multiple_of(t * 2 * rc, 2 * rc)
                r_ref[pl.ds(base, rc), :] = lo
                r_ref[pl.ds(base + rc, rc), :] = hi
            return 0

        lax.fori_loop(0, tm // GATHER_UNROLL, gather, 0)

    for sl in range(2):
        pl.when(slot == sl)(functools.partial(gather_from, sl))
    routed = jnp.concatenate(
        [r_ref[pl.ds(c, tm, stride=2 * rc), :] for c in range(2 * rc)], axis=-1)
    o_ref[...] = _ln(alpha * h_ref[...] + (routed + shared_ref[...]), g_ref[...], b_ref[...])


def _combine_ln(ys, lpos, gate, tables, lists, h1, shared, ln_g, ln_b, alpha, tm, n_e):
    t, d = h1.shape
    rc = d // 2 // LANES
    last = t // tm - 1
    vec = lambda a: a.reshape(1, d)
    grid_spec = pltpu.PrefetchScalarGridSpec(
        num_scalar_prefetch=3,
        grid=(t // tm,),
        in_specs=[
            pl.BlockSpec((TOP_K * tm,), lambda i, *_: (i,), memory_space=pltpu.SMEM),
            pl.BlockSpec((TOP_K * tm,), lambda i, *_: (i,), memory_space=pltpu.SMEM),
            pl.BlockSpec((LIST_ROWS * LANES,), lambda i, *_: (0,), memory_space=pltpu.SMEM),
            pl.BlockSpec((LIST_ROWS * LANES,), lambda i, *_: (jnp.minimum(i + 1, last),),
                         memory_space=pltpu.SMEM),
            pl.BlockSpec((tm, d), lambda i, *_: (i, 0)),
            pl.BlockSpec((tm, d), lambda i, *_: (i, 0)),
            pl.BlockSpec((1, d), lambda i, *_: (0, 0)),
            pl.BlockSpec((1, d), lambda i, *_: (0, 0)),
            pl.BlockSpec(memory_space=pl.ANY),
        ],
        out_specs=pl.BlockSpec((tm, d), lambda i, *_: (i, 0)),
        scratch_shapes=[
            pltpu.VMEM((2, TOP_K * tm * rc, LANES), jnp.int32),
            pltpu.VMEM((tm * 2 * rc, LANES), F32),
            pltpu.SemaphoreType.DMA((2,)),
        ],
    )
    return pl.pallas_call(
        functools.partial(_combine_kernel, tm=tm, n_e=n_e, rc=rc, alpha=alpha),
        grid_spec=grid_spec,
        out_shape=jax.ShapeDtypeStruct((t, d), F32),
        compiler_params=_params("arbitrary"),
        name="combine_ln",
    )(*tables, lpos, gate, lists, lists, h1, shared, vec(ln_g), vec(ln_b), ys)


def _ple_kernel(h_ref, p_ref, wp_ref, wg_ref, g_ref, b_ref, o_ref):
    sub = h_ref.shape[0] // ROW_SUBBLOCKS
    acc = {}

    def matmul(i):
        rows = pl.ds(i * sub, sub)
        e = _dot(p_ref[0, rows, :].astype(BF16), wp_ref[...])
        acc[i] = (e, _dot(h_ref[rows, :].astype(BF16), wg_ref[...]))

    def epilogue(i):
        rows = pl.ds(i * sub, sub)
        e, logit = acc[i]
        o_ref[rows, :] = h_ref[rows, :] + _ln(jax.nn.sigmoid(logit) * e, g_ref[...], b_ref[...])

    _skewed(ROW_SUBBLOCKS, matmul, epilogue)


def _ple(h2, p, layer, w_proj, w_gate, ln_g, ln_b, bm):
    t, d = h2.shape
    pd = p.shape[-1]
    vec = lambda a: a.reshape(1, d)
    return pl.pallas_call(
        _ple_kernel,
        grid=(t // bm,),
        in_specs=[
            pl.BlockSpec((bm, d), lambda i: (i, 0)),
            pl.BlockSpec((1, bm, pd), lambda i: (layer, i, 0)),
            pl.BlockSpec((pd, d), lambda i: (0, 0)),
            pl.BlockSpec((d, d), lambda i: (0, 0)),
            pl.BlockSpec((1, d), lambda i: (0, 0)),
            pl.BlockSpec((1, d), lambda i: (0, 0)),
        ],
        out_specs=pl.BlockSpec((bm, d), lambda i: (i, 0)),
        out_shape=jax.ShapeDtypeStruct((t, d), F32),
        compiler_params=_params("parallel"),
        name="ple",
    )(h2, p, w_proj, w_gate, vec(ln_g), vec(ln_b))


def _moe_tables(cnt, n_blocks, rc):
    n_e = cnt.shape[1]
    total = jnp.sum(cnt, axis=0)
    padded = (total + EXPERT_BLOCK - 1) // EXPERT_BLOCK * EXPERT_BLOCK
    ends = jnp.cumsum(padded)
    starts = ends - padded
    gstart = starts[None, :] + jnp.cumsum(cnt, axis=0) - cnt
    lstart = jnp.cumsum(cnt, axis=1) - cnt
    blk_start = jnp.arange(n_blocks, dtype=jnp.int32) * EXPERT_BLOCK
    blk_e = jnp.minimum(jnp.sum(ends[None, :] <= blk_start[:, None], axis=1), n_e - 1)
    onehot = blk_e[:, None] == jnp.arange(n_e)[None, :]
    used = jnp.sum(jnp.where(onehot, (starts + total)[None, :], 0), axis=1)
    blk_nv = jnp.clip(used - blk_start, 0, EXPERT_BLOCK)
    ids = jnp.arange(n_e)
    later = (ids[None, :] > ids[:, None]) & (total[None, :] > 0)
    nxt_e = jnp.min(jnp.where(later, ids[None, :], n_e), axis=1)
    nxt_e = jnp.where(nxt_e == n_e, -1, nxt_e)
    blk_nxt = jnp.sum(jnp.where(onehot, nxt_e[None, :], 0), axis=1)
    n_live = ends[-1:] // EXPERT_BLOCK
    i32 = lambda a: a.astype(jnp.int32)
    tables = (i32(cnt).reshape(-1), i32(lstart * rc).reshape(-1), i32(gstart * rc).reshape(-1))
    return tables, i32(blk_e), i32(blk_nv), i32(blk_nxt), i32(n_live)


def _moe_ln(h1, layer, w_router, b_router, w_gate, w_up, w_down, ws_gate, ws_up, ws_down,
            ln_g, ln_b, alpha, tm):
    t, d = h1.shape
    n_e = w_router.shape[1]
    n_blocks = t * TOP_K // EXPERT_BLOCK + n_e
    rc = d // 2 // LANES
    ws_gu = jnp.concatenate([ws_gate, ws_up], axis=1).astype(BF16)
    lpos, gate, cnt, shared, lists = _router(h1, w_router.T.astype(BF16), b_router, ws_gu,
                                             ws_down.astype(BF16), tm)
    tables, blk_e, blk_nv, blk_nxt, n_live = _moe_tables(cnt, n_blocks, rc)
    lpos = (lpos * rc).T.reshape(-1)
    gate = gate.T.reshape(-1)
    xs = _dispatch(h1, lpos, tables, lists, n_blocks * EXPERT_BLOCK, tm, n_e)
    ys = _experts(xs, blk_e, blk_nv, blk_nxt, n_live, w_gate, w_up, w_down, layer, d)
    return _combine_ln(ys, lpos, gate, tables, lists, h1, shared, ln_g, ln_b, alpha, tm, n_e)


def _tiles(t, s):
    return {"mm_m": min(1024, t), "mm_n": 512, "glu_n": 1024, "proj_m": min(512, t),
            "seq": min(512, s),
            "moe": min(256, t)}


def kernel(x, p, a_w_in, a_b_in, a_w_dw, a_b_dw, a_ln_g, a_ln_b, a_w_out, a_b_out, b_w_in, b_w_conv, b_w_out, c_w_grp, c_scale, ln1_g, ln1_b, ln2_g, ln2_b, router_w, router_b, exp_w_gate, exp_w_up, exp_w_down, sh_w_gate, sh_w_up, sh_w_down, ple_w_proj, ple_w_gate, ple_ln_g, ple_ln_b):
    bsz, s, d = x.shape
    depth = ln1_g.shape[0]
    t = bsz * s
    assert (t * TOP_K) % EXPERT_BLOCK == 0 and d % (2 * LANES) == 0
    alpha = (2 * depth) ** 0.25
    tl = _tiles(t, s)
    bn = min(tl["mm_n"], d)
    h = x.reshape(t, d)
    p_rows = p.reshape(depth, t, p.shape[-1])
    for i in range(depth):
        kind, j = i % 3, i // 3
        if kind == 0:
            u = _glu_proj(h, a_w_in[j].astype(BF16), a_b_in[j], tl["mm_m"], min(tl["glu_n"], d))
            v = _conv_ln_silu(u.reshape(bsz, s, d), a_w_dw[j], a_b_dw[j], a_ln_g[j], a_ln_b[j],
                              tl["seq"])
            h1 = _proj_residual_ln(v.reshape(t, d), a_w_out[j].astype(BF16), a_b_out[j], h,
                                   ln1_g[i], ln1_b[i], alpha, tl["proj_m"])
        elif kind == 1:
            bg, cv = _bcv_proj(h, b_w_in[j].astype(BF16), tl["mm_m"], bn)
            v = _gated_short_conv(bg.reshape(bsz, s, d), cv.reshape(bsz, s, d), b_w_conv[j],
                                  tl["seq"])
            h1 = _proj_residual_ln(v.reshape(t, d), b_w_out[j].astype(BF16),
                                   jnp.zeros((d,), F32), h, ln1_g[i], ln1_b[i], alpha,
                                   tl["proj_m"])
        else:
            h1 = _pool_mixer_ln(h.reshape(bsz, s, d), c_w_grp[j].astype(BF16), c_scale[j],
                                ln1_g[i], ln1_b[i], alpha, tl["seq"]).reshape(t, d)
        h2 = _moe_ln(h1, i, router_w[i], router_b[i], exp_w_gate, exp_w_up, exp_w_down,
                     sh_w_gate[i], sh_w_up[i], sh_w_down[i], ln2_g[i], ln2_b[i], alpha,
                     tl["moe"])
        h = _ple(h2, p_rows, i, ple_w_proj[i].astype(BF16), ple_w_gate[i].astype(BF16),
                 ple_ln_g[i], ple_ln_b[i], tl["proj_m"])
    return h.reshape(bsz, s, d)
```
